```python
import jax, jax.numpy as jnp
from jax import lax
import numpy as np

D_MODEL = 1024
BATCH = 8
SEQ = 8192
DEPTH = 1

HEAD_DIM = 64
DIL_GROUPS = ((128, 1), (512, 4), (2048, 16))
DIL_HEADS_PER_GROUP = 4
DIL_HEADS = DIL_HEADS_PER_GROUP * len(DIL_GROUPS)
DIL_OUT = DIL_HEADS_PER_GROUP * HEAD_DIM
SB_HEADS = 8
SB_OUT = SB_HEADS * HEAD_DIM
ROPE_THETA = 500000.0
ROPE_DIMS = HEAD_DIM // 4
Q_BLOCK = 128
N_EXPERTS = 32
TOP_K = 4
D_EXPERT = D_MODEL
SWIGLU_ALPHA = 1.702
SWIGLU_LIMIT = 7.0
MOE_BLOCK = 256
NORM_EPS = 1e-6
NEG_INF = -1e30
IN_COLS = 3 * DIL_HEADS * HEAD_DIM + 3 * SB_HEADS * HEAD_DIM + 2 * D_MODEL

kernel_name = "hybrid_dilated_stickbreaking_moe_layer"


def rms_norm(x, gain):
    xf = x.astype(jnp.float32)
    y = xf * lax.rsqrt(jnp.mean(xf * xf, axis=-1, keepdims=True) + NORM_EPS)
    return (y * gain.astype(jnp.float32)).astype(x.dtype)


def partial_rope(t, positions):
    half = ROPE_DIMS // 2
    inv_freq = ROPE_THETA ** (-(jnp.arange(half, dtype=jnp.float32) * 2.0 / ROPE_DIMS))
    ang = positions.astype(jnp.float32)[..., None] * inv_freq
    cos = jnp.cos(ang)[:, :, None, :]
    sin = jnp.sin(ang)[:, :, None, :]
    tr = t[..., :ROPE_DIMS].astype(jnp.float32)
    t1, t2 = tr[..., :half], tr[..., half:]
    rot = jnp.concatenate([t1 * cos - t2 * sin, t2 * cos + t1 * sin], axis=-1)
    return jnp.concatenate([rot.astype(t.dtype), t[..., ROPE_DIMS:]], axis=-1)


def dilated_group_attention(q, k, v, window, dilation):
    b, s, h, hd = q.shape
    r = dilation
    n_back = window // dilation
    length = s // r
    nb = -(-length // Q_BLOCK)
    lp = nb * Q_BLOCK

    def to_sub(t):
        t = t.reshape(b, length, r, h, hd).transpose(0, 2, 1, 3, 4)
        t = jnp.pad(t, ((0, 0), (0, 0), (0, lp - length), (0, 0), (0, 0)))
        return t.reshape(b, r, nb, Q_BLOCK, h, hd).astype(jnp.float32)

    def with_prev(t):
        prev = jnp.pad(t, ((0, 0), (0, 0), (1, 0), (0, 0), (0, 0), (0, 0)))[:, :, :-1]
        return jnp.concatenate([prev, t], axis=3)

    qs = to_sub(q)
    kc = with_prev(to_sub(k))
    vc = with_prev(to_sub(v))
    scores = jnp.einsum('brnqhd,brnkhd->brnhqk', qs, kc) * (hd ** -0.5)
    blk = jnp.arange(nb)[:, None, None]
    qi = jnp.arange(Q_BLOCK)[None, :, None] + Q_BLOCK
    ki = jnp.arange(2 * Q_BLOCK)[None, None, :]
    dist = qi - ki
    valid = (dist >= 0) & (dist <= n_back) & (blk * Q_BLOCK + ki - Q_BLOCK >= 0)
    scores = jnp.where(valid[None, None, :, None], scores, NEG_INF)
    m = jnp.max(scores, axis=-1, keepdims=True)
    p = jnp.exp(scores - m)
    den = jnp.sum(p, axis=-1)
    out = jnp.einsum('brnhqk,brnkhd->brnqhd', p, vc) / jnp.swapaxes(den, -1, -2)[..., None]
    lse = jnp.swapaxes(m[..., 0] + jnp.log(den), -1, -2)

    def from_sub(t):
        t = t.reshape((b, r, lp) + t.shape[4:])[:, :, :length]
        return jnp.swapaxes(t, 1, 2).reshape((b, s) + t.shape[3:])

    return from_sub(out), from_sub(lse)


def dilated_mixture(q, k, v):
    outs, lses = [], []
    for g, (window, dilation) in enumerate(DIL_GROUPS):
        sl = slice(g * DIL_HEADS_PER_GROUP, (g + 1) * DIL_HEADS_PER_GROUP)
        o, l = dilated_group_attention(q[:, :, sl], k[:, :, sl], v[:, :, sl], window, dilation)
        outs.append(o)
        lses.append(l)
    outs = jnp.stack(outs)
    alpha = jax.nn.softmax(jnp.stack(lses), axis=0)
    return jnp.sum(alpha[..., None] * outs, axis=0)


def stick_breaking_attention(q, k, v):
    b, s, h, hd = q.shape
    nb = s // Q_BLOCK
    qb = jnp.moveaxis(q.reshape(b, nb, Q_BLOCK, h, hd), 1, 0).astype(jnp.float32)
    kf = k.astype(jnp.float32)
    vf = v.astype(jnp.float32)
    key_idx = jnp.arange(s)

    def one_block(args):
        q_blk, i = args
        z = jnp.einsum('bqhd,bkhd->bhqk', q_blk, kf) * (hd ** -0.5)
        t = i * Q_BLOCK + jnp.arange(Q_BLOCK)
        causal = key_idx[None, :] < t[:, None]
        log_beta = jax.nn.log_sigmoid(z)
        log_not = jnp.where(causal, -jax.nn.softplus(z), 0.0)
        suffix = lax.cumsum(log_not, axis=3, reverse=True) - log_not
        weights = jnp.where(causal, jnp.exp(log_beta + suffix), 0.0)
        return jnp.einsum('bhqk,bkhd->bqhd', weights, vf)

    out = lax.map(one_block, (qb, jnp.arange(nb)))
    return jnp.moveaxis(out, 0, 1).reshape(b, s, h, hd)


def hybrid_mixer(h, positions, w_in, w_branch_a, w_branch_b, w_out):
    b, s, _ = h.shape
    widths = [DIL_HEADS * HEAD_DIM] * 3 + [SB_HEADS * HEAD_DIM] * 3 + [D_MODEL, D_MODEL]
    cuts = [int(v) for v in np.cumsum(widths)[:-1]]
    proj = h @ w_in
    qa, ka, va, qb, kb, vb, ga, gb = jnp.split(proj, cuts, axis=-1)
    qa = partial_rope(qa.reshape(b, s, DIL_HEADS, HEAD_DIM), positions)
    ka = partial_rope(ka.reshape(b, s, DIL_HEADS, HEAD_DIM), positions)
    va = va.reshape(b, s, DIL_HEADS, HEAD_DIM)
    ya = dilated_mixture(qa, ka, va).reshape(b, s, DIL_OUT).astype(h.dtype) @ w_branch_a
    yb = stick_breaking_attention(qb.reshape(b, s, SB_HEADS, HEAD_DIM),
                                  kb.reshape(b, s, SB_HEADS, HEAD_DIM),
                                  vb.reshape(b, s, SB_HEADS, HEAD_DIM)).reshape(b, s, SB_OUT).astype(h.dtype) @ w_branch_b
    merged = jax.nn.sigmoid(ga) * ya + jax.nn.sigmoid(gb) * yb
    return merged @ w_out


def clamped_swiglu(hu):
    x_glu = jnp.minimum(hu[..., ::2], SWIGLU_LIMIT)
    x_lin = jnp.clip(hu[..., 1::2], -SWIGLU_LIMIT, SWIGLU_LIMIT)
    return x_glu * jax.nn.sigmoid(SWIGLU_ALPHA * x_glu) * (x_lin + 1.0)


def moe_ffn(h, w_router, b_router, w_up, b_up, w_down, b_down):
    b, s, d = h.shape
    n_tok = b * s
    xf = h.reshape(n_tok, d)
    logits = (xf @ w_router + b_router).astype(jnp.float32)
    top_val, top_idx = lax.top_k(logits, TOP_K)
    top_w = jax.nn.softmax(top_val, axis=-1)
    n_assign = n_tok * TOP_K
    flat_e = top_idx.reshape(-1)
    flat_tok = jnp.arange(n_assign, dtype=jnp.int32) // TOP_K
    order = jnp.argsort(flat_e, stable=True)
    e_sorted = flat_e[order]
    counts = jnp.bincount(flat_e, length=N_EXPERTS)
    padded = (counts + MOE_BLOCK - 1) // MOE_BLOCK * MOE_BLOCK
    pad_end = jnp.cumsum(padded)
    pad_start = pad_end - padded
    start = jnp.cumsum(counts) - counts
    dest = pad_start[e_sorted] + jnp.arange(n_assign) - start[e_sorted]
    n_blocks = -(-n_assign // MOE_BLOCK) + N_EXPERTS
    n_slots = n_blocks * MOE_BLOCK
    tok_buf = jnp.full((n_slots,), n_tok, jnp.int32).at[dest].set(flat_tok[order])
    gate_buf = jnp.zeros((n_slots,), jnp.float32).at[dest].set(top_w.reshape(-1)[order])
    block_expert = jnp.minimum(jnp.searchsorted(pad_end, jnp.arange(n_blocks) * MOE_BLOCK, side='right'),
                               N_EXPERTS - 1)
    x_pad = jnp.concatenate([xf, jnp.zeros((1, d), xf.dtype)], axis=0)

    def expert_chunk(args):
        toks, e = args
        hu = x_pad[toks] @ w_up[e] + b_up[e]
        return clamped_swiglu(hu) @ w_down[e] + b_down[e]

    ys = lax.map(expert_chunk, (tok_buf.reshape(n_blocks, MOE_BLOCK), block_expert))
    ys = ys.reshape(n_slots, d).astype(jnp.float32) * gate_buf[:, None]
    out = jnp.zeros((n_tok + 1, d), jnp.float32).at[tok_buf].add(ys)[:n_tok]
    return out.reshape(b, s, d).astype(h.dtype)


def setup_inputs(seed: int = 0) -> dict:
    key = jax.random.key(seed)
    ks = jax.random.split(key, 24)

    def normal(k, shape, scale):
        return jax.random.normal(k, shape, jnp.float32) * scale

    L, D, E, F = DEPTH, D_MODEL, N_EXPERTS, D_EXPERT
    positions = (jnp.arange(SEQ, dtype=jnp.int32)[None, :]
                 + jax.random.randint(ks[2], (BATCH, 1), 0, 4096, dtype=jnp.int32))
    return {
        "x": normal(ks[0], (BATCH, SEQ, D), 1.0),
        "c": normal(ks[1], (BATCH, D), 1.0),
        "positions": positions,
        "ada_w": normal(ks[3], (L, D, 6 * D), D ** -0.5),
        "ada_b": normal(ks[4], (L, 6 * D), 0.01),
        "norm_mix_pre": 1.0 + normal(ks[5], (L, D), 0.05),
        "norm_mix_post": 1.0 + normal(ks[6], (L, D), 0.05),
        "norm_ffn_pre": 1.0 + normal(ks[7], (L, D), 0.05),
        "norm_ffn_post": 1.0 + normal(ks[8], (L, D), 0.05),
        "w_in": normal(ks[9], (L, D, IN_COLS), D ** -0.5),
        "w_branch_a": normal(ks[10], (L, DIL_OUT, D), DIL_OUT ** -0.5),
        "w_branch_b": normal(ks[11], (L, SB_OUT, D), SB_OUT ** -0.5),
        "w_out": normal(ks[12], (L, D, D), D ** -0.5),
        "w_router": normal(ks[13], (L, D, E), D ** -0.5),
        "b_router": normal(ks[14], (L, E), 0.01),
        "w_up": normal(ks[15], (L, E, D, 2 * F), D ** -0.5),
        "b_up": normal(ks[16], (L, E, 2 * F), 0.01),
        "w_down": normal(ks[17], (L, E, F, D), F ** -0.5),
        "b_down": normal(ks[18], (L, E, D), 0.01),
    }


def reference(x, c, positions, ada_w, ada_b, norm_mix_pre, norm_mix_post, norm_ffn_pre, norm_ffn_post,
              w_in, w_branch_a, w_branch_b, w_out, w_router, b_router, w_up, b_up, w_down, b_down):
    for layer in range(DEPTH):
        mod = (jax.nn.silu(c) @ ada_w[layer] + ada_b[layer]).astype(x.dtype)
        shift1, scale1, gate1, shift2, scale2, gate2 = jnp.split(mod[:, None, :], 6, axis=-1)
        h = rms_norm(x, norm_mix_pre[layer]) * (1.0 + scale1) + shift1
        y = hybrid_mixer(h, positions, w_in[layer], w_branch_a[layer], w_branch_b[layer], w_out[layer])
        x = (x + gate1 * rms_norm(y, norm_mix_post[layer])).astype(x.dtype)
        h = rms_norm(x, norm_ffn_pre[layer]) * (1.0 + scale2) + shift2
        y = moe_ffn(h, w_router[layer], b_router[layer], w_up[layer], b_up[layer], w_down[layer], b_down[layer])
        x = (x + gate2 * rms_norm(y, norm_ffn_post[layer])).astype(x.dtype)
    return x
```

```python
import functools

import numpy as np
import jax
import jax.numpy as jnp
from jax import lax
from jax.experimental import pallas as pl
from jax.experimental.pallas import tpu as pltpu
from jax.experimental.pallas import tpu_sc as plsc

F32 = jnp.float32
BF16 = jnp.bfloat16
I32 = jnp.int32

HEAD_DIM = 64
DIL_GROUPS = ((128, 1), (512, 4), (2048, 16))
DIL_HEADS_PER_GROUP = 4
DIL_HEADS = DIL_HEADS_PER_GROUP * len(DIL_GROUPS)
DIL_W = DIL_HEADS * HEAD_DIM
DIL_OUT = DIL_HEADS_PER_GROUP * HEAD_DIM
SB_HEADS = 8
SB_W = SB_HEADS * HEAD_DIM
ROPE_THETA = 500000.0
ROPE_DIMS = HEAD_DIM // 4
Q_BLOCK = 128
N_EXPERTS = 32
TOP_K = 4
SWIGLU_ALPHA = 1.702
SWIGLU_LIMIT = 7.0
NORM_EPS = 1e-6
NEG_INF = -1e30

LANES = 128
ROW_BLOCK = 512
MOE_ROWS = 512
SB_TQ = 256
SB_TK = 256
SB_LOG_FLOOR = -105.0
VMEM_LIMIT = 56 * 1024 * 1024

SC_CORES = 2
SC_SUBCORES = 16
SC_WORKERS = SC_CORES * SC_SUBCORES
SC_CHUNK = 16


def _cparams(sem):
    return pltpu.CompilerParams(dimension_semantics=sem, vmem_limit_bytes=VMEM_LIMIT)


def _rms(x, gain):
    ms = jnp.mean(x * x, axis=-1, keepdims=True)
    return x * lax.rsqrt(ms + NORM_EPS) * gain


def _adaln_kernel(c_ref, w_ref, b_ref, o_ref):
    c = c_ref[...]
    s = c * jax.nn.sigmoid(c)
    o_ref[...] = jnp.dot(s, w_ref[...], preferred_element_type=F32,
                         precision=lax.Precision.HIGHEST) + b_ref[...]


def _adaln(c, ada_w, ada_b):
    b, d = c.shape
    n = ada_w.shape[1]
    return pl.pallas_call(
        _adaln_kernel,
        grid=(n // d,),
        in_specs=[pl.BlockSpec((b, d), lambda j: (0, 0)),
                  pl.BlockSpec((d, d), lambda j: (0, j)),
                  pl.BlockSpec((1, d), lambda j: (0, j))],
        out_specs=pl.BlockSpec((b, d), lambda j: (0, j)),
        out_shape=jax.ShapeDtypeStruct((b, n), F32),
        compiler_params=_cparams(("arbitrary",)),
        name="adaln",
    )(c, ada_w, ada_b.reshape(1, n))


def _rope_table():
    half = ROPE_DIMS // 2
    inv_freq = ROPE_THETA ** (-(np.arange(half, dtype=np.float32) * 2.0 / ROPE_DIMS))
    d = np.arange(LANES) % HEAD_DIM
    tab = np.zeros((8, LANES), np.float32)
    tab[0] = np.where(d < ROPE_DIMS, inv_freq[d % half], 0.0)
    tab[1] = np.where(d < half, -1.0, 1.0)
    tab[2] = np.where(d < half, 1.0, 0.0)
    return jnp.asarray(tab)


def _inproj_kernel(x_ref, mod_ref, g_ref, pos_ref, rope_ref, w_ref,
                   qa_ref, ka_ref, va_ref, qb_ref, kb_ref, vb_ref, ga_ref, gb_ref):
    x = x_ref[...]
    tm = x.shape[0]
    shift = mod_ref[0, 0:1, :]
    scale = mod_ref[0, 1:2, :]
    hb = (_rms(x, g_ref[...]) * (1.0 + scale) + shift).astype(BF16)

    pos = pos_ref[0].astype(F32)
    pos_rows = jnp.broadcast_to(pos, (LANES, tm)).T
    ang = pos_rows * rope_ref[0:1, :]
    cos = jnp.cos(ang)
    sin = jnp.sin(ang) * rope_ref[1:2, :]
    first = rope_ref[2:3, :] > 0.5
    half = ROPE_DIMS // 2

    def rope(t):
        partner = jnp.where(first, pltpu.roll(t, LANES - half, 1), pltpu.roll(t, half, 1))
        return t * cos + partner * sin

    qk_scale = HEAD_DIM ** -0.5
    col = 0
    plan = ((qa_ref, DIL_W, "rope_q"), (ka_ref, DIL_W, "rope"), (va_ref, DIL_W, "id"),
            (qb_ref, SB_W, "scale"), (kb_ref, SB_W, "id"), (vb_ref, SB_W, "id"),
            (ga_ref, x.shape[1], "sig"), (gb_ref, x.shape[1], "sig"))
    for ref, width, mode in plan:
        for c0 in range(0, width, 2 * LANES):
            t = jnp.dot(hb, w_ref[:, col + c0:col + c0 + 2 * LANES], preferred_element_type=F32)
            if mode in ("rope", "rope_q"):
                parts = [rope(t[:, :LANES]), rope(t[:, LANES:])]
                t = jnp.concatenate(parts, axis=1)
                if mode == "rope_q":
                    t = t * qk_scale
            elif mode == "scale":
                t = t * qk_scale
            elif mode == "sig":
                t = jax.nn.sigmoid(t)
            ref[:, c0:c0 + 2 * LANES] = t.astype(ref.dtype)
        col += width


def _in_proj(x2, mod3, gain, positions, w_in_bf16, seq):
    t, d = x2.shape
    tm = ROW_BLOCK
    nb = t // tm
    per_seq = seq // tm
    pos3 = positions.reshape(nb, 1, tm)
    widths = (DIL_W, DIL_W, DIL_W, SB_W, SB_W, SB_W, d, d)
    row = lambda i: (i, 0)
    return pl.pallas_call(
        _inproj_kernel,
        grid=(nb,),
        in_specs=[pl.BlockSpec((tm, d), row),
                  pl.BlockSpec((1, 6, d), lambda i: (i // per_seq, 0, 0)),
                  pl.BlockSpec((1, d), lambda i: (0, 0)),
                  pl.BlockSpec((1, 1, tm), lambda i: (i, 0, 0)),
                  pl.BlockSpec((8, LANES), lambda i: (0, 0)),
                  pl.BlockSpec(w_in_bf16.shape, lambda i: (0, 0))],
        out_specs=[pl.BlockSpec((tm, w), row) for w in widths],
        out_shape=[jax.ShapeDtypeStruct((t, w), BF16) for w in widths],
        compiler_params=_cparams(("parallel",)),
        name="in_proj",
    )(x2, mod3, gain.reshape(1, d), pos3, _rope_table(), w_in_bf16)


def _dilated_kernel(q_ref, kp_ref, kc_ref, vp_ref, vc_ref, o_ref, l_ref):
    n = pl.program_id(2)
    q = q_ref[...]
    k = jnp.concatenate([kp_ref[...], kc_ref[...]], axis=0)
    v = jnp.concatenate([vp_ref[...], vc_ref[...]], axis=0)
    qi = lax.broadcasted_iota(I32, (Q_BLOCK, 2 * Q_BLOCK), 0) + Q_BLOCK
    ki = lax.broadcasted_iota(I32, (Q_BLOCK, 2 * Q_BLOCK), 1)
    dist = qi - ki
    valid = (dist >= 0) & (dist <= Q_BLOCK) & (n * Q_BLOCK + ki - Q_BLOCK >= 0)
    lane = lax.broadcasted_iota(I32, (1, DIL_OUT), 1)
    o = jnp.zeros((Q_BLOCK, DIL_OUT), F32)
    l = jnp.zeros((Q_BLOCK, DIL_OUT), F32)
    for h in range(DIL_HEADS_PER_GROUP):
        head = (lane >= h * HEAD_DIM) & (lane < (h + 1) * HEAD_DIM)
        qm = jnp.where(head, q, jnp.zeros_like(q))
        s = lax.dot_general(qm, k, (((1,), (1,)), ((), ())), preferred_element_type=F32)
        s = jnp.where(valid, s, NEG_INF)
        m = jnp.max(s, axis=1, keepdims=True)
        p = jnp.exp(s - m)
        den = jnp.sum(p, axis=1, keepdims=True)
        pv = jnp.dot(p.astype(BF16), v, preferred_element_type=F32)
        o = jnp.where(head, pv / den, o)
        l = jnp.where(head, m + jnp.log(den), l)
    o_ref[...] = o
    l_ref[...] = l


def _dilated_group(qa, ka, va, batch, seq, group):
    _, r = DIL_GROUPS[group]
    length = seq // r
    nb = length // Q_BLOCK
    n_groups = len(DIL_GROUPS)
    view = lambda a: a.reshape(batch, length, r * DIL_W)
    cur = lambda b, rho, n: (b, n, rho * n_groups + group)
    prev = lambda b, rho, n: (b, jnp.maximum(n - 1, 0), rho * n_groups + group)
    blk = (None, Q_BLOCK, DIL_OUT)
    out_spec = pl.BlockSpec(blk, lambda b, rho, n: (b, n, rho))
    o, l = pl.pallas_call(
        _dilated_kernel,
        grid=(batch, r, nb),
        in_specs=[pl.BlockSpec(blk, cur), pl.BlockSpec(blk, prev), pl.BlockSpec(blk, cur),
                  pl.BlockSpec(blk, prev), pl.BlockSpec(blk, cur)],
        out_specs=[out_spec, out_spec],
        out_shape=[jax.ShapeDtypeStruct((batch, length, r * DIL_OUT), F32)] * 2,
        compiler_params=_cparams(("parallel", "parallel", "arbitrary")),
        name=f"dilated_g{group}",
    )(view(qa), view(ka), view(ka), view(va), view(va))
    return o.reshape(batch * seq, DIL_OUT), l.reshape(batch * seq, DIL_OUT)


def _stickbreak_kernel(q_ref, k_ref, v_ref, o_ref, acc_ref, csum_ref):
    i = pl.program_id(2)
    tq, tk = SB_TQ, SB_TK
    q = q_ref[...]
    lane = lax.broadcasted_iota(I32, (1, LANES), 1)
    t_idx = i * tq + lax.broadcasted_iota(I32, (tq, tk), 0)
    k_off = lax.broadcasted_iota(I32, (tq, tk), 1)
    later = (lax.broadcasted_iota(I32, (tk, tk), 0) > lax.broadcasted_iota(I32, (tk, tk), 1))
    later = jnp.where(later, 1.0, 0.0).astype(BF16)
    first_block = ((i + 1) * tq - 1) // tk

    outs = []
    for h in range(2):
        head = (lane >= h * HEAD_DIM) & (lane < (h + 1) * HEAD_DIM)
        qm = jnp.where(head, q, jnp.zeros_like(q))
        acc_ref[...] = jnp.zeros_like(acc_ref)
        csum_ref[...] = jnp.zeros_like(csum_ref)

        def cond(state):
            j, top = state
            return (j >= 0) & (top > SB_LOG_FLOOR)

        def body(state):
            j, _ = state
            start = pl.multiple_of(j * tk, tk)
            k = k_ref[pl.ds(start, tk), :]
            v = v_ref[pl.ds(start, tk), :]
            z = lax.dot_general(qm, k, (((1,), (1,)), ((), ())), preferred_element_type=F32)
            sp = jnp.maximum(z, 0.0) + jnp.log(1.0 + jnp.exp(-jnp.abs(z)))
            causal = (start + k_off) < t_idx
            log_not = jnp.where(causal, -sp, 0.0)
            hi = log_not.astype(BF16)
            lo = (log_not - hi.astype(F32)).astype(BF16)
            inner = (jnp.dot(hi, later, preferred_element_type=F32)
                     + jnp.dot(lo, later, preferred_element_type=F32))
            csum = csum_ref[...]
            w = jnp.where(causal, jnp.exp((z - sp) + inner + csum), 0.0)
            acc_ref[...] += jnp.dot(w.astype(BF16), v, preferred_element_type=F32)
            csum = csum + jnp.sum(log_not, axis=1, keepdims=True)
            csum_ref[...] = csum
            return j - 1, jnp.max(csum)

        lax.while_loop(cond, body, (first_block, jnp.float32(0.0)))
        outs.append(acc_ref[...])
    o_ref[...] = jnp.where(lane < HEAD_DIM, outs[0], outs[1]).astype(o_ref.dtype)


def _stickbreak(qb, kb, vb, batch, seq):
    view = lambda a: a.reshape(batch, seq, SB_W)
    nq = seq // SB_TQ
    pairs = SB_W // LANES
    kv_spec = pl.BlockSpec((None, seq, LANES), lambda b, hp, i: (b, 0, hp))
    q_spec = pl.BlockSpec((None, SB_TQ, LANES), lambda b, hp, i: (b, i, hp))
    out = pl.pallas_call(
        _stickbreak_kernel,
        grid=(batch, pairs, nq),
        in_specs=[q_spec, kv_spec, kv_spec],
        out_specs=q_spec,
        out_shape=jax.ShapeDtypeStruct((batch, seq, SB_W), BF16),
        scratch_shapes=[pltpu.VMEM((SB_TQ, LANES), F32), pltpu.VMEM((SB_TQ, 1), F32)],
        compiler_params=_cparams(("parallel", "parallel", "arbitrary")),
        name="stickbreak",
    )(view(qb), view(kb), view(vb))
    return out.reshape(batch * seq, SB_W)


def _postmix_kernel(o0_ref, o1_ref, o2_ref, l0_ref, l1_ref, l2_ref, sb_ref, ga_ref, gb_ref, x_ref,
                    mod_ref, gpost_ref, gffn_ref, wa_ref, wb_ref, wo_ref, wr_ref, br_ref,
                    x1_ref, h2_ref, idx_ref, gate_ref, rank_ref, cnt_ref, carry_ref):
    step = pl.program_id(0)
    tm = x_ref.shape[0]

    @pl.when(step == 0)
    def _():
        carry_ref[...] = jnp.zeros_like(carry_ref)

    l0, l1, l2 = l0_ref[...], l1_ref[...], l2_ref[...]
    m = jnp.maximum(jnp.maximum(l0, l1), l2)
    e0, e1, e2 = jnp.exp(l0 - m), jnp.exp(l1 - m), jnp.exp(l2 - m)
    merged = (e0 * o0_ref[...] + e1 * o1_ref[...] + e2 * o2_ref[...]) / (e0 + e1 + e2)
    ya = jnp.dot(merged.astype(BF16), wa_ref[...], preferred_element_type=F32)
    yb = jnp.dot(sb_ref[...], wb_ref[...], preferred_element_type=F32)
    mix = ga_ref[...].astype(F32) * ya + gb_ref[...].astype(F32) * yb
    y = jnp.dot(mix.astype(BF16), wo_ref[...], preferred_element_type=F32)
    gate1 = mod_ref[0, 2:3, :]
    x1 = x_ref[...] + gate1 * _rms(y, gpost_ref[...])
    x1_ref[...] = x1
    h2 = _rms(x1, gffn_ref[...]) * (1.0 + mod_ref[0, 4:5, :]) + mod_ref[0, 3:4, :]
    h2_ref[...] = h2

    logits = lax.dot_general(wr_ref[...], h2, (((1,), (1,)), ((), ())), preferred_element_type=F32,
                             precision=lax.Precision.HIGHEST) + br_ref[...]
    e_iota = lax.broadcasted_iota(I32, (N_EXPERTS, tm), 0)
    before = (lax.broadcasted_iota(I32, (tm, tm), 0) < lax.broadcasted_iota(I32, (tm, tm), 1))
    before = jnp.where(before, 1.0, 0.0).astype(BF16)

    picks, vals = [], []
    work = logits
    chosen = jnp.zeros((N_EXPERTS, tm), F32)
    for _ in range(TOP_K):
        top = jnp.max(work, axis=0, keepdims=True)
        idx = jnp.min(jnp.where(work == top, e_iota, N_EXPERTS), axis=0, keepdims=True)
        one = e_iota == idx
        picks.append((idx, one))
        vals.append(top)
        chosen = jnp.where(one, 1.0, chosen)
        work = jnp.where(one, -jnp.inf, work)
    exps = [jnp.exp(v - vals[0]) for v in vals]
    den = exps[0] + exps[1] + exps[2] + exps[3]

    rank = jnp.dot(chosen.astype(BF16), before, preferred_element_type=F32) + carry_ref[...]
    carry_ref[...] += jnp.sum(chosen, axis=1, keepdims=True)
    cnt_ref[...] = carry_ref[...].astype(I32)
    for kk, (idx, one) in enumerate(picks):
        idx_ref[kk:kk + 1, :] = idx
        gate_ref[kk:kk + 1, :] = exps[kk] / den
        rank_ref[kk:kk + 1, :] = jnp.sum(jnp.where(one, rank, 0.0), axis=0, keepdims=True).astype(I32)


def _post_mix(dil, sb, ga, gb, x2, mod3, gpost, gffn, wa, wb, wo, w_router, b_router, seq):
    t, d = x2.shape
    tm = ROW_BLOCK
    nb = t // tm
    per_seq = seq // tm
    row = lambda i: (i, 0)
    const = lambda i: (0, 0)
    colblk = lambda i: (0, i)
    (o0, l0), (o1, l1), (o2, l2) = dil
    in_specs = ([pl.BlockSpec((tm, DIL_OUT), row)] * 6
                + [pl.BlockSpec((tm, SB_W), row), pl.BlockSpec((tm, d), row), pl.BlockSpec((tm, d), row),
                   pl.BlockSpec((tm, d), row),
                   pl.BlockSpec((1, 6, d), lambda i: (i // per_seq, 0, 0)),
                   pl.BlockSpec((1, d), const), pl.BlockSpec((1, d), const),
                   pl.BlockSpec(wa.shape, const), pl.BlockSpec(wb.shape, const), pl.BlockSpec(wo.shape, const),
                   pl.BlockSpec((N_EXPERTS, d), const), pl.BlockSpec((N_EXPERTS, 1), const)])
    out_specs = [pl.BlockSpec((tm, d), row), pl.BlockSpec((tm, d), row),
                 pl.BlockSpec((TOP_K, tm), colblk), pl.BlockSpec((TOP_K, tm), colblk),
                 pl.BlockSpec((TOP_K, tm), colblk), pl.BlockSpec((N_EXPERTS, 1), const)]
    out_shape = [jax.ShapeDtypeStruct((t, d), F32), jax.ShapeDtypeStruct((t, d), F32),
                 jax.ShapeDtypeStruct((TOP_K, t), I32), jax.ShapeDtypeStruct((TOP_K, t), F32),
                 jax.ShapeDtypeStruct((TOP_K, t), I32), jax.ShapeDtypeStruct((N_EXPERTS, 1), I32)]
    return pl.pallas_call(
        _postmix_kernel,
        grid=(nb,),
        in_specs=in_specs,
        out_specs=out_specs,
        out_shape=out_shape,
        scratch_shapes=[pltpu.VMEM((N_EXPERTS, 1), F32)],
        compiler_params=_cparams(("arbitrary",)),
        name="post_mix",
    )(o0, o1, o2, l0, l1, l2, sb, ga, gb, x2, mod3, gpost.reshape(1, d), gffn.reshape(1, d),
      wa, wb, wo, w_router.T, b_router.reshape(N_EXPERTS, 1))


def _slots_kernel(start_ref, idx_ref, rank_ref, slot_ref):
    idx = idx_ref[...]
    base = jnp.zeros_like(idx)
    for e in range(N_EXPERTS):
        base = jnp.where(idx == e, start_ref[e], base)
    slot_ref[...] = base + rank_ref[...]


def _slots(pad_start, idx, rank):
    k, t = idx.shape
    tb = min(t, 8192)
    blk = pl.BlockSpec((k, tb), lambda i, s: (0, i))
    return pl.pallas_call(
        _slots_kernel,
        grid_spec=pltpu.PrefetchScalarGridSpec(
            num_scalar_prefetch=1, grid=(t // tb,), in_specs=[blk, blk], out_specs=blk),
        out_shape=jax.ShapeDtypeStruct((k, t), I32),
        compiler_params=_cparams(("parallel",)),
        name="slots",
    )(pad_start, idx, rank)


def _sc_mesh():
    return plsc.VectorSubcoreMesh(core_axis_name="c", subcore_axis_name="s")


def _sc_dispatch(rows, slot, n_slots):
    t, d = rows.shape
    per_worker = t // SC_WORKERS
    n_chunks = per_worker // SC_CHUNK

    @functools.partial(
        pl.kernel, mesh=_sc_mesh(),
        out_type=jax.ShapeDtypeStruct((n_slots, d), rows.dtype),
        scratch_types=[pltpu.VMEM((TOP_K, SC_CHUNK), I32), pltpu.VMEM((SC_CHUNK, d), rows.dtype),
                       pltpu.SemaphoreType.DMA],
        name="dispatch",
    )
    def run(rows_hbm, slot_hbm, out_hbm, idx_v, rows_v, sem):
        wid = lax.axis_index("s") * SC_CORES + lax.axis_index("c")

        @pl.loop(0, n_chunks)
        def _(ci):
            base = pl.multiple_of(wid * per_worker + ci * SC_CHUNK, SC_CHUNK)
            pltpu.sync_copy(rows_hbm.at[pl.ds(base, SC_CHUNK)], rows_v)
            for kk in range(TOP_K):
                pltpu.sync_copy(slot_hbm.at[kk, pl.ds(base, SC_CHUNK)], idx_v.at[kk])
            copies = [pltpu.make_async_copy(rows_v, out_hbm.at[idx_v.at[kk]], sem) for kk in range(TOP_K)]
            for cp in copies:
                cp.start()
            for cp in copies:
                cp.wait()

    return run(rows, slot)


def _sc_combine(ys, slot):
    _, d = ys.shape
    k, t = slot.shape
    per_worker = t // SC_WORKERS
    n_chunks = per_worker // SC_CHUNK

    @functools.partial(
        pl.kernel, mesh=_sc_mesh(),
        out_type=jax.ShapeDtypeStruct((k, t, d), ys.dtype),
        scratch_types=[pltpu.VMEM((TOP_K, SC_CHUNK), I32), pltpu.VMEM((TOP_K, SC_CHUNK, d), ys.dtype),
                       pltpu.SemaphoreType.DMA],
        name="combine",
    )
    def run(ys_hbm, slot_hbm, out_hbm, idx_v, rows_v, sem):
        wid = lax.axis_index("s") * SC_CORES + lax.axis_index("c")

        @pl.loop(0, n_chunks)
        def _(ci):
            base = pl.multiple_of(wid * per_worker + ci * SC_CHUNK, SC_CHUNK)
            for kk in range(TOP_K):
                pltpu.sync_copy(slot_hbm.at[kk, pl.ds(base, SC_CHUNK)], idx_v.at[kk])
            copies = [pltpu.make_async_copy(ys_hbm.at[idx_v.at[kk]], rows_v.at[kk], sem) for kk in range(TOP_K)]
            for cp in copies:
                cp.start()
            for cp in copies:
                cp.wait()
            for kk in range(TOP_K):
                pltpu.sync_copy(rows_v.at[kk], out_hbm.at[kk, pl.ds(base, SC_CHUNK)])

    return run(ys, slot)


def _experts_kernel(be_ref, nused_ref, x_ref, wg_ref, wl_ref, bg_ref, bl_ref, wd_ref, bd_ref, y_ref):
    @pl.when(pl.program_id(0) < nused_ref[0])
    def _():
        x = x_ref[...].astype(BF16)
        glu = jnp.dot(x, wg_ref[0], preferred_element_type=F32) + bg_ref[0]
        lin = jnp.dot(x, wl_ref[0], preferred_element_type=F32) + bl_ref[0]
        glu = jnp.minimum(glu, SWIGLU_LIMIT)
        lin = jnp.clip(lin, -SWIGLU_LIMIT, SWIGLU_LIMIT)
        act = glu * jax.nn.sigmoid(SWIGLU_ALPHA * glu) * (lin + 1.0)
        y_ref[...] = jnp.dot(act.astype(BF16), wd_ref[0], preferred_element_type=F32) + bd_ref[0]


def _experts(xs, block_expert, n_used, w_glu, w_lin, b_glu, b_lin, w_down, b_down):
    n_slots, d = xs.shape
    f = w_glu.shape[2]
    nblk = n_slots // MOE_ROWS
    rows = lambda i, be, nu: (i, 0)
    by_e = lambda i, be, nu: (be[i], 0, 0)
    return pl.pallas_call(
        _experts_kernel,
        grid_spec=pltpu.PrefetchScalarGridSpec(
            num_scalar_prefetch=2, grid=(nblk,),
            in_specs=[pl.BlockSpec((MOE_ROWS, d), rows),
                      pl.BlockSpec((1, d, f), by_e), pl.BlockSpec((1, d, f), by_e),
                      pl.BlockSpec((1, 1, f), by_e), pl.BlockSpec((1, 1, f), by_e),
                      pl.BlockSpec((1, f, d), by_e), pl.BlockSpec((1, 1, d), by_e)],
            out_specs=pl.BlockSpec((MOE_ROWS, d), rows)),
        out_shape=jax.ShapeDtypeStruct((n_slots, d), F32),
        compiler_params=_cparams(("arbitrary",)),
        name="experts",
    )(block_expert, n_used, xs, w_glu, w_lin, b_glu, b_lin, w_down, b_down)


def _final_kernel(g_ref, gate_ref, x1_ref, mod_ref, gain_ref, o_ref):
    tm = x1_ref.shape[0]
    gates = gate_ref[...]
    padded = jnp.concatenate([gates, jnp.zeros((LANES - TOP_K, tm), F32)], axis=0)
    gate_cols = padded.T
    y = jnp.zeros(x1_ref.shape, F32)
    for kk in range(TOP_K):
        y = y + g_ref[kk].astype(F32) * gate_cols[:, kk:kk + 1]
    o_ref[...] = x1_ref[...] + mod_ref[0, 5:6, :] * _rms(y, gain_ref[...])


def _final(g, gates, x1, mod3, gain, seq):
    t, d = x1.shape
    tm = ROW_BLOCK
    per_seq = seq // tm
    row = lambda i: (i, 0)
    return pl.pallas_call(
        _final_kernel,
        grid=(t // tm,),
        in_specs=[pl.BlockSpec((TOP_K, tm, d), lambda i: (0, i, 0)),
                  pl.BlockSpec((TOP_K, tm), lambda i: (0, i)),
                  pl.BlockSpec((tm, d), row),
                  pl.BlockSpec((1, 6, d), lambda i: (i // per_seq, 0, 0)),
                  pl.BlockSpec((1, d), lambda i: (0, 0))],
        out_specs=pl.BlockSpec((tm, d), row),
        out_shape=jax.ShapeDtypeStruct((t, d), F32),
        compiler_params=_cparams(("parallel",)),
        name="final",
    )(g, gates, x1, mod3, gain.reshape(1, d))


def _layer(x2, c, positions, seq, ada_w, ada_b, norm_mix_pre, norm_mix_post, norm_ffn_pre, norm_ffn_post,
           w_in, w_branch_a, w_branch_b, w_out, w_router, b_router, w_up, b_up, w_down, b_down):
    t, d = x2.shape
    batch = t // seq
    mod3 = _adaln(c, ada_w, ada_b).reshape(batch, 6, d)

    qa, ka, va, qb, kb, vb, ga, gb = _in_proj(x2, mod3, norm_mix_pre, positions, w_in.astype(BF16), seq)
    dil = [_dilated_group(qa, ka, va, batch, seq, g) for g in range(len(DIL_GROUPS))]
    sb = _stickbreak(qb, kb, vb, batch, seq)
    x1, h2, idx, gates, rank, counts = _post_mix(
        dil, sb, ga, gb, x2, mod3, norm_mix_post, norm_ffn_pre,
        w_branch_a.astype(BF16), w_branch_b.astype(BF16), w_out.astype(BF16), w_router, b_router, seq)

    counts = counts.reshape(N_EXPERTS)
    padded = (counts + MOE_ROWS - 1) // MOE_ROWS * MOE_ROWS
    pad_end = jnp.cumsum(padded)
    pad_start = (pad_end - padded).astype(I32)
    nblk = (t * TOP_K) // MOE_ROWS + N_EXPERTS
    block_expert = jnp.minimum(
        jnp.searchsorted(pad_end, jnp.arange(nblk, dtype=I32) * MOE_ROWS, side="right"), N_EXPERTS - 1).astype(I32)
    n_used = (pad_end[-1:] // MOE_ROWS).astype(I32)

    slot = _slots(pad_start, idx, rank)
    xs = _sc_dispatch(h2, slot, nblk * MOE_ROWS)
    f = w_up.shape[2] // 2
    ys = _experts(xs, block_expert, n_used,
                  w_up[:, :, 0::2].astype(BF16), w_up[:, :, 1::2].astype(BF16),
                  b_up[:, 0::2].reshape(N_EXPERTS, 1, f), b_up[:, 1::2].reshape(N_EXPERTS, 1, f),
                  w_down.astype(BF16), b_down.reshape(N_EXPERTS, 1, d))
    g = _sc_combine(ys, slot)
    return _final(g, gates, x1, mod3, norm_ffn_post, seq)


def kernel(x, c, positions, ada_w, ada_b, norm_mix_pre, norm_mix_post, norm_ffn_pre, norm_ffn_post,
           w_in, w_branch_a, w_branch_b, w_out, w_router, b_router, w_up, b_up, w_down, b_down):
    batch, seq, d = x.shape
    x2 = x.reshape(batch * seq, d)
    for layer in range(ada_w.shape[0]):
        x2 = _layer(x2, c, positions, seq, ada_w[layer], ada_b[layer], norm_mix_pre[layer], norm_mix_post[layer],
                    norm_ffn_pre[layer], norm_ffn_post[layer], w_in[layer], w_branch_a[layer], w_branch_b[layer],
                    w_out[layer], w_router[layer], b_router[layer], w_up[layer], b_up[layer], w_down[layer],
                    b_down[layer])
    return x2.reshape(batch, seq, d)
```

```python
import functools

import numpy as np
import jax
import jax.numpy as jnp
from jax import lax
from jax.experimental import pallas as pl
from jax.experimental.pallas import tpu as pltpu
from jax.experimental.pallas import tpu_sc as plsc

F32 = jnp.float32
BF16 = jnp.bfloat16
I32 = jnp.int32

HEAD_DIM = 64
DIL_GROUPS = ((128, 1), (512, 4), (2048, 16))
DIL_HEADS_PER_GROUP = 4
DIL_HEADS = DIL_HEADS_PER_GROUP * len(DIL_GROUPS)
DIL_W = DIL_HEADS * HEAD_DIM
DIL_OUT = DIL_HEADS_PER_GROUP * HEAD_DIM
SB_HEADS = 8
SB_W = SB_HEADS * HEAD_DIM
ROPE_THETA = 500000.0
ROPE_DIMS = HEAD_DIM // 4
Q_BLOCK = 128
N_EXPERTS = 32
TOP_K = 4
SWIGLU_ALPHA = 1.702
SWIGLU_LIMIT = 7.0
NORM_EPS = 1e-6
NEG_INF = -1e30

LANES = 128
ROW_BLOCK = 512
MOE_ROWS = 512
DIL_TOKENS = 512
SB_TQ = 256
SB_LOG_FLOOR = -105.0
VMEM_LIMIT = 56 * 1024 * 1024

SC_CORES = 2
SC_SUBCORES = 16
SC_WORKERS = SC_CORES * SC_SUBCORES
SC_CHUNK = 16


def _cparams(sem):
    return pltpu.CompilerParams(dimension_semantics=sem, vmem_limit_bytes=VMEM_LIMIT)


def _rms(x, gain):
    ms = jnp.mean(x * x, axis=-1, keepdims=True)
    return x * lax.rsqrt(ms + NORM_EPS) * gain


def _adaln_kernel(c_ref, w_ref, b_ref, o_ref):
    c = c_ref[...]
    s = c * jax.nn.sigmoid(c)
    o_ref[...] = jnp.dot(s, w_ref[...], preferred_element_type=F32,
                         precision=lax.Precision.HIGHEST) + b_ref[...]


def _adaln(c, ada_w, ada_b):
    b, d = c.shape
    n = ada_w.shape[1]
    return pl.pallas_call(
        _adaln_kernel,
        grid=(n // d,),
        in_specs=[pl.BlockSpec((b, d), lambda j: (0, 0)),
                  pl.BlockSpec((d, d), lambda j: (0, j)),
                  pl.BlockSpec((1, d), lambda j: (0, j))],
        out_specs=pl.BlockSpec((b, d), lambda j: (0, j)),
        out_shape=jax.ShapeDtypeStruct((b, n), F32),
        compiler_params=_cparams(("arbitrary",)),
        name="adaln",
    )(c, ada_w, ada_b.reshape(1, n))


def _rope_table():
    half = ROPE_DIMS // 2
    inv_freq = ROPE_THETA ** (-(np.arange(half, dtype=np.float32) * 2.0 / ROPE_DIMS))
    d = np.arange(LANES) % HEAD_DIM
    tab = np.zeros((8, LANES), np.float32)
    tab[0] = np.where(d < ROPE_DIMS, inv_freq[d % half], 0.0)
    tab[1] = np.where(d < half, -1.0, 1.0)
    tab[2] = np.where(d < half, 1.0, 0.0)
    return jnp.asarray(tab)


def _inproj_kernel(x_ref, mod_ref, g_ref, pos_ref, rope_ref, w_ref,
                   qa_ref, ka_ref, va_ref, qb_ref, kb_ref, vb_ref, ga_ref, gb_ref):
    x = x_ref[...]
    tm = x.shape[0]
    shift = mod_ref[0, 0:1, :]
    scale = mod_ref[0, 1:2, :]
    hb = (_rms(x, g_ref[...]) * (1.0 + scale) + shift).astype(BF16)

    pos = pos_ref[0].astype(F32)
    pos_rows = jnp.broadcast_to(pos, (LANES, tm)).T
    ang = pos_rows * rope_ref[0:1, :]
    cos = jnp.cos(ang)
    sin = jnp.sin(ang) * rope_ref[1:2, :]
    first = rope_ref[2:3, :] > 0.5
    half = ROPE_DIMS // 2

    def rope(t):
        partner = jnp.where(first, pltpu.roll(t, LANES - half, 1), pltpu.roll(t, half, 1))
        return t * cos + partner * sin

    qk_scale = HEAD_DIM ** -0.5
    col = 0
    plan = ((qa_ref, DIL_W, "rope_q"), (ka_ref, DIL_W, "rope"), (va_ref, DIL_W, "id"),
            (qb_ref, SB_W, "scale"), (kb_ref, SB_W, "id"), (vb_ref, SB_W, "id"),
            (ga_ref, x.shape[1], "sig"), (gb_ref, x.shape[1], "sig"))
    for ref, width, mode in plan:
        for c0 in range(0, width, 2 * LANES):
            t = jnp.dot(hb, w_ref[:, col + c0:col + c0 + 2 * LANES], preferred_element_type=F32)
            if mode in ("rope", "rope_q"):
                parts = [rope(t[:, :LANES]), rope(t[:, LANES:])]
                t = jnp.concatenate(parts, axis=1)
                if mode == "rope_q":
                    t = t * qk_scale
            elif mode == "scale":
                t = t * qk_scale
            elif mode == "sig":
                t = jax.nn.sigmoid(t)
            ref[:, c0:c0 + 2 * LANES] = t.astype(ref.dtype)
        col += width


def _in_proj(x2, mod3, gain, positions, w_in_bf16, seq):
    t, d = x2.shape
    tm = ROW_BLOCK
    nb = t // tm
    per_seq = seq // tm
    pos3 = positions.reshape(nb, 1, tm)
    widths = (DIL_W, DIL_W, DIL_W, SB_W, SB_W, SB_W, d, d)
    row = lambda i: (i, 0)
    return pl.pallas_call(
        _inproj_kernel,
        grid=(nb,),
        in_specs=[pl.BlockSpec((tm, d), row),
                  pl.BlockSpec((1, 6, d), lambda i: (i // per_seq, 0, 0)),
                  pl.BlockSpec((1, d), lambda i: (0, 0)),
                  pl.BlockSpec((1, 1, tm), lambda i: (i, 0, 0)),
                  pl.BlockSpec((8, LANES), lambda i: (0, 0)),
                  pl.BlockSpec(w_in_bf16.shape, lambda i: (0, 0))],
        out_specs=[pl.BlockSpec((tm, w), row) for w in widths],
        out_shape=[jax.ShapeDtypeStruct((t, w), BF16) for w in widths],
        compiler_params=_cparams(("parallel",)),
        name="in_proj",
    )(x2, mod3, gain.reshape(1, d), pos3, _rope_table(), w_in_bf16)


def _dilated_kernel(q_ref, kp_ref, kc_ref, vp_ref, vc_ref, o_ref, l_ref,
                    qf_ref, kf_ref, vf_ref, of_ref, lf_ref, *, r, nq):
    n = pl.program_id(1)
    prev_rows = Q_BLOCK * r
    halves = DIL_OUT // LANES
    for c in range(halves):
        cols = slice(c * LANES, (c + 1) * LANES)
        qf_ref[c] = q_ref[:, cols].astype(F32)
        kf_ref[c, 0:prev_rows, :] = kp_ref[:, cols].astype(F32)
        kf_ref[c, prev_rows:, :] = kc_ref[:, cols].astype(F32)
        vf_ref[c, 0:prev_rows, :] = vp_ref[:, cols].astype(F32)
        vf_ref[c, prev_rows:, :] = vc_ref[:, cols].astype(F32)

    def gather_rows(ref, start, size):
        idx = pl.ds(start, size, stride=r) if r > 1 else pl.ds(start, size)
        return jnp.concatenate([ref[c, idx, :] for c in range(halves)], axis=1).astype(BF16)

    def scatter_rows(ref, start, val):
        idx = pl.ds(start, Q_BLOCK, stride=r) if r > 1 else pl.ds(start, Q_BLOCK)
        for c in range(halves):
            ref[c, idx, :] = val[:, c * LANES:(c + 1) * LANES]

    qi = lax.broadcasted_iota(I32, (Q_BLOCK, 2 * Q_BLOCK), 0) + Q_BLOCK
    ki = lax.broadcasted_iota(I32, (Q_BLOCK, 2 * Q_BLOCK), 1)
    dist = qi - ki
    band = (dist >= 0) & (dist <= Q_BLOCK)
    lane = lax.broadcasted_iota(I32, (1, DIL_OUT), 1)

    heads = [(lane >= h * HEAD_DIM) & (lane < (h + 1) * HEAD_DIM) for h in range(DIL_HEADS_PER_GROUP)]
    ones = jnp.ones((2 * Q_BLOCK, LANES), BF16)

    def sub_block(idx, carry):
        rho = idx % r
        qb = idx // r
        base = r * Q_BLOCK * qb + rho
        q = gather_rows(qf_ref, base, Q_BLOCK)
        k = gather_rows(kf_ref, base, 2 * Q_BLOCK)
        v = gather_rows(vf_ref, base, 2 * Q_BLOCK)
        valid = band & ((n * nq + qb - 1) * Q_BLOCK + ki >= 0)
        qs = jnp.concatenate([jnp.where(heads[h], q, jnp.zeros_like(q)) for h in range(DIL_HEADS_PER_GROUP)], axis=0)
        s = lax.dot_general(qs, k, (((1,), (1,)), ((), ())), preferred_element_type=F32)
        s = jnp.where(jnp.concatenate([valid] * DIL_HEADS_PER_GROUP, axis=0), s, NEG_INF)
        m = jnp.max(s, axis=1, keepdims=True)
        p = jnp.exp(s - m).astype(BF16)
        pv = jnp.dot(p, v, preferred_element_type=F32)
        den = jnp.dot(p, ones, preferred_element_type=F32)
        inv = 1.0 / den
        lse = m + jnp.log(den)
        o = jnp.zeros((Q_BLOCK, DIL_OUT), F32)
        l = jnp.zeros((Q_BLOCK, DIL_OUT), F32)
        for h in range(DIL_HEADS_PER_GROUP):
            blk = slice(h * Q_BLOCK, (h + 1) * Q_BLOCK)
            o = jnp.where(heads[h], pv[blk] * jnp.concatenate([inv[blk]] * halves, axis=1), o)
            l = jnp.where(heads[h], jnp.concatenate([lse[blk]] * halves, axis=1), l)
        scatter_rows(of_ref, base, o)
        scatter_rows(lf_ref, base, l)
        return carry

    lax.fori_loop(0, r * nq, sub_block, 0, unroll=2)
    for c in range(halves):
        o_ref[:, c * LANES:(c + 1) * LANES] = of_ref[c]
        l_ref[:, c * LANES:(c + 1) * LANES] = lf_ref[c]


def _dilated_group(qa, ka, va, batch, seq, group):
    _, r = DIL_GROUPS[group]
    nq = max(1, DIL_TOKENS // (Q_BLOCK * r))
    prev_rows = Q_BLOCK * r
    cur_rows = prev_rows * nq
    halves = DIL_OUT // LANES
    view = lambda a: a.reshape(batch, seq, DIL_W)
    cur = lambda b, n: (b, n, group)
    prev = lambda b, n: (b, jnp.maximum(n * nq - 1, 0), group)
    cur_spec = pl.BlockSpec((None, cur_rows, DIL_OUT), cur)
    prev_spec = pl.BlockSpec((None, prev_rows, DIL_OUT), prev)
    out_spec = pl.BlockSpec((None, cur_rows, DIL_OUT), lambda b, n: (b, n, 0))
    o, l = pl.pallas_call(
        functools.partial(_dilated_kernel, r=r, nq=nq),
        grid=(batch, seq // cur_rows),
        in_specs=[cur_spec, prev_spec, cur_spec, prev_spec, cur_spec],
        out_specs=[out_spec, out_spec],
        out_shape=[jax.ShapeDtypeStruct((batch, seq, DIL_OUT), F32)] * 2,
        scratch_shapes=[pltpu.VMEM((halves, cur_rows, LANES), F32),
                        pltpu.VMEM((halves, prev_rows + cur_rows, LANES), F32),
                        pltpu.VMEM((halves, prev_rows + cur_rows, LANES), F32),
                        pltpu.VMEM((halves, cur_rows, LANES), F32),
                        pltpu.VMEM((halves, cur_rows, LANES), F32)],
        compiler_params=_cparams(("parallel", "arbitrary")),
        name=f"dilated_g{group}",
    )(view(qa), view(ka), view(ka), view(va), view(va))
    return o.reshape(batch * seq, DIL_OUT), l.reshape(batch * seq, DIL_OUT)


def _stickbreak_kernel(q_ref, k_ref, v_ref, o_ref, acc_ref, csum_ref):
    i = pl.program_id(2)
    tq = SB_TQ
    q = q_ref[...]
    lane = lax.broadcasted_iota(I32, (1, LANES), 1)
    zero = jnp.zeros_like(q)
    qs = jnp.concatenate([jnp.where(lane < HEAD_DIM, q, zero), jnp.where(lane >= HEAD_DIM, q, zero)], axis=0)
    row = lax.broadcasted_iota(I32, (2 * tq, tq), 0)
    causal = lax.broadcasted_iota(I32, (2 * tq, tq), 1) < jnp.where(row >= tq, row - tq, row)
    later = (lax.broadcasted_iota(I32, (tq, tq), 0) > lax.broadcasted_iota(I32, (tq, tq), 1))
    later = jnp.where(later, 1.0, 0.0).astype(BF16)
    acc_ref[...] = jnp.zeros_like(acc_ref)
    csum_ref[...] = jnp.zeros_like(csum_ref)

    def sweep(j, diagonal):
        start = pl.multiple_of(j * tq, tq)
        k = k_ref[pl.ds(start, tq), :]
        v = v_ref[pl.ds(start, tq), :]
        z = lax.dot_general(qs, k, (((1,), (1,)), ((), ())), preferred_element_type=F32)
        sp = jnp.maximum(z, 0.0) + jnp.log(1.0 + jnp.exp(-jnp.abs(z)))
        log_not = jnp.where(causal, -sp, 0.0) if diagonal else -sp
        hi = log_not.astype(BF16)
        lo = (log_not - hi.astype(F32)).astype(BF16)
        inner = (jnp.dot(hi, later, preferred_element_type=F32)
                 + jnp.dot(lo, later, preferred_element_type=F32))
        csum = csum_ref[...]
        w = jnp.exp((z - sp) + inner + csum)
        if diagonal:
            w = jnp.where(causal, w, 0.0)
        acc_ref[...] += jnp.dot(w.astype(BF16), v, preferred_element_type=F32)
        csum = csum + jnp.sum(log_not, axis=1, keepdims=True)
        csum_ref[...] = csum
        return jnp.max(csum)

    def cond(state):
        j, top = state
        return (j >= 0) & (top > SB_LOG_FLOOR)

    def body(state):
        j, _ = state
        return j - 1, sweep(j, False)

    lax.while_loop(cond, body, (i - 1, sweep(i, True)))
    acc = acc_ref[...]
    o_ref[...] = jnp.where(lane < HEAD_DIM, acc[:tq], acc[tq:]).astype(o_ref.dtype)


def _stickbreak(qb, kb, vb, batch, seq):
    view = lambda a: a.reshape(batch, seq, SB_W)
    nq = seq // SB_TQ
    pairs = SB_W // LANES
    kv_spec = pl.BlockSpec((None, seq, LANES), lambda b, hp, i: (b, 0, hp))
    q_spec = pl.BlockSpec((None, SB_TQ, LANES), lambda b, hp, i: (b, i, hp))
    out = pl.pallas_call(
        _stickbreak_kernel,
        grid=(batch, pairs, nq),
        in_specs=[q_spec, kv_spec, kv_spec],
        out_specs=q_spec,
        out_shape=jax.ShapeDtypeStruct((batch, seq, SB_W), BF16),
        scratch_shapes=[pltpu.VMEM((2 * SB_TQ, LANES), F32), pltpu.VMEM((2 * SB_TQ, 1), F32)],
        compiler_params=_cparams(("parallel", "parallel", "arbitrary")),
        name="stickbreak",
    )(view(qb), view(kb), view(vb))
    return out.reshape(batch * seq, SB_W)


def _postmix_kernel(o0_ref, o1_ref, o2_ref, l0_ref, l1_ref, l2_ref, sb_ref, ga_ref, gb_ref, x_ref,
                    mod_ref, gpost_ref, gffn_ref, wa_ref, wb_ref, wo_ref, wr_ref, br_ref,
                    x1_ref, h2_ref, idx_ref, gate_ref, rank_ref, cnt_ref, carry_ref):
    step = pl.program_id(0)
    tm = x_ref.shape[0]

    @pl.when(step == 0)
    def _():
        carry_ref[...] = jnp.zeros_like(carry_ref)

    l0, l1, l2 = l0_ref[...], l1_ref[...], l2_ref[...]
    m = jnp.maximum(jnp.maximum(l0, l1), l2)
    e0, e1, e2 = jnp.exp(l0 - m), jnp.exp(l1 - m), jnp.exp(l2 - m)
    merged = (e0 * o0_ref[...] + e1 * o1_ref[...] + e2 * o2_ref[...]) / (e0 + e1 + e2)
    ya = jnp.dot(merged.astype(BF16), wa_ref[...], preferred_element_type=F32)
    yb = jnp.dot(sb_ref[...], wb_ref[...], preferred_element_type=F32)
    mix = ga_ref[...].astype(F32) * ya + gb_ref[...].astype(F32) * yb
    y = jnp.dot(mix.astype(BF16), wo_ref[...], preferred_element_type=F32)
    gate1 = mod_ref[0, 2:3, :]
    x1 = x_ref[...] + gate1 * _rms(y, gpost_ref[...])
    x1_ref[...] = x1
    h2 = _rms(x1, gffn_ref[...]) * (1.0 + mod_ref[0, 4:5, :]) + mod_ref[0, 3:4, :]
    h2_ref[...] = h2

    logits = lax.dot_general(wr_ref[...], h2, (((1,), (1,)), ((), ())), preferred_element_type=F32,
                             precision=lax.Precision.HIGHEST) + br_ref[...]
    e_iota = lax.broadcasted_iota(I32, (N_EXPERTS, tm), 0)
    before = (lax.broadcasted_iota(I32, (tm, tm), 0) < lax.broadcasted_iota(I32, (tm, tm), 1))
    before = jnp.where(before, 1.0, 0.0).astype(BF16)

    picks, vals = [], []
    work = logits
    chosen = jnp.zeros((N_EXPERTS, tm), F32)
    for _ in range(TOP_K):
        top = jnp.max(work, axis=0, keepdims=True)
        idx = jnp.min(jnp.where(work == top, e_iota, N_EXPERTS), axis=0, keepdims=True)
        one = e_iota == idx
        picks.append((idx, one))
        vals.append(top)
        chosen = jnp.where(one, 1.0, chosen)
        work = jnp.where(one, -jnp.inf, work)
    exps = [jnp.exp(v - vals[0]) for v in vals]
    den = exps[0] + exps[1] + exps[2] + exps[3]

    rank = jnp.dot(chosen.astype(BF16), before, preferred_element_type=F32) + carry_ref[...]
    carry_ref[...] += jnp.sum(chosen, axis=1, keepdims=True)
    cnt_ref[...] = carry_ref[...].astype(I32)
    for kk, (idx, one) in enumerate(picks):
        idx_ref[kk:kk + 1, :] = idx
        gate_ref[kk:kk + 1, :] = exps[kk] / den
        rank_ref[kk:kk + 1, :] = jnp.sum(jnp.where(one, rank, 0.0), axis=0, keepdims=True).astype(I32)


def _post_mix(dil, sb, ga, gb, x2, mod3, gpost, gffn, wa, wb, wo, w_router, b_router, seq):
    t, d = x2.shape
    tm = ROW_BLOCK
    nb = t // tm
    per_seq = seq // tm
    row = lambda i: (i, 0)
    const = lambda i: (0, 0)
    colblk = lambda i: (0, i)
    (o0, l0), (o1, l1), (o2, l2) = dil
    in_specs = ([pl.BlockSpec((tm, DIL_OUT), row)] * 6
                + [pl.BlockSpec((tm, SB_W), row), pl.BlockSpec((tm, d), row), pl.BlockSpec((tm, d), row),
                   pl.BlockSpec((tm, d), row),
                   pl.BlockSpec((1, 6, d), lambda i: (i // per_seq, 0, 0)),
                   pl.BlockSpec((1, d), const), pl.BlockSpec((1, d), const),
                   pl.BlockSpec(wa.shape, const), pl.BlockSpec(wb.shape, const), pl.BlockSpec(wo.shape, const),
                   pl.BlockSpec((N_EXPERTS, d), const), pl.BlockSpec((N_EXPERTS, 1), const)])
    out_specs = [pl.BlockSpec((tm, d), row), pl.BlockSpec((tm, d), row),
                 pl.BlockSpec((TOP_K, tm), colblk), pl.BlockSpec((TOP_K, tm), colblk),
                 pl.BlockSpec((TOP_K, tm), colblk), pl.BlockSpec((N_EXPERTS, 1), const)]
    out_shape = [jax.ShapeDtypeStruct((t, d), F32), jax.ShapeDtypeStruct((t, d), F32),
                 jax.ShapeDtypeStruct((TOP_K, t), I32), jax.ShapeDtypeStruct((TOP_K, t), F32),
                 jax.ShapeDtypeStruct((TOP_K, t), I32), jax.ShapeDtypeStruct((N_EXPERTS, 1), I32)]
    return pl.pallas_call(
        _postmix_kernel,
        grid=(nb,),
        in_specs=in_specs,
        out_specs=out_specs,
        out_shape=out_shape,
        scratch_shapes=[pltpu.VMEM((N_EXPERTS, 1), F32)],
        compiler_params=_cparams(("arbitrary",)),
        name="post_mix",
    )(o0, o1, o2, l0, l1, l2, sb, ga, gb, x2, mod3, gpost.reshape(1, d), gffn.reshape(1, d),
      wa, wb, wo, w_router.T, b_router.reshape(N_EXPERTS, 1))


def _slots_kernel(start_ref, idx_ref, rank_ref, slot_ref):
    idx = idx_ref[...]
    base = jnp.zeros_like(idx)
    for e in range(N_EXPERTS):
        base = jnp.where(idx == e, start_ref[e], base)
    slot_ref[...] = base + rank_ref[...]


def _slots(pad_start, idx, rank):
    k, t = idx.shape
    tb = min(t, 8192)
    blk = pl.BlockSpec((k, tb), lambda i, s: (0, i))
    return pl.pallas_call(
        _slots_kernel,
        grid_spec=pltpu.PrefetchScalarGridSpec(
            num_scalar_prefetch=1, grid=(t // tb,), in_specs=[blk, blk], out_specs=blk),
        out_shape=jax.ShapeDtypeStruct((k, t), I32),
        compiler_params=_cparams(("parallel",)),
        name="slots",
    )(pad_start, idx, rank)


def _sc_mesh():
    return plsc.VectorSubcoreMesh(core_axis_name="c", subcore_axis_name="s")


def _sc_dispatch(rows, slot, n_slots):
    t, d = rows.shape
    per_worker = t // SC_WORKERS
    n_chunks = per_worker // SC_CHUNK

    @functools.partial(
        pl.kernel, mesh=_sc_mesh(),
        out_type=jax.ShapeDtypeStruct((n_slots, d), rows.dtype),
        scratch_types=[pltpu.VMEM((TOP_K, SC_CHUNK), I32), pltpu.VMEM((SC_CHUNK, d), rows.dtype),
                       pltpu.SemaphoreType.DMA],
        name="dispatch",
    )
    def run(rows_hbm, slot_hbm, out_hbm, idx_v, rows_v, sem):
        wid = lax.axis_index("s") * SC_CORES + lax.axis_index("c")

        @pl.loop(0, n_chunks)
        def _(ci):
            base = pl.multiple_of(wid * per_worker + ci * SC_CHUNK, SC_CHUNK)
            pltpu.sync_copy(rows_hbm.at[pl.ds(base, SC_CHUNK)], rows_v)
            for kk in range(TOP_K):
                pltpu.sync_copy(slot_hbm.at[kk, pl.ds(base, SC_CHUNK)], idx_v.at[kk])
            copies = [pltpu.make_async_copy(rows_v, out_hbm.at[idx_v.at[kk]], sem) for kk in range(TOP_K)]
            for cp in copies:
                cp.start()
            for cp in copies:
                cp.wait()

    return run(rows, slot)


def _sc_combine(ys, slot):
    _, d = ys.shape
    k, t = slot.shape
    per_worker = t // SC_WORKERS
    n_chunks = per_worker // SC_CHUNK

    @functools.partial(
        pl.kernel, mesh=_sc_mesh(),
        out_type=jax.ShapeDtypeStruct((k, t, d), ys.dtype),
        scratch_types=[pltpu.VMEM((TOP_K, SC_CHUNK), I32), pltpu.VMEM((TOP_K, SC_CHUNK, d), ys.dtype),
                       pltpu.SemaphoreType.DMA],
        name="combine",
    )
    def run(ys_hbm, slot_hbm, out_hbm, idx_v, rows_v, sem):
        wid = lax.axis_index("s") * SC_CORES + lax.axis_index("c")

        @pl.loop(0, n_chunks)
        def _(ci):
            base = pl.multiple_of(wid * per_worker + ci * SC_CHUNK, SC_CHUNK)
            for kk in range(TOP_K):
                pltpu.sync_copy(slot_hbm.at[kk, pl.ds(base, SC_CHUNK)], idx_v.at[kk])
            copies = [pltpu.make_async_copy(ys_hbm.at[idx_v.at[kk]], rows_v.at[kk], sem) for kk in range(TOP_K)]
            for cp in copies:
                cp.start()
            for cp in copies:
                cp.wait()
            for kk in range(TOP_K):
                pltpu.sync_copy(rows_v.at[kk], out_hbm.at[kk, pl.ds(base, SC_CHUNK)])

    return run(ys, slot)


def _split_up_kernel(w_ref, glu_ref, lin_ref):
    tile = 2 * LANES
    src = lax.broadcasted_iota(I32, (tile, tile), 0)
    dst = lax.broadcasted_iota(I32, (tile, tile), 1)
    wanted = jnp.where(dst < LANES, 2 * dst, 2 * (dst - LANES) + 1)
    pick = jnp.where(src == wanted, 1.0, 0.0).astype(BF16)
    for c in range(w_ref.shape[2] // tile):
        w = w_ref[0, :, c * tile:(c + 1) * tile].astype(BF16)
        both = jnp.dot(w, pick, preferred_element_type=F32)
        glu_ref[0, :, c * LANES:(c + 1) * LANES] = both[:, :LANES].astype(BF16)
        lin_ref[0, :, c * LANES:(c + 1) * LANES] = both[:, LANES:].astype(BF16)


def _split_up(w_up):
    e, d, f2 = w_up.shape
    cols = 512
    out_spec = pl.BlockSpec((1, d, cols // 2), lambda i, j: (i, 0, j))
    return pl.pallas_call(
        _split_up_kernel,
        grid=(e, f2 // cols),
        in_specs=[pl.BlockSpec((1, d, cols), lambda i, j: (i, 0, j))],
        out_specs=[out_spec, out_spec],
        out_shape=[jax.ShapeDtypeStruct((e, d, f2 // 2), BF16)] * 2,
        compiler_params=_cparams(("parallel", "parallel")),
        name="split_up",
    )(w_up)


def _experts_kernel(be_ref, nused_ref, x_ref, wg_ref, wl_ref, bg_ref, bl_ref, wd_ref, bd_ref, y_ref):
    @pl.when(pl.program_id(0) < nused_ref[0])
    def _():
        x = x_ref[...].astype(BF16)
        glu = jnp.dot(x, wg_ref[0], preferred_element_type=F32) + bg_ref[0]
        lin = jnp.dot(x, wl_ref[0], preferred_element_type=F32) + bl_ref[0]
        glu = jnp.minimum(glu, SWIGLU_LIMIT)
        lin = jnp.clip(lin, -SWIGLU_LIMIT, SWIGLU_LIMIT)
        act = glu * jax.nn.sigmoid(SWIGLU_ALPHA * glu) * (lin + 1.0)
        y_ref[...] = jnp.dot(act.astype(BF16), wd_ref[0], preferred_element_type=F32) + bd_ref[0]


def _experts(xs, block_expert, n_used, w_glu, w_lin, b_glu, b_lin, w_down, b_down):
    n_slots, d = xs.shape
    f = w_glu.shape[2]
    nblk = n_slots // MOE_ROWS
    rows = lambda i, be, nu: (i, 0)
    by_e = lambda i, be, nu: (be[i], 0, 0)
    return pl.pallas_call(
        _experts_kernel,
        grid_spec=pltpu.PrefetchScalarGridSpec(
            num_scalar_prefetch=2, grid=(nblk,),
            in_specs=[pl.BlockSpec((MOE_ROWS, d), rows),
                      pl.BlockSpec((1, d, f), by_e), pl.BlockSpec((1, d, f), by_e),
                      pl.BlockSpec((1, 1, f), by_e), pl.BlockSpec((1, 1, f), by_e),
                      pl.BlockSpec((1, f, d), by_e), pl.BlockSpec((1, 1, d), by_e)],
            out_specs=pl.BlockSpec((MOE_ROWS, d), rows)),
        out_shape=jax.ShapeDtypeStruct((n_slots, d), F32),
        compiler_params=_cparams(("arbitrary",)),
        name="experts",
    )(block_expert, n_used, xs, w_glu, w_lin, b_glu, b_lin, w_down, b_down)


def _final_kernel(g_ref, gate_ref, x1_ref, mod_ref, gain_ref, o_ref):
    tm = x1_ref.shape[0]
    gates = gate_ref[...]
    padded = jnp.concatenate([gates, jnp.zeros((LANES - TOP_K, tm), F32)], axis=0)
    gate_cols = padded.T
    y = jnp.zeros(x1_ref.shape, F32)
    for kk in range(TOP_K):
        y = y + g_ref[kk].astype(F32) * gate_cols[:, kk:kk + 1]
    o_ref[...] = x1_ref[...] + mod_ref[0, 5:6, :] * _rms(y, gain_ref[...])


def _final(g, gates, x1, mod3, gain, seq):
    t, d = x1.shape
    tm = ROW_BLOCK
    per_seq = seq // tm
    row = lambda i: (i, 0)
    return pl.pallas_call(
        _final_kernel,
        grid=(t // tm,),
        in_specs=[pl.BlockSpec((TOP_K, tm, d), lambda i: (0, i, 0)),
                  pl.BlockSpec((TOP_K, tm), lambda i: (0, i)),
                  pl.BlockSpec((tm, d), row),
                  pl.BlockSpec((1, 6, d), lambda i: (i // per_seq, 0, 0)),
                  pl.BlockSpec((1, d), lambda i: (0, 0))],
        out_specs=pl.BlockSpec((tm, d), row),
        out_shape=jax.ShapeDtypeStruct((t, d), F32),
        compiler_params=_cparams(("parallel",)),
        name="final",
    )(g, gates, x1, mod3, gain.reshape(1, d))


def _layer(x2, c, positions, seq, ada_w, ada_b, norm_mix_pre, norm_mix_post, norm_ffn_pre, norm_ffn_post,
           w_in, w_branch_a, w_branch_b, w_out, w_router, b_router, w_up, b_up, w_down, b_down):
    t, d = x2.shape
    batch = t // seq
    mod3 = _adaln(c, ada_w, ada_b).reshape(batch, 6, d)

    qa, ka, va, qb, kb, vb, ga, gb = _in_proj(x2, mod3, norm_mix_pre, positions, w_in.astype(BF16), seq)
    dil = [_dilated_group(qa, ka, va, batch, seq, g) for g in range(len(DIL_GROUPS))]
    sb = _stickbreak(qb, kb, vb, batch, seq)
    x1, h2, idx, gates, rank, counts = _post_mix(
        dil, sb, ga, gb, x2, mod3, norm_mix_post, norm_ffn_pre,
        w_branch_a.astype(BF16), w_branch_b.astype(BF16), w_out.astype(BF16), w_router, b_router, seq)

    counts = counts.reshape(N_EXPERTS)
    padded = (counts + MOE_ROWS - 1) // MOE_ROWS * MOE_ROWS
    pad_end = jnp.cumsum(padded)
    pad_start = (pad_end - padded).astype(I32)
    nblk = (t * TOP_K) // MOE_ROWS + N_EXPERTS
    block_first_row = jnp.arange(nblk, dtype=I32)[:, None] * MOE_ROWS
    block_expert = jnp.minimum(jnp.sum(pad_end[None, :] <= block_first_row, axis=1), N_EXPERTS - 1).astype(I32)
    n_used = (pad_end[-1:] // MOE_ROWS).astype(I32)

    slot = _slots(pad_start, idx, rank)
    xs = _sc_dispatch(h2, slot, nblk * MOE_ROWS)
    f = w_up.shape[2] // 2
    w_glu, w_lin = _split_up(w_up)
    ys = _experts(xs, block_expert, n_used, w_glu, w_lin,
                  b_up[:, 0::2].reshape(N_EXPERTS, 1, f), b_up[:, 1::2].reshape(N_EXPERTS, 1, f),
                  w_down.astype(BF16), b_down.reshape(N_EXPERTS, 1, d))
    g = _sc_combine(ys, slot)
    return _final(g, gates, x1, mod3, norm_ffn_post, seq)


def kernel(x, c, positions, ada_w, ada_b, norm_mix_pre, norm_mix_post, norm_ffn_pre, norm_ffn_post,
           w_in, w_branch_a, w_branch_b, w_out, w_router, b_router, w_up, b_up, w_down, b_down):
    batch, seq, d = x.shape
    x2 = x.reshape(batch * seq, d)
    for layer in range(ada_w.shape[0]):
        x2 = _layer(x2, c, positions, seq, ada_w[layer], ada_b[layer], norm_mix_pre[layer], norm_mix_post[layer],
                    norm_ffn_pre[layer], norm_ffn_post[layer], w_in[layer], w_branch_a[layer], w_branch_b[layer],
                    w_out[layer], w_router[layer], b_router[layer], w_up[layer], b_up[layer], w_down[layer],
                    b_down[layer])
    return x2.reshape(batch, seq, d)
```

```python
import functools

import numpy as np
import jax
import jax.numpy as jnp
from jax import lax
from jax.experimental import pallas as pl
from jax.experimental.pallas import tpu as pltpu
from jax.experimental.pallas import tpu_sc as plsc

F32 = jnp.float32
BF16 = jnp.bfloat16
I32 = jnp.int32
U32 = jnp.uint32

HEAD_DIM = 64
DIL_GROUPS = ((128, 1), (512, 4), (2048, 16))
DIL_HEADS_PER_GROUP = 4
DIL_HEADS = DIL_HEADS_PER_GROUP * len(DIL_GROUPS)
DIL_W = DIL_HEADS * HEAD_DIM
DIL_OUT = DIL_HEADS_PER_GROUP * HEAD_DIM
SB_HEADS = 8
SB_W = SB_HEADS * HEAD_DIM
ROPE_THETA = 500000.0
ROPE_DIMS = HEAD_DIM // 4
Q_BLOCK = 128
N_EXPERTS = 32
TOP_K = 4
SWIGLU_ALPHA = 1.702
SWIGLU_LIMIT = 7.0
NORM_EPS = 1e-6
NEG_INF = -1e30

LANES = 128
ROW_BLOCK = 512
MOE_ROWS = 512
DIL_TOKENS = 512
SB_TQ = 256
SB_LOG_FLOOR = -105.0
VMEM_LIMIT = 56 * 1024 * 1024

SC_CORES = 2
SC_SUBCORES = 16
SC_WORKERS = SC_CORES * SC_SUBCORES
SC_DISPATCH_CHUNK = 64
SC_COMBINE_CHUNK = 32


def _cparams(sem):
    return pltpu.CompilerParams(dimension_semantics=sem, vmem_limit_bytes=VMEM_LIMIT)


def _rms(x, gain):
    ms = jnp.mean(x * x, axis=-1, keepdims=True)
    return x * lax.rsqrt(ms + NORM_EPS) * gain


def _pack_bf16_pairs(x):
    n = x.shape[1] // 2
    u = lax.bitcast_convert_type(x, U32)
    r = (u + jnp.uint32(0x7FFF) + ((u >> 16) & jnp.uint32(1))) >> 16
    return r[:, :n] | (r[:, n:] << 16)


def _unpack_bf16_pairs(w):
    lo = lax.bitcast_convert_type(w << 16, F32)
    hi = lax.bitcast_convert_type(w & jnp.uint32(0xFFFF0000), F32)
    return lo, hi


def _adaln_kernel(c_ref, w_ref, b_ref, o_ref):
    c = c_ref[...]
    s = c * jax.nn.sigmoid(c)
    o_ref[...] = jnp.dot(s, w_ref[...], preferred_element_type=F32,
                         precision=lax.Precision.HIGHEST) + b_ref[...]


def _adaln(c, ada_w, ada_b):
    b, d = c.shape
    n = ada_w.shape[1]
    return pl.pallas_call(
        _adaln_kernel,
        grid=(n // d,),
        in_specs=[pl.BlockSpec((b, d), lambda j: (0, 0)),
                  pl.BlockSpec((d, d), lambda j: (0, j)),
                  pl.BlockSpec((1, d), lambda j: (0, j))],
        out_specs=pl.BlockSpec((b, d), lambda j: (0, j)),
        out_shape=jax.ShapeDtypeStruct((b, n), F32),
        compiler_params=_cparams(("arbitrary",)),
        name="adaln",
    )(c, ada_w, ada_b.reshape(1, n))


def _rope_table():
    half = ROPE_DIMS // 2
    inv_freq = ROPE_THETA ** (-(np.arange(half, dtype=np.float32) * 2.0 / ROPE_DIMS))
    d = np.arange(LANES) % HEAD_DIM
    tab = np.zeros((8, LANES), np.float32)
    tab[0] = np.where(d < ROPE_DIMS, inv_freq[d % half], 0.0)
    tab[1] = np.where(d < half, -1.0, 1.0)
    tab[2] = np.where(d < half, 1.0, 0.0)
    return jnp.asarray(tab)


def _inproj_kernel(x_ref, mod_ref, g_ref, pos_ref, rope_ref, w_ref,
                   qa_ref, ka_ref, va_ref, qb_ref, kb_ref, vb_ref, ga_ref, gb_ref):
    x = x_ref[...]
    tm = x.shape[0]
    shift = mod_ref[0, 0:1, :]
    scale = mod_ref[0, 1:2, :]
    hb = (_rms(x, g_ref[...]) * (1.0 + scale) + shift).astype(BF16)

    pos = pos_ref[0].astype(F32)
    pos_rows = jnp.broadcast_to(pos, (LANES, tm)).T
    ang = pos_rows * rope_ref[0:1, :]
    cos = jnp.cos(ang)
    sin = jnp.sin(ang) * rope_ref[1:2, :]
    first = rope_ref[2:3, :] > 0.5
    half = ROPE_DIMS // 2

    def rope(t):
        partner = jnp.where(first, pltpu.roll(t, LANES - half, 1), pltpu.roll(t, half, 1))
        return t * cos + partner * sin

    qk_scale = HEAD_DIM ** -0.5
    col = 0
    plan = ((qa_ref, DIL_W, "rope_q"), (ka_ref, DIL_W, "rope"), (va_ref, DIL_W, "id"),
            (qb_ref, SB_W, "scale"), (kb_ref, SB_W, "id"), (vb_ref, SB_W, "id"),
            (ga_ref, x.shape[1], "sig"), (gb_ref, x.shape[1], "sig"))
    for ref, width, mode in plan:
        for c0 in range(0, width, 2 * LANES):
            t = jnp.dot(hb, w_ref[:, col + c0:col + c0 + 2 * LANES], preferred_element_type=F32)
            if mode in ("rope", "rope_q"):
                parts = [rope(t[:, :LANES]), rope(t[:, LANES:])]
                t = jnp.concatenate(parts, axis=1)
                if mode == "rope_q":
                    t = t * qk_scale
            elif mode == "scale":
                t = t * qk_scale
            elif mode == "sig":
                t = jax.nn.sigmoid(t)
            ref[:, c0:c0 + 2 * LANES] = t.astype(ref.dtype)
        col += width


def _in_proj(x2, mod3, gain, positions, w_in_bf16, seq):
    t, d = x2.shape
    tm = ROW_BLOCK
    nb = t // tm
    per_seq = seq // tm
    pos3 = positions.reshape(nb, 1, tm)
    widths = (DIL_W, DIL_W, DIL_W, SB_W, SB_W, SB_W, d, d)
    row = lambda i: (i, 0)
    return pl.pallas_call(
        _inproj_kernel,
        grid=(nb,),
        in_specs=[pl.BlockSpec((tm, d), row),
                  pl.BlockSpec((1, 6, d), lambda i: (i // per_seq, 0, 0)),
                  pl.BlockSpec((1, d), lambda i: (0, 0)),
                  pl.BlockSpec((1, 1, tm), lambda i: (i, 0, 0)),
                  pl.BlockSpec((8, LANES), lambda i: (0, 0)),
                  pl.BlockSpec(w_in_bf16.shape, lambda i: (0, 0))],
        out_specs=[pl.BlockSpec((tm, w), row) for w in widths],
        out_shape=[jax.ShapeDtypeStruct((t, w), BF16) for w in widths],
        compiler_params=_cparams(("parallel",)),
        name="in_proj",
    )(x2, mod3, gain.reshape(1, d), pos3, _rope_table(), w_in_bf16)


def _dilated_kernel(q_ref, kp_ref, kc_ref, vp_ref, vc_ref, o_ref, l_ref,
                    qf_ref, kf_ref, vf_ref, of_ref, lf_ref, *, r, nq):
    n = pl.program_id(1)
    prev_rows = Q_BLOCK * r
    halves = DIL_OUT // LANES
    for c in range(halves):
        cols = slice(c * LANES, (c + 1) * LANES)
        qf_ref[c] = q_ref[:, cols].astype(F32)
        kf_ref[c, 0:prev_rows, :] = kp_ref[:, cols].astype(F32)
        kf_ref[c, prev_rows:, :] = kc_ref[:, cols].astype(F32)
        vf_ref[c, 0:prev_rows, :] = vp_ref[:, cols].astype(F32)
        vf_ref[c, prev_rows:, :] = vc_ref[:, cols].astype(F32)

    def gather_rows(ref, start, size):
        idx = pl.ds(start, size, stride=r) if r > 1 else pl.ds(start, size)
        return jnp.concatenate([ref[c, idx, :] for c in range(halves)], axis=1).astype(BF16)

    def scatter_rows(ref, start, val):
        idx = pl.ds(start, Q_BLOCK, stride=r) if r > 1 else pl.ds(start, Q_BLOCK)
        for c in range(halves):
            ref[c, idx, :] = val[:, c * LANES:(c + 1) * LANES]

    qi = lax.broadcasted_iota(I32, (Q_BLOCK, 2 * Q_BLOCK), 0) + Q_BLOCK
    ki = lax.broadcasted_iota(I32, (Q_BLOCK, 2 * Q_BLOCK), 1)
    dist = qi - ki
    band = (dist >= 0) & (dist <= Q_BLOCK)
    lane = lax.broadcasted_iota(I32, (1, DIL_OUT), 1)

    heads = [(lane >= h * HEAD_DIM) & (lane < (h + 1) * HEAD_DIM) for h in range(DIL_HEADS_PER_GROUP)]
    ones = jnp.ones((2 * Q_BLOCK, LANES), BF16)

    def sub_block(idx, carry):
        rho = idx % r
        qb = idx // r
        base = r * Q_BLOCK * qb + rho
        q = gather_rows(qf_ref, base, Q_BLOCK)
        k = gather_rows(kf_ref, base, 2 * Q_BLOCK)
        v = gather_rows(vf_ref, base, 2 * Q_BLOCK)
        valid = band & ((n * nq + qb - 1) * Q_BLOCK + ki >= 0)
        qs = jnp.concatenate([jnp.where(heads[h], q, jnp.zeros_like(q)) for h in range(DIL_HEADS_PER_GROUP)], axis=0)
        s = lax.dot_general(qs, k, (((1,), (1,)), ((), ())), preferred_element_type=F32)
        s = jnp.where(jnp.concatenate([valid] * DIL_HEADS_PER_GROUP, axis=0), s, NEG_INF)
        m = jnp.max(s, axis=1, keepdims=True)
        p = jnp.exp(s - m).astype(BF16)
        pv = jnp.dot(p, v, preferred_element_type=F32)
        den = jnp.dot(p, ones, preferred_element_type=F32)
        inv = 1.0 / den
        lse = m + jnp.log(den)
        o = jnp.zeros((Q_BLOCK, DIL_OUT), F32)
        l = jnp.zeros((Q_BLOCK, DIL_OUT), F32)
        for h in range(DIL_HEADS_PER_GROUP):
            blk = slice(h * Q_BLOCK, (h + 1) * Q_BLOCK)
            o = jnp.where(heads[h], pv[blk] * jnp.concatenate([inv[blk]] * halves, axis=1), o)
            l = jnp.where(heads[h], jnp.concatenate([lse[blk]] * halves, axis=1), l)
        scatter_rows(of_ref, base, o)
        scatter_rows(lf_ref, base, l)
        return carry

    lax.fori_loop(0, r * nq, sub_block, 0, unroll=2)
    for c in range(halves):
        o_ref[:, c * LANES:(c + 1) * LANES] = of_ref[c]
        l_ref[:, c * LANES:(c + 1) * LANES] = lf_ref[c]


def _dilated_group(qa, ka, va, batch, seq, group):
    _, r = DIL_GROUPS[group]
    nq = max(1, DIL_TOKENS // (Q_BLOCK * r))
    prev_rows = Q_BLOCK * r
    cur_rows = prev_rows * nq
    halves = DIL_OUT // LANES
    view = lambda a: a.reshape(batch, seq, DIL_W)
    cur = lambda b, n: (b, n, group)
    prev = lambda b, n: (b, jnp.maximum(n * nq - 1, 0), group)
    cur_spec = pl.BlockSpec((None, cur_rows, DIL_OUT), cur)
    prev_spec = pl.BlockSpec((None, prev_rows, DIL_OUT), prev)
    out_spec = pl.BlockSpec((None, cur_rows, DIL_OUT), lambda b, n: (b, n, 0))
    o, l = pl.pallas_call(
        functools.partial(_dilated_kernel, r=r, nq=nq),
        grid=(batch, seq // cur_rows),
        in_specs=[cur_spec, prev_spec, cur_spec, prev_spec, cur_spec],
        out_specs=[out_spec, out_spec],
        out_shape=[jax.ShapeDtypeStruct((batch, seq, DIL_OUT), F32)] * 2,
        scratch_shapes=[pltpu.VMEM((halves, cur_rows, LANES), F32),
                        pltpu.VMEM((halves, prev_rows + cur_rows, LANES), F32),
                        pltpu.VMEM((halves, prev_rows + cur_rows, LANES), F32),
                        pltpu.VMEM((halves, cur_rows, LANES), F32),
                        pltpu.VMEM((halves, cur_rows, LANES), F32)],
        compiler_params=_cparams(("parallel", "arbitrary")),
        name=f"dilated_g{group}",
    )(view(qa), view(ka), view(ka), view(va), view(va))
    return o.reshape(batch * seq, DIL_OUT), l.reshape(batch * seq, DIL_OUT)


def _stickbreak_kernel(q_ref, k_ref, v_ref, o_ref, acc_ref, csum_ref):
    i = pl.program_id(2)
    tq = SB_TQ
    q = q_ref[...]
    lane = lax.broadcasted_iota(I32, (1, LANES), 1)
    zero = jnp.zeros_like(q)
    qs = jnp.concatenate([jnp.where(lane < HEAD_DIM, q, zero), jnp.where(lane >= HEAD_DIM, q, zero)], axis=0)
    row = lax.broadcasted_iota(I32, (2 * tq, tq), 0)
    causal = lax.broadcasted_iota(I32, (2 * tq, tq), 1) < jnp.where(row >= tq, row - tq, row)
    later = (lax.broadcasted_iota(I32, (tq, tq), 0) > lax.broadcasted_iota(I32, (tq, tq), 1))
    later = jnp.where(later, 1.0, 0.0).astype(BF16)
    acc_ref[...] = jnp.zeros_like(acc_ref)
    csum_ref[...] = jnp.zeros_like(csum_ref)

    def sweep(j, diagonal):
        start = pl.multiple_of(j * tq, tq)
        k = k_ref[pl.ds(start, tq), :]
        v = v_ref[pl.ds(start, tq), :]
        z = lax.dot_general(qs, k, (((1,), (1,)), ((), ())), preferred_element_type=F32)
        sp = jnp.maximum(z, 0.0) + jnp.log(1.0 + jnp.exp(-jnp.abs(z)))
        log_not = jnp.where(causal, -sp, 0.0) if diagonal else -sp
        hi = log_not.astype(BF16)
        lo = (log_not - hi.astype(F32)).astype(BF16)
        inner = (jnp.dot(hi, later, preferred_element_type=F32)
                 + jnp.dot(lo, later, preferred_element_type=F32))
        csum = csum_ref[...]
        w = jnp.exp((z - sp) + inner + csum)
        if diagonal:
            w = jnp.where(causal, w, 0.0)
        acc_ref[...] += jnp.dot(w.astype(BF16), v, preferred_element_type=F32)
        csum = csum + jnp.sum(log_not, axis=1, keepdims=True)
        csum_ref[...] = csum
        return jnp.max(csum)

    def cond(state):
        j, top = state
        return (j >= 0) & (top > SB_LOG_FLOOR)

    def body(state):
        j, _ = state
        return j - 1, sweep(j, False)

    lax.while_loop(cond, body, (i - 1, sweep(i, True)))
    acc = acc_ref[...]
    o_ref[...] = jnp.where(lane < HEAD_DIM, acc[:tq], acc[tq:]).astype(o_ref.dtype)


def _stickbreak(qb, kb, vb, batch, seq):
    view = lambda a: a.reshape(batch, seq, SB_W)
    nq = seq // SB_TQ
    pairs = SB_W // LANES
    kv_spec = pl.BlockSpec((None, seq, LANES), lambda b, hp, i: (b, 0, hp))
    q_spec = pl.BlockSpec((None, SB_TQ, LANES), lambda b, hp, i: (b, i, hp))
    out = pl.pallas_call(
        _stickbreak_kernel,
        grid=(batch, pairs, nq),
        in_specs=[q_spec, kv_spec, kv_spec],
        out_specs=q_spec,
        out_shape=jax.ShapeDtypeStruct((batch, seq, SB_W), BF16),
        scratch_shapes=[pltpu.VMEM((2 * SB_TQ, LANES), F32), pltpu.VMEM((2 * SB_TQ, 1), F32)],
        compiler_params=_cparams(("parallel", "parallel", "arbitrary")),
        name="stickbreak",
    )(view(qb), view(kb), view(vb))
    return out.reshape(batch * seq, SB_W)


def _postmix_kernel(o0_ref, o1_ref, o2_ref, l0_ref, l1_ref, l2_ref, sb_ref, ga_ref, gb_ref, x_ref,
                    mod_ref, gpost_ref, gffn_ref, wa_ref, wb_ref, wo_ref, wr_ref, br_ref,
                    x1_ref, h2_ref, idx_ref, gate_ref, rank_ref, cnt_ref, carry_ref):
    step = pl.program_id(0)
    tm = x_ref.shape[0]

    @pl.when(step == 0)
    def _():
        carry_ref[...] = jnp.zeros_like(carry_ref)

    l0, l1, l2 = l0_ref[...], l1_ref[...], l2_ref[...]
    m = jnp.maximum(jnp.maximum(l0, l1), l2)
    e0, e1, e2 = jnp.exp(l0 - m), jnp.exp(l1 - m), jnp.exp(l2 - m)
    merged = (e0 * o0_ref[...] + e1 * o1_ref[...] + e2 * o2_ref[...]) / (e0 + e1 + e2)
    ya = jnp.dot(merged.astype(BF16), wa_ref[...], preferred_element_type=F32)
    yb = jnp.dot(sb_ref[...], wb_ref[...], preferred_element_type=F32)
    mix = ga_ref[...].astype(F32) * ya + gb_ref[...].astype(F32) * yb
    y = jnp.dot(mix.astype(BF16), wo_ref[...], preferred_element_type=F32)
    gate1 = mod_ref[0, 2:3, :]
    x1 = x_ref[...] + gate1 * _rms(y, gpost_ref[...])
    x1_ref[...] = x1
    h2 = _rms(x1, gffn_ref[...]) * (1.0 + mod_ref[0, 4:5, :]) + mod_ref[0, 3:4, :]
    h2_ref[...] = _pack_bf16_pairs(h2)

    logits = lax.dot_general(wr_ref[...], h2, (((1,), (1,)), ((), ())), preferred_element_type=F32,
                             precision=lax.Precision.HIGHEST) + br_ref[...]
    e_iota = lax.broadcasted_iota(I32, (N_EXPERTS, tm), 0)
    before = (lax.broadcasted_iota(I32, (tm, tm), 0) < lax.broadcasted_iota(I32, (tm, tm), 1))
    before = jnp.where(before, 1.0, 0.0).astype(BF16)

    picks, vals = [], []
    work = logits
    chosen = jnp.zeros((N_EXPERTS, tm), F32)
    for _ in range(TOP_K):
        top = jnp.max(work, axis=0, keepdims=True)
        idx = jnp.min(jnp.where(work == top, e_iota, N_EXPERTS), axis=0, keepdims=True)
        one = e_iota == idx
        picks.append((idx, one))
        vals.append(top)
        chosen = jnp.where(one, 1.0, chosen)
        work = jnp.where(one, -jnp.inf, work)
    exps = [jnp.exp(v - vals[0]) for v in vals]
    den = exps[0] + exps[1] + exps[2] + exps[3]

    rank = jnp.dot(chosen.astype(BF16), before, preferred_element_type=F32) + carry_ref[...]
    carry_ref[...] += jnp.sum(chosen, axis=1, keepdims=True)
    cnt_ref[...] = carry_ref[...].astype(I32)
    for kk, (idx, one) in enumerate(picks):
        idx_ref[kk:kk + 1, :] = idx
        gate_ref[kk:kk + 1, :] = exps[kk] / den
        rank_ref[kk:kk + 1, :] = jnp.sum(jnp.where(one, rank, 0.0), axis=0, keepdims=True).astype(I32)


def _post_mix(dil, sb, ga, gb, x2, mod3, gpost, gffn, wa, wb, wo, w_router, b_router, seq):
    t, d = x2.shape
    tm = ROW_BLOCK
    nb = t // tm
    per_seq = seq // tm
    row = lambda i: (i, 0)
    const = lambda i: (0, 0)
    colblk = lambda i: (0, i)
    (o0, l0), (o1, l1), (o2, l2) = dil
    in_specs = ([pl.BlockSpec((tm, DIL_OUT), row)] * 6
                + [pl.BlockSpec((tm, SB_W), row), pl.BlockSpec((tm, d), row), pl.BlockSpec((tm, d), row),
                   pl.BlockSpec((tm, d), row),
                   pl.BlockSpec((1, 6, d), lambda i: (i // per_seq, 0, 0)),
                   pl.BlockSpec((1, d), const), pl.BlockSpec((1, d), const),
                   pl.BlockSpec(wa.shape, const), pl.BlockSpec(wb.shape, const), pl.BlockSpec(wo.shape, const),
                   pl.BlockSpec((N_EXPERTS, d), const), pl.BlockSpec((N_EXPERTS, 1), const)])
    out_specs = [pl.BlockSpec((tm, d), row), pl.BlockSpec((tm, d // 2), row),
                 pl.BlockSpec((TOP_K, tm), colblk), pl.BlockSpec((TOP_K, tm), colblk),
                 pl.BlockSpec((TOP_K, tm), colblk), pl.BlockSpec((N_EXPERTS, 1), const)]
    out_shape = [jax.ShapeDtypeStruct((t, d), F32), jax.ShapeDtypeStruct((t, d // 2), U32),
                 jax.ShapeDtypeStruct((TOP_K, t), I32), jax.ShapeDtypeStruct((TOP_K, t), F32),
                 jax.ShapeDtypeStruct((TOP_K, t), I32), jax.ShapeDtypeStruct((N_EXPERTS, 1), I32)]
    return pl.pallas_call(
        _postmix_kernel,
        grid=(nb,),
        in_specs=in_specs,
        out_specs=out_specs,
        out_shape=out_shape,
        scratch_shapes=[pltpu.VMEM((N_EXPERTS, 1), F32)],
        compiler_params=_cparams(("arbitrary",)),
        name="post_mix",
    )(o0, o1, o2, l0, l1, l2, sb, ga, gb, x2, mod3, gpost.reshape(1, d), gffn.reshape(1, d),
      wa, wb, wo, w_router.T, b_router.reshape(N_EXPERTS, 1))


def _slots_kernel(start_ref, idx_ref, rank_ref, slot_ref):
    idx = idx_ref[...]
    base = jnp.zeros_like(idx)
    for e in range(N_EXPERTS):
        base = jnp.where(idx == e, start_ref[e], base)
    slot_ref[...] = base + rank_ref[...]


def _slots(pad_start, idx, rank):
    k, t = idx.shape
    tb = min(t, 8192)
    blk = pl.BlockSpec((k, tb), lambda i, s: (0, i))
    return pl.pallas_call(
        _slots_kernel,
        grid_spec=pltpu.PrefetchScalarGridSpec(
            num_scalar_prefetch=1, grid=(t // tb,), in_specs=[blk, blk], out_specs=blk),
        out_shape=jax.ShapeDtypeStruct((k, t), I32),
        compiler_params=_cparams(("parallel",)),
        name="slots",
    )(pad_start, idx, rank)


def _sc_mesh():
    return plsc.VectorSubcoreMesh(core_axis_name="c", subcore_axis_name="s")


def _sc_dispatch(rows, slot, n_slots):
    t, d = rows.shape
    chunk = SC_DISPATCH_CHUNK
    per_worker = t // SC_WORKERS
    n_chunks = per_worker // chunk

    @functools.partial(
        pl.kernel, mesh=_sc_mesh(),
        out_type=jax.ShapeDtypeStruct((n_slots, d), rows.dtype),
        scratch_types=[pltpu.VMEM((TOP_K, chunk), I32), pltpu.VMEM((chunk, d), rows.dtype),
                       pltpu.SemaphoreType.DMA],
        name="dispatch",
    )
    def run(rows_hbm, slot_hbm, out_hbm, idx_v, rows_v, sem):
        wid = lax.axis_index("s") * SC_CORES + lax.axis_index("c")

        @pl.loop(0, n_chunks)
        def _(ci):
            base = pl.multiple_of(wid * per_worker + ci * chunk, chunk)
            loads = [pltpu.make_async_copy(rows_hbm.at[pl.ds(base, chunk)], rows_v, sem)]
            loads += [pltpu.make_async_copy(slot_hbm.at[kk, pl.ds(base, chunk)], idx_v.at[kk], sem)
                      for kk in range(TOP_K)]
            for cp in loads:
                cp.start()
            for cp in loads:
                cp.wait()
            copies = [pltpu.make_async_copy(rows_v, out_hbm.at[idx_v.at[kk]], sem) for kk in range(TOP_K)]
            for cp in copies:
                cp.start()
            for cp in copies:
                cp.wait()

    return run(rows, slot)


def _sc_combine(ys, slot):
    _, d = ys.shape
    k, t = slot.shape
    chunk = SC_COMBINE_CHUNK
    per_worker = t // SC_WORKERS
    n_chunks = per_worker // chunk

    @functools.partial(
        pl.kernel, mesh=_sc_mesh(),
        out_type=jax.ShapeDtypeStruct((k, t, d), ys.dtype),
        scratch_types=[pltpu.VMEM((TOP_K, chunk), I32), pltpu.VMEM((TOP_K, chunk, d), ys.dtype),
                       pltpu.SemaphoreType.DMA],
        name="combine",
    )
    def run(ys_hbm, slot_hbm, out_hbm, idx_v, rows_v, sem):
        wid = lax.axis_index("s") * SC_CORES + lax.axis_index("c")

        @pl.loop(0, n_chunks)
        def _(ci):
            base = pl.multiple_of(wid * per_worker + ci * chunk, chunk)
            loads = [pltpu.make_async_copy(slot_hbm.at[kk, pl.ds(base, chunk)], idx_v.at[kk], sem)
                     for kk in range(TOP_K)]
            for cp in loads:
                cp.start()
            for cp in loads:
                cp.wait()
            gathers = [pltpu.make_async_copy(ys_hbm.at[idx_v.at[kk]], rows_v.at[kk], sem) for kk in range(TOP_K)]
            for cp in gathers:
                cp.start()
            for cp in gathers:
                cp.wait()
            stores = [pltpu.make_async_copy(rows_v.at[kk], out_hbm.at[kk, pl.ds(base, chunk)], sem)
                      for kk in range(TOP_K)]
            for cp in stores:
                cp.start()
            for cp in stores:
                cp.wait()

    return run(ys, slot)


def _split_up_kernel(w_ref, glu_ref, lin_ref):
    tile = 2 * LANES
    src = lax.broadcasted_iota(I32, (tile, tile), 0)
    dst = lax.broadcasted_iota(I32, (tile, tile), 1)
    wanted = jnp.where(dst < LANES, 2 * dst, 2 * (dst - LANES) + 1)
    pick = jnp.where(src == wanted, 1.0, 0.0).astype(BF16)
    for c in range(w_ref.shape[2] // tile):
        w = w_ref[0, :, c * tile:(c + 1) * tile].astype(BF16)
        both = jnp.dot(w, pick, preferred_element_type=F32)
        glu_ref[0, :, c * LANES:(c + 1) * LANES] = both[:, :LANES].astype(BF16)
        lin_ref[0, :, c * LANES:(c + 1) * LANES] = both[:, LANES:].astype(BF16)


def _split_up(w_up):
    e, d, f2 = w_up.shape
    cols = 512
    out_spec = pl.BlockSpec((1, d, cols // 2), lambda i, j: (i, 0, j))
    return pl.pallas_call(
        _split_up_kernel,
        grid=(e, f2 // cols),
        in_specs=[pl.BlockSpec((1, d, cols), lambda i, j: (i, 0, j))],
        out_specs=[out_spec, out_spec],
        out_shape=[jax.ShapeDtypeStruct((e, d, f2 // 2), BF16)] * 2,
        compiler_params=_cparams(("parallel", "parallel")),
        name="split_up",
    )(w_up)


def _experts_kernel(be_ref, nused_ref, x_ref, wg_ref, wl_ref, bg_ref, bl_ref, wd_ref, bd_ref, y_ref):
    @pl.when(pl.program_id(0) < nused_ref[0])
    def _():
        lo, hi = _unpack_bf16_pairs(x_ref[...])
        x = jnp.concatenate([lo, hi], axis=1).astype(BF16)
        glu = jnp.dot(x, wg_ref[0], preferred_element_type=F32) + bg_ref[0]
        lin = jnp.dot(x, wl_ref[0], preferred_element_type=F32) + bl_ref[0]
        glu = jnp.minimum(glu, SWIGLU_LIMIT)
        lin = jnp.clip(lin, -SWIGLU_LIMIT, SWIGLU_LIMIT)
        act = glu * jax.nn.sigmoid(SWIGLU_ALPHA * glu) * (lin + 1.0)
        y = jnp.dot(act.astype(BF16), wd_ref[0], preferred_element_type=F32) + bd_ref[0]
        y_ref[...] = _pack_bf16_pairs(y)


def _experts(xs, block_expert, n_used, w_glu, w_lin, b_glu, b_lin, w_down, b_down):
    n_slots, half_d = xs.shape
    _, d, f = w_glu.shape
    nblk = n_slots // MOE_ROWS
    rows = lambda i, be, nu: (i, 0)
    by_e = lambda i, be, nu: (be[i], 0, 0)
    return pl.pallas_call(
        _experts_kernel,
        grid_spec=pltpu.PrefetchScalarGridSpec(
            num_scalar_prefetch=2, grid=(nblk,),
            in_specs=[pl.BlockSpec((MOE_ROWS, half_d), rows),
                      pl.BlockSpec((1, d, f), by_e), pl.BlockSpec((1, d, f), by_e),
                      pl.BlockSpec((1, 1, f), by_e), pl.BlockSpec((1, 1, f), by_e),
                      pl.BlockSpec((1, f, d), by_e), pl.BlockSpec((1, 1, d), by_e)],
            out_specs=pl.BlockSpec((MOE_ROWS, half_d), rows)),
        out_shape=jax.ShapeDtypeStruct((n_slots, half_d), U32),
        compiler_params=_cparams(("arbitrary",)),
        name="experts",
    )(block_expert, n_used, xs, w_glu, w_lin, b_glu, b_lin, w_down, b_down)


def _final_kernel(g_ref, gate_ref, x1_ref, mod_ref, gain_ref, o_ref):
    tm = x1_ref.shape[0]
    gates = gate_ref[...]
    padded = jnp.concatenate([gates, jnp.zeros((LANES - TOP_K, tm), F32)], axis=0)
    gate_cols = padded.T
    y_lo = jnp.zeros(g_ref.shape[1:], F32)
    y_hi = jnp.zeros(g_ref.shape[1:], F32)
    for kk in range(TOP_K):
        lo, hi = _unpack_bf16_pairs(g_ref[kk])
        y_lo = y_lo + lo * gate_cols[:, kk:kk + 1]
        y_hi = y_hi + hi * gate_cols[:, kk:kk + 1]
    y = jnp.concatenate([y_lo, y_hi], axis=1)
    o_ref[...] = x1_ref[...] + mod_ref[0, 5:6, :] * _rms(y, gain_ref[...])


def _final(g, gates, x1, mod3, gain, seq):
    t, d = x1.shape
    tm = ROW_BLOCK
    per_seq = seq // tm
    row = lambda i: (i, 0)
    return pl.pallas_call(
        _final_kernel,
        grid=(t // tm,),
        in_specs=[pl.BlockSpec((TOP_K, tm, d // 2), lambda i: (0, i, 0)),
                  pl.BlockSpec((TOP_K, tm), lambda i: (0, i)),
                  pl.BlockSpec((tm, d), row),
                  pl.BlockSpec((1, 6, d), lambda i: (i // per_seq, 0, 0)),
                  pl.BlockSpec((1, d), lambda i: (0, 0))],
        out_specs=pl.BlockSpec((tm, d), row),
        out_shape=jax.ShapeDtypeStruct((t, d), F32),
        compiler_params=_cparams(("parallel",)),
        name="final",
    )(g, gates, x1, mod3, gain.reshape(1, d))


def _layer(x2, c, positions, seq, ada_w, ada_b, norm_mix_pre, norm_mix_post, norm_ffn_pre, norm_ffn_post,
           w_in, w_branch_a, w_branch_b, w_out, w_router, b_router, w_up, b_up, w_down, b_down):
    t, d = x2.shape
    batch = t // seq
    mod3 = _adaln(c, ada_w, ada_b).reshape(batch, 6, d)

    qa, ka, va, qb, kb, vb, ga, gb = _in_proj(x2, mod3, norm_mix_pre, positions, w_in.astype(BF16), seq)
    dil = [_dilated_group(qa, ka, va, batch, seq, g) for g in range(len(DIL_GROUPS))]
    sb = _stickbreak(qb, kb, vb, batch, seq)
    x1, h2, idx, gates, rank, counts = _post_mix(
        dil, sb, ga, gb, x2, mod3, norm_mix_post, norm_ffn_pre,
        w_branch_a.astype(BF16), w_branch_b.astype(BF16), w_out.astype(BF16), w_router, b_router, seq)

    counts = counts.reshape(N_EXPERTS)
    padded = (counts + MOE_ROWS - 1) // MOE_ROWS * MOE_ROWS
    pad_end = jnp.cumsum(padded)
    pad_start = (pad_end - padded).astype(I32)
    nblk = (t * TOP_K) // MOE_ROWS + N_EXPERTS
    block_first_row = jnp.arange(nblk, dtype=I32)[:, None] * MOE_ROWS
    block_expert = jnp.minimum(jnp.sum(pad_end[None, :] <= block_first_row, axis=1), N_EXPERTS - 1).astype(I32)
    n_used = (pad_end[-1:] // MOE_ROWS).astype(I32)

    slot = _slots(pad_start, idx, rank)
    xs = _sc_dispatch(h2, slot, nblk * MOE_ROWS)
    f = w_up.shape[2] // 2
    w_glu, w_lin = _split_up(w_up)
    ys = _experts(xs, block_expert, n_used, w_glu, w_lin,
                  b_up[:, 0::2].reshape(N_EXPERTS, 1, f), b_up[:, 1::2].reshape(N_EXPERTS, 1, f),
                  w_down.astype(BF16), b_down.reshape(N_EXPERTS, 1, d))
    g = _sc_combine(ys, slot)
    return _final(g, gates, x1, mod3, norm_ffn_post, seq)


def kernel(x, c, positions, ada_w, ada_b, norm_mix_pre, norm_mix_post, norm_ffn_pre, norm_ffn_post,
           w_in, w_branch_a, w_branch_b, w_out, w_router, b_router, w_up, b_up, w_down, b_down):
    batch, seq, d = x.shape
    x2 = x.reshape(batch * seq, d)
    for layer in range(ada_w.shape[0]):
        x2 = _layer(x2, c, positions, seq, ada_w[layer], ada_b[layer], norm_mix_pre[layer], norm_mix_post[layer],
                    norm_ffn_pre[layer], norm_ffn_post[layer], w_in[layer], w_branch_a[layer], w_branch_b[layer],
                    w_out[layer], w_router[layer], b_router[layer], w_up[layer], b_up[layer], w_down[layer],
                    b_down[layer])
    return x2.reshape(batch, seq, d)
```

```python
import functools

import numpy as np
import jax
import jax.numpy as jnp
from jax import lax
from jax.experimental import pallas as pl
from jax.experimental.pallas import tpu as pltpu
from jax.experimental.pallas import tpu_sc as plsc

F32 = jnp.float32
BF16 = jnp.bfloat16
I32 = jnp.int32
U32 = jnp.uint32

HEAD_DIM = 64
DIL_GROUPS = ((128, 1), (512, 4), (2048, 16))
DIL_HEADS_PER_GROUP = 4
DIL_HEADS = DIL_HEADS_PER_GROUP * len(DIL_GROUPS)
DIL_W = DIL_HEADS * HEAD_DIM
DIL_OUT = DIL_HEADS_PER_GROUP * HEAD_DIM
SB_HEADS = 8
SB_W = SB_HEADS * HEAD_DIM
ROPE_THETA = 500000.0
ROPE_DIMS = HEAD_DIM // 4
Q_BLOCK = 128
N_EXPERTS = 32
TOP_K = 4
SWIGLU_ALPHA = 1.702
SWIGLU_LIMIT = 7.0
NORM_EPS = 1e-6
NEG_INF = -1e30

LANES = 128
ROW_BLOCK = 512
MOE_ROWS = 512
DIL_TOKENS = 512
SB_TQ = 256
SB_LOG_FLOOR = -105.0
VMEM_LIMIT = 56 * 1024 * 1024

SC_CORES = 2
SC_SUBCORES = 16
SC_WORKERS = SC_CORES * SC_SUBCORES
SC_DISPATCH_CHUNK = 64
SC_COMBINE_CHUNK = 32


def _cparams(sem):
    return pltpu.CompilerParams(dimension_semantics=sem, vmem_limit_bytes=VMEM_LIMIT)


def _rms(x, gain):
    ms = jnp.mean(x * x, axis=-1, keepdims=True)
    return x * lax.rsqrt(ms + NORM_EPS) * gain


def _pack_bf16_pairs(x):
    n = x.shape[1] // 2
    u = lax.bitcast_convert_type(x, U32)
    r = (u + jnp.uint32(0x7FFF) + ((u >> 16) & jnp.uint32(1))) >> 16
    return r[:, :n] | (r[:, n:] << 16)


def _unpack_bf16_pairs(w):
    lo = lax.bitcast_convert_type(w << 16, F32)
    hi = lax.bitcast_convert_type(w & jnp.uint32(0xFFFF0000), F32)
    return lo, hi


def _adaln_kernel(c_ref, w_ref, b_ref, o_ref):
    c = c_ref[...]
    s = c * jax.nn.sigmoid(c)
    o_ref[...] = jnp.dot(s, w_ref[...], preferred_element_type=F32,
                         precision=lax.Precision.HIGHEST) + b_ref[...]


def _adaln(c, ada_w, ada_b):
    b, d = c.shape
    n = ada_w.shape[1]
    return pl.pallas_call(
        _adaln_kernel,
        grid=(n // d,),
        in_specs=[pl.BlockSpec((b, d), lambda j: (0, 0)),
                  pl.BlockSpec((d, d), lambda j: (0, j)),
                  pl.BlockSpec((1, d), lambda j: (0, j))],
        out_specs=pl.BlockSpec((b, d), lambda j: (0, j)),
        out_shape=jax.ShapeDtypeStruct((b, n), F32),
        compiler_params=_cparams(("arbitrary",)),
        name="adaln",
    )(c, ada_w, ada_b.reshape(1, n))


def _rope_table():
    half = ROPE_DIMS // 2
    inv_freq = ROPE_THETA ** (-(np.arange(half, dtype=np.float32) * 2.0 / ROPE_DIMS))
    d = np.arange(LANES) % HEAD_DIM
    tab = np.zeros((8, LANES), np.float32)
    tab[0] = np.where(d < ROPE_DIMS, inv_freq[d % half], 0.0)
    tab[1] = np.where(d < half, -1.0, 1.0)
    tab[2] = np.where(d < half, 1.0, 0.0)
    return jnp.asarray(tab)


def _inproj_kernel(x_ref, mod_ref, g_ref, pos_ref, rope_ref, w_ref,
                   qa_ref, ka_ref, va_ref, qb_ref, kb_ref, vb_ref, ga_ref, gb_ref):
    x = x_ref[...]
    tm = x.shape[0]
    shift = mod_ref[0, 0:1, :]
    scale = mod_ref[0, 1:2, :]
    hb = (_rms(x, g_ref[...]) * (1.0 + scale) + shift).astype(BF16)

    pos = pos_ref[0].astype(F32)
    pos_rows = jnp.broadcast_to(pos, (LANES, tm)).T
    ang = pos_rows * rope_ref[0:1, :]
    cos = jnp.cos(ang)
    sin = jnp.sin(ang) * rope_ref[1:2, :]
    first = rope_ref[2:3, :] > 0.5
    half = ROPE_DIMS // 2

    def rope(t):
        partner = jnp.where(first, pltpu.roll(t, LANES - half, 1), pltpu.roll(t, half, 1))
        return t * cos + partner * sin

    qk_scale = HEAD_DIM ** -0.5
    col = 0
    plan = ((qa_ref, DIL_W, "rope_q"), (ka_ref, DIL_W, "rope"), (va_ref, DIL_W, "id"),
            (qb_ref, SB_W, "scale"), (kb_ref, SB_W, "id"), (vb_ref, SB_W, "id"),
            (ga_ref, x.shape[1], "sig"), (gb_ref, x.shape[1], "sig"))
    for ref, width, mode in plan:
        for c0 in range(0, width, 2 * LANES):
            t = jnp.dot(hb, w_ref[:, col + c0:col + c0 + 2 * LANES], preferred_element_type=F32)
            if mode in ("rope", "rope_q"):
                parts = [rope(t[:, :LANES]), rope(t[:, LANES:])]
                t = jnp.concatenate(parts, axis=1)
                if mode == "rope_q":
                    t = t * qk_scale
            elif mode == "scale":
                t = t * qk_scale
            elif mode == "sig":
                t = jax.nn.sigmoid(t)
            ref[:, c0:c0 + 2 * LANES] = t.astype(ref.dtype)
        col += width


def _in_proj(x2, mod3, gain, positions, w_in_bf16, seq):
    t, d = x2.shape
    tm = ROW_BLOCK
    nb = t // tm
    per_seq = seq // tm
    pos3 = positions.reshape(nb, 1, tm)
    widths = (DIL_W, DIL_W, DIL_W, SB_W, SB_W, SB_W, d, d)
    row = lambda i: (i, 0)
    return pl.pallas_call(
        _inproj_kernel,
        grid=(nb,),
        in_specs=[pl.BlockSpec((tm, d), row),
                  pl.BlockSpec((1, 6, d), lambda i: (i // per_seq, 0, 0)),
                  pl.BlockSpec((1, d), lambda i: (0, 0)),
                  pl.BlockSpec((1, 1, tm), lambda i: (i, 0, 0)),
                  pl.BlockSpec((8, LANES), lambda i: (0, 0)),
                  pl.BlockSpec(w_in_bf16.shape, lambda i: (0, 0))],
        out_specs=[pl.BlockSpec((tm, w), row) for w in widths],
        out_shape=[jax.ShapeDtypeStruct((t, w), BF16) for w in widths],
        compiler_params=_cparams(("parallel",)),
        name="in_proj",
    )(x2, mod3, gain.reshape(1, d), pos3, _rope_table(), w_in_bf16)


def _dilated_kernel(q_ref, kp_ref, kc_ref, vp_ref, vc_ref, o_ref, l_ref,
                    qf_ref, kf_ref, vf_ref, of_ref, lf_ref, *, r, nq):
    n = pl.program_id(1)
    prev_rows = Q_BLOCK * r
    halves = DIL_OUT // LANES
    for c in range(halves):
        cols = slice(c * LANES, (c + 1) * LANES)
        qf_ref[c] = q_ref[:, cols].astype(F32)
        kf_ref[c, 0:prev_rows, :] = kp_ref[:, cols].astype(F32)
        kf_ref[c, prev_rows:, :] = kc_ref[:, cols].astype(F32)
        vf_ref[c, 0:prev_rows, :] = vp_ref[:, cols].astype(F32)
        vf_ref[c, prev_rows:, :] = vc_ref[:, cols].astype(F32)

    def gather_rows(ref, start, size):
        idx = pl.ds(start, size, stride=r) if r > 1 else pl.ds(start, size)
        return jnp.concatenate([ref[c, idx, :] for c in range(halves)], axis=1).astype(BF16)

    def scatter_rows(ref, start, val):
        idx = pl.ds(start, Q_BLOCK, stride=r) if r > 1 else pl.ds(start, Q_BLOCK)
        for c in range(halves):
            ref[c, idx, :] = val[:, c * LANES:(c + 1) * LANES]

    qi = lax.broadcasted_iota(I32, (Q_BLOCK, 2 * Q_BLOCK), 0) + Q_BLOCK
    ki = lax.broadcasted_iota(I32, (Q_BLOCK, 2 * Q_BLOCK), 1)
    dist = qi - ki
    band = (dist >= 0) & (dist <= Q_BLOCK)
    lane = lax.broadcasted_iota(I32, (1, DIL_OUT), 1)

    heads = [(lane >= h * HEAD_DIM) & (lane < (h + 1) * HEAD_DIM) for h in range(DIL_HEADS_PER_GROUP)]
    ones = jnp.ones((2 * Q_BLOCK, LANES), BF16)

    def sub_block(idx, carry):
        rho = idx % r
        qb = idx // r
        base = r * Q_BLOCK * qb + rho
        q = gather_rows(qf_ref, base, Q_BLOCK)
        k = gather_rows(kf_ref, base, 2 * Q_BLOCK)
        v = gather_rows(vf_ref, base, 2 * Q_BLOCK)
        valid = band & ((n * nq + qb - 1) * Q_BLOCK + ki >= 0)
        qs = jnp.concatenate([jnp.where(heads[h], q, jnp.zeros_like(q)) for h in range(DIL_HEADS_PER_GROUP)], axis=0)
        s = lax.dot_general(qs, k, (((1,), (1,)), ((), ())), preferred_element_type=F32)
        s = jnp.where(jnp.concatenate([valid] * DIL_HEADS_PER_GROUP, axis=0), s, NEG_INF)
        m = jnp.max(s, axis=1, keepdims=True)
        p = jnp.exp(s - m).astype(BF16)
        pv = jnp.dot(p, v, preferred_element_type=F32)
        den = jnp.dot(p, ones, preferred_element_type=F32)
        inv = 1.0 / den
        lse = m + jnp.log(den)
        o = jnp.zeros((Q_BLOCK, DIL_OUT), F32)
        l = jnp.zeros((Q_BLOCK, DIL_OUT), F32)
        for h in range(DIL_HEADS_PER_GROUP):
            blk = slice(h * Q_BLOCK, (h + 1) * Q_BLOCK)
            o = jnp.where(heads[h], pv[blk] * jnp.concatenate([inv[blk]] * halves, axis=1), o)
            l = jnp.where(heads[h], jnp.concatenate([lse[blk]] * halves, axis=1), l)
        scatter_rows(of_ref, base, o)
        scatter_rows(lf_ref, base, l)
        return carry

    lax.fori_loop(0, r * nq, sub_block, 0, unroll=2)
    for c in range(halves):
        o_ref[:, c * LANES:(c + 1) * LANES] = of_ref[c]
        l_ref[:, c * LANES:(c + 1) * LANES] = lf_ref[c]


def _dilated_group(qa, ka, va, batch, seq, group):
    _, r = DIL_GROUPS[group]
    nq = max(1, DIL_TOKENS // (Q_BLOCK * r))
    prev_rows = Q_BLOCK * r
    cur_rows = prev_rows * nq
    halves = DIL_OUT // LANES
    view = lambda a: a.reshape(batch, seq, DIL_W)
    cur = lambda b, n: (b, n, group)
    prev = lambda b, n: (b, jnp.maximum(n * nq - 1, 0), group)
    cur_spec = pl.BlockSpec((None, cur_rows, DIL_OUT), cur)
    prev_spec = pl.BlockSpec((None, prev_rows, DIL_OUT), prev)
    out_spec = pl.BlockSpec((None, cur_rows, DIL_OUT), lambda b, n: (b, n, 0))
    o, l = pl.pallas_call(
        functools.partial(_dilated_kernel, r=r, nq=nq),
        grid=(batch, seq // cur_rows),
        in_specs=[cur_spec, prev_spec, cur_spec, prev_spec, cur_spec],
        out_specs=[out_spec, out_spec],
        out_shape=[jax.ShapeDtypeStruct((batch, seq, DIL_OUT), F32)] * 2,
        scratch_shapes=[pltpu.VMEM((halves, cur_rows, LANES), F32),
                        pltpu.VMEM((halves, prev_rows + cur_rows, LANES), F32),
                        pltpu.VMEM((halves, prev_rows + cur_rows, LANES), F32),
                        pltpu.VMEM((halves, cur_rows, LANES), F32),
                        pltpu.VMEM((halves, cur_rows, LANES), F32)],
        compiler_params=_cparams(("parallel", "arbitrary")),
        name=f"dilated_g{group}",
    )(view(qa), view(ka), view(ka), view(va), view(va))
    return o.reshape(batch * seq, DIL_OUT), l.reshape(batch * seq, DIL_OUT)


def _stickbreak_kernel(q_ref, k_ref, v_ref, o_ref, acc_ref, csum_ref):
    i = pl.program_id(2)
    tq = SB_TQ
    q = q_ref[...]
    lane = lax.broadcasted_iota(I32, (1, LANES), 1)
    zero = jnp.zeros_like(q)
    qs = jnp.concatenate([jnp.where(lane < HEAD_DIM, q, zero), jnp.where(lane >= HEAD_DIM, q, zero)], axis=0)
    row = lax.broadcasted_iota(I32, (2 * tq, tq), 0)
    causal = lax.broadcasted_iota(I32, (2 * tq, tq), 1) < jnp.where(row >= tq, row - tq, row)
    later = (lax.broadcasted_iota(I32, (tq, tq), 0) > lax.broadcasted_iota(I32, (tq, tq), 1))
    later = jnp.where(later, 1.0, 0.0).astype(BF16)
    acc_ref[...] = jnp.zeros_like(acc_ref)
    csum_ref[...] = jnp.zeros_like(csum_ref)

    def sweep(j, diagonal):
        start = pl.multiple_of(j * tq, tq)
        k = k_ref[pl.ds(start, tq), :]
        v = v_ref[pl.ds(start, tq), :]
        z = lax.dot_general(qs, k, (((1,), (1,)), ((), ())), preferred_element_type=F32)
        sp = jnp.maximum(z, 0.0) + jnp.log(1.0 + jnp.exp(-jnp.abs(z)))
        log_not = jnp.where(causal, -sp, 0.0) if diagonal else -sp
        hi = log_not.astype(BF16)
        lo = (log_not - hi.astype(F32)).astype(BF16)
        inner = (jnp.dot(hi, later, preferred_element_type=F32)
                 + jnp.dot(lo, later, preferred_element_type=F32))
        csum = csum_ref[...]
        w = jnp.exp((z - sp) + inner + csum)
        if diagonal:
            w = jnp.where(causal, w, 0.0)
        acc_ref[...] += jnp.dot(w.astype(BF16), v, preferred_element_type=F32)
        csum = csum + jnp.sum(log_not, axis=1, keepdims=True)
        csum_ref[...] = csum
        return jnp.max(csum)

    def cond(state):
        j, top = state
        return (j >= 0) & (top > SB_LOG_FLOOR)

    def body(state):
        j, _ = state
        return j - 1, sweep(j, False)

    lax.while_loop(cond, body, (i - 1, sweep(i, True)))
    acc = acc_ref[...]
    o_ref[...] = jnp.where(lane < HEAD_DIM, acc[:tq], acc[tq:]).astype(o_ref.dtype)


def _stickbreak(qb, kb, vb, batch, seq):
    view = lambda a: a.reshape(batch, seq, SB_W)
    nq = seq // SB_TQ
    pairs = SB_W // LANES
    kv_spec = pl.BlockSpec((None, seq, LANES), lambda b, hp, i: (b, 0, hp))
    q_spec = pl.BlockSpec((None, SB_TQ, LANES), lambda b, hp, i: (b, i, hp))
    out = pl.pallas_call(
        _stickbreak_kernel,
        grid=(batch, pairs, nq),
        in_specs=[q_spec, kv_spec, kv_spec],
        out_specs=q_spec,
        out_shape=jax.ShapeDtypeStruct((batch, seq, SB_W), BF16),
        scratch_shapes=[pltpu.VMEM((2 * SB_TQ, LANES), F32), pltpu.VMEM((2 * SB_TQ, 1), F32)],
        compiler_params=_cparams(("parallel", "parallel", "arbitrary")),
        name="stickbreak",
    )(view(qb), view(kb), view(vb))
    return out.reshape(batch * seq, SB_W)


def _postmix_kernel(o0_ref, o1_ref, o2_ref, l0_ref, l1_ref, l2_ref, sb_ref, ga_ref, gb_ref, x_ref,
                    mod_ref, gpost_ref, gffn_ref, wa_ref, wb_ref, wo_ref, wr_ref, br_ref,
                    x1_ref, h2_ref, idx_ref, gate_ref, rank_ref, cnt_ref, carry_ref):
    step = pl.program_id(0)
    tm = x_ref.shape[0]

    @pl.when(step == 0)
    def _():
        carry_ref[...] = jnp.zeros_like(carry_ref)

    l0, l1, l2 = l0_ref[...], l1_ref[...], l2_ref[...]
    m = jnp.maximum(jnp.maximum(l0, l1), l2)
    e0, e1, e2 = jnp.exp(l0 - m), jnp.exp(l1 - m), jnp.exp(l2 - m)
    merged = (e0 * o0_ref[...] + e1 * o1_ref[...] + e2 * o2_ref[...]) / (e0 + e1 + e2)
    ya = jnp.dot(merged.astype(BF16), wa_ref[...], preferred_element_type=F32)
    yb = jnp.dot(sb_ref[...], wb_ref[...], preferred_element_type=F32)
    mix = ga_ref[...].astype(F32) * ya + gb_ref[...].astype(F32) * yb
    y = jnp.dot(mix.astype(BF16), wo_ref[...], preferred_element_type=F32)
    gate1 = mod_ref[0, 2:3, :]
    x1 = x_ref[...] + gate1 * _rms(y, gpost_ref[...])
    x1_ref[...] = x1
    h2 = _rms(x1, gffn_ref[...]) * (1.0 + mod_ref[0, 4:5, :]) + mod_ref[0, 3:4, :]
    h2_ref[...] = _pack_bf16_pairs(h2)

    logits = lax.dot_general(wr_ref[...], h2, (((1,), (1,)), ((), ())), preferred_element_type=F32,
                             precision=lax.Precision.HIGHEST) + br_ref[...]
    e_iota = lax.broadcasted_iota(I32, (N_EXPERTS, tm), 0)
    before = (lax.broadcasted_iota(I32, (tm, tm), 0) < lax.broadcasted_iota(I32, (tm, tm), 1))
    before = jnp.where(before, 1.0, 0.0).astype(BF16)

    picks, vals = [], []
    work = logits
    chosen = jnp.zeros((N_EXPERTS, tm), F32)
    for _ in range(TOP_K):
        top = jnp.max(work, axis=0, keepdims=True)
        idx = jnp.min(jnp.where(work == top, e_iota, N_EXPERTS), axis=0, keepdims=True)
        one = e_iota == idx
        picks.append((idx, one))
        vals.append(top)
        chosen = jnp.where(one, 1.0, chosen)
        work = jnp.where(one, -jnp.inf, work)
    exps = [jnp.exp(v - vals[0]) for v in vals]
    den = exps[0] + exps[1] + exps[2] + exps[3]

    rank = jnp.dot(chosen.astype(BF16), before, preferred_element_type=F32) + carry_ref[...]
    carry_ref[...] += jnp.sum(chosen, axis=1, keepdims=True)
    cnt_ref[...] = carry_ref[...].astype(I32)
    for kk, (idx, one) in enumerate(picks):
        idx_ref[kk:kk + 1, :] = idx
        gate_ref[kk:kk + 1, :] = exps[kk] / den
        rank_ref[kk:kk + 1, :] = jnp.sum(jnp.where(one, rank, 0.0), axis=0, keepdims=True).astype(I32)


def _post_mix(dil, sb, ga, gb, x2, mod3, gpost, gffn, wa, wb, wo, w_router, b_router, seq):
    t, d = x2.shape
    tm = ROW_BLOCK
    nb = t // tm
    per_seq = seq // tm
    row = lambda i: (i, 0)
    const = lambda i: (0, 0)
    colblk = lambda i: (0, i)
    (o0, l0), (o1, l1), (o2, l2) = dil
    in_specs = ([pl.BlockSpec((tm, DIL_OUT), row)] * 6
                + [pl.BlockSpec((tm, SB_W), row), pl.BlockSpec((tm, d), row), pl.BlockSpec((tm, d), row),
                   pl.BlockSpec((tm, d), row),
                   pl.BlockSpec((1, 6, d), lambda i: (i // per_seq, 0, 0)),
                   pl.BlockSpec((1, d), const), pl.BlockSpec((1, d), const),
                   pl.BlockSpec(wa.shape, const), pl.BlockSpec(wb.shape, const), pl.BlockSpec(wo.shape, const),
                   pl.BlockSpec((N_EXPERTS, d), const), pl.BlockSpec((N_EXPERTS, 1), const)])
    out_specs = [pl.BlockSpec((tm, d), row), pl.BlockSpec((tm, d // 2), row),
                 pl.BlockSpec((TOP_K, tm), colblk), pl.BlockSpec((TOP_K, tm), colblk),
                 pl.BlockSpec((TOP_K, tm), colblk), pl.BlockSpec((N_EXPERTS, 1), const)]
    out_shape = [jax.ShapeDtypeStruct((t, d), F32), jax.ShapeDtypeStruct((t, d // 2), U32),
                 jax.ShapeDtypeStruct((TOP_K, t), I32), jax.ShapeDtypeStruct((TOP_K, t), F32),
                 jax.ShapeDtypeStruct((TOP_K, t), I32), jax.ShapeDtypeStruct((N_EXPERTS, 1), I32)]
    return pl.pallas_call(
        _postmix_kernel,
        grid=(nb,),
        in_specs=in_specs,
        out_specs=out_specs,
        out_shape=out_shape,
        scratch_shapes=[pltpu.VMEM((N_EXPERTS, 1), F32)],
        compiler_params=_cparams(("arbitrary",)),
        name="post_mix",
    )(o0, o1, o2, l0, l1, l2, sb, ga, gb, x2, mod3, gpost.reshape(1, d), gffn.reshape(1, d),
      wa, wb, wo, w_router.T, b_router.reshape(N_EXPERTS, 1))


def _slots_kernel(start_ref, idx_ref, rank_ref, slot_ref):
    idx = idx_ref[...]
    base = jnp.zeros_like(idx)
    for e in range(N_EXPERTS):
        base = jnp.where(idx == e, start_ref[e], base)
    slot_ref[...] = base + rank_ref[...]


def _slots(pad_start, idx, rank):
    k, t = idx.shape
    tb = min(t, 8192)
    blk = pl.BlockSpec((k, tb), lambda i, s: (0, i))
    return pl.pallas_call(
        _slots_kernel,
        grid_spec=pltpu.PrefetchScalarGridSpec(
            num_scalar_prefetch=1, grid=(t // tb,), in_specs=[blk, blk], out_specs=blk),
        out_shape=jax.ShapeDtypeStruct((k, t), I32),
        compiler_params=_cparams(("parallel",)),
        name="slots",
    )(pad_start, idx, rank)


def _sc_mesh():
    return plsc.VectorSubcoreMesh(core_axis_name="c", subcore_axis_name="s")


def _sc_dispatch(rows, slot, n_slots):
    t, d = rows.shape
    chunk = SC_DISPATCH_CHUNK
    per_worker = t // SC_WORKERS
    n_chunks = per_worker // chunk

    @functools.partial(
        pl.kernel, mesh=_sc_mesh(),
        out_type=jax.ShapeDtypeStruct((n_slots, d), rows.dtype),
        scratch_types=[pltpu.VMEM((TOP_K, chunk), I32), pltpu.VMEM((chunk, d), rows.dtype),
                       pltpu.SemaphoreType.DMA],
        name="dispatch",
    )
    def run(rows_hbm, slot_hbm, out_hbm, idx_v, rows_v, sem):
        wid = lax.axis_index("s") * SC_CORES + lax.axis_index("c")

        @pl.loop(0, n_chunks)
        def _(ci):
            base = pl.multiple_of(wid * per_worker + ci * chunk, chunk)
            loads = [pltpu.make_async_copy(rows_hbm.at[pl.ds(base, chunk)], rows_v, sem)]
            loads += [pltpu.make_async_copy(slot_hbm.at[kk, pl.ds(base, chunk)], idx_v.at[kk], sem)
                      for kk in range(TOP_K)]
            for cp in loads:
                cp.start()
            for cp in loads:
                cp.wait()
            copies = [pltpu.make_async_copy(rows_v, out_hbm.at[idx_v.at[kk]], sem) for kk in range(TOP_K)]
            for cp in copies:
                cp.start()
            for cp in copies:
                cp.wait()

    return run(rows, slot)


def _sc_combine(ys, slot):
    _, d = ys.shape
    k, t = slot.shape
    chunk = SC_COMBINE_CHUNK
    per_worker = t // SC_WORKERS
    n_chunks = per_worker // chunk

    @functools.partial(
        pl.kernel, mesh=_sc_mesh(),
        out_type=jax.ShapeDtypeStruct((k, t, d), ys.dtype),
        scratch_types=[pltpu.VMEM((TOP_K, chunk), I32), pltpu.VMEM((TOP_K, chunk, d), ys.dtype),
                       pltpu.SemaphoreType.DMA],
        name="combine",
    )
    def run(ys_hbm, slot_hbm, out_hbm, idx_v, rows_v, sem):
        wid = lax.axis_index("s") * SC_CORES + lax.axis_index("c")

        @pl.loop(0, n_chunks)
        def _(ci):
            base = pl.multiple_of(wid * per_worker + ci * chunk, chunk)
            loads = [pltpu.make_async_copy(slot_hbm.at[kk, pl.ds(base, chunk)], idx_v.at[kk], sem)
                     for kk in range(TOP_K)]
            for cp in loads:
                cp.start()
            for cp in loads:
                cp.wait()
            gathers = [pltpu.make_async_copy(ys_hbm.at[idx_v.at[kk]], rows_v.at[kk], sem) for kk in range(TOP_K)]
            for cp in gathers:
                cp.start()
            for cp in gathers:
                cp.wait()
            stores = [pltpu.make_async_copy(rows_v.at[kk], out_hbm.at[kk, pl.ds(base, chunk)], sem)
                      for kk in range(TOP_K)]
            for cp in stores:
                cp.start()
            for cp in stores:
                cp.wait()

    return run(ys, slot)


def _experts_kernel(be_ref, nused_ref, x_ref, wu_ref, bg_ref, bl_ref, wd_ref, bd_ref, y_ref,
                    glu_w_ref, lin_w_ref, down_w_ref):
    i = pl.program_id(0)
    live = i < nused_ref[0]
    new_expert = (i == 0) | (be_ref[i] != be_ref[jnp.maximum(i - 1, 0)])

    @pl.when(live & new_expert)
    def _():
        tile = 2 * LANES
        src = lax.broadcasted_iota(I32, (tile, tile), 0)
        dst = lax.broadcasted_iota(I32, (tile, tile), 1)
        wanted = jnp.where(dst < LANES, 2 * dst, 2 * (dst - LANES) + 1)
        pick = jnp.where(src == wanted, 1.0, 0.0).astype(BF16)
        for c in range(wu_ref.shape[2] // tile):
            w = wu_ref[0, :, c * tile:(c + 1) * tile].astype(BF16)
            both = jnp.dot(w, pick, preferred_element_type=F32)
            glu_w_ref[:, c * LANES:(c + 1) * LANES] = both[:, :LANES].astype(BF16)
            lin_w_ref[:, c * LANES:(c + 1) * LANES] = both[:, LANES:].astype(BF16)
        down_w_ref[...] = wd_ref[0].astype(BF16)

    @pl.when(live)
    def _():
        lo, hi = _unpack_bf16_pairs(x_ref[...])
        x = jnp.concatenate([lo, hi], axis=1).astype(BF16)
        glu = jnp.dot(x, glu_w_ref[...], preferred_element_type=F32) + bg_ref[0]
        lin = jnp.dot(x, lin_w_ref[...], preferred_element_type=F32) + bl_ref[0]
        glu = jnp.minimum(glu, SWIGLU_LIMIT)
        lin = jnp.clip(lin, -SWIGLU_LIMIT, SWIGLU_LIMIT)
        act = glu * jax.nn.sigmoid(SWIGLU_ALPHA * glu) * (lin + 1.0)
        y = jnp.dot(act.astype(BF16), down_w_ref[...], preferred_element_type=F32) + bd_ref[0]
        y_ref[...] = _pack_bf16_pairs(y)


def _experts(xs, block_expert, n_used, w_up, b_glu, b_lin, w_down, b_down):
    n_slots, half_d = xs.shape
    _, d, f2 = w_up.shape
    f = f2 // 2
    nblk = n_slots // MOE_ROWS
    rows = lambda i, be, nu: (i, 0)
    by_e = lambda i, be, nu: (be[i], 0, 0)
    return pl.pallas_call(
        _experts_kernel,
        grid_spec=pltpu.PrefetchScalarGridSpec(
            num_scalar_prefetch=2, grid=(nblk,),
            in_specs=[pl.BlockSpec((MOE_ROWS, half_d), rows),
                      pl.BlockSpec((1, d, f2), by_e),
                      pl.BlockSpec((1, 1, f), by_e), pl.BlockSpec((1, 1, f), by_e),
                      pl.BlockSpec((1, f, d), by_e), pl.BlockSpec((1, 1, d), by_e)],
            out_specs=pl.BlockSpec((MOE_ROWS, half_d), rows),
            scratch_shapes=[pltpu.VMEM((d, f), BF16), pltpu.VMEM((d, f), BF16), pltpu.VMEM((f, d), BF16)]),
        out_shape=jax.ShapeDtypeStruct((n_slots, half_d), U32),
        compiler_params=_cparams(("arbitrary",)),
        name="experts",
    )(block_expert, n_used, xs, w_up, b_glu, b_lin, w_down, b_down)


def _final_kernel(g_ref, gate_ref, x1_ref, mod_ref, gain_ref, o_ref):
    tm = x1_ref.shape[0]
    gates = gate_ref[...]
    padded = jnp.concatenate([gates, jnp.zeros((LANES - TOP_K, tm), F32)], axis=0)
    gate_cols = padded.T
    y_lo = jnp.zeros(g_ref.shape[1:], F32)
    y_hi = jnp.zeros(g_ref.shape[1:], F32)
    for kk in range(TOP_K):
        lo, hi = _unpack_bf16_pairs(g_ref[kk])
        y_lo = y_lo + lo * gate_cols[:, kk:kk + 1]
        y_hi = y_hi + hi * gate_cols[:, kk:kk + 1]
    y = jnp.concatenate([y_lo, y_hi], axis=1)
    o_ref[...] = x1_ref[...] + mod_ref[0, 5:6, :] * _rms(y, gain_ref[...])


def _final(g, gates, x1, mod3, gain, seq):
    t, d = x1.shape
    tm = ROW_BLOCK
    per_seq = seq // tm
    row = lambda i: (i, 0)
    return pl.pallas_call(
        _final_kernel,
        grid=(t // tm,),
        in_specs=[pl.BlockSpec((TOP_K, tm, d // 2), lambda i: (0, i, 0)),
                  pl.BlockSpec((TOP_K, tm), lambda i: (0, i)),
                  pl.BlockSpec((tm, d), row),
                  pl.BlockSpec((1, 6, d), lambda i: (i // per_seq, 0, 0)),
                  pl.BlockSpec((1, d), lambda i: (0, 0))],
        out_specs=pl.BlockSpec((tm, d), row),
        out_shape=jax.ShapeDtypeStruct((t, d), F32),
        compiler_params=_cparams(("parallel",)),
        name="final",
    )(g, gates, x1, mod3, gain.reshape(1, d))


def _layer(x2, c, positions, seq, ada_w, ada_b, norm_mix_pre, norm_mix_post, norm_ffn_pre, norm_ffn_post,
           w_in, w_branch_a, w_branch_b, w_out, w_router, b_router, w_up, b_up, w_down, b_down):
    t, d = x2.shape
    batch = t // seq
    mod3 = _adaln(c, ada_w, ada_b).reshape(batch, 6, d)

    qa, ka, va, qb, kb, vb, ga, gb = _in_proj(x2, mod3, norm_mix_pre, positions, w_in.astype(BF16), seq)
    dil = [_dilated_group(qa, ka, va, batch, seq, g) for g in range(len(DIL_GROUPS))]
    sb = _stickbreak(qb, kb, vb, batch, seq)
    x1, h2, idx, gates, rank, counts = _post_mix(
        dil, sb, ga, gb, x2, mod3, norm_mix_post, norm_ffn_pre,
        w_branch_a.astype(BF16), w_branch_b.astype(BF16), w_out.astype(BF16), w_router, b_router, seq)

    counts = counts.reshape(N_EXPERTS)
    padded = (counts + MOE_ROWS - 1) // MOE_ROWS * MOE_ROWS
    pad_end = jnp.cumsum(padded)
    pad_start = (pad_end - padded).astype(I32)
    nblk = (t * TOP_K) // MOE_ROWS + N_EXPERTS
    block_first_row = jnp.arange(nblk, dtype=I32)[:, None] * MOE_ROWS
    block_expert = jnp.minimum(jnp.sum(pad_end[None, :] <= block_first_row, axis=1), N_EXPERTS - 1).astype(I32)
    n_used = (pad_end[-1:] // MOE_ROWS).astype(I32)

    slot = _slots(pad_start, idx, rank)
    xs = _sc_dispatch(h2, slot, nblk * MOE_ROWS)
    f = w_up.shape[2] // 2
    ys = _experts(xs, block_expert, n_used, w_up,
                  b_up[:, 0::2].reshape(N_EXPERTS, 1, f), b_up[:, 1::2].reshape(N_EXPERTS, 1, f),
                  w_down, b_down.reshape(N_EXPERTS, 1, d))
    g = _sc_combine(ys, slot)
    return _final(g, gates, x1, mod3, norm_ffn_post, seq)


def kernel(x, c, positions, ada_w, ada_b, norm_mix_pre, norm_mix_post, norm_ffn_pre, norm_ffn_post,
           w_in, w_branch_a, w_branch_b, w_out, w_router, b_router, w_up, b_up, w_down, b_down):
    batch, seq, d = x.shape
    x2 = x.reshape(batch * seq, d)
    for layer in range(ada_w.shape[0]):
        x2 = _layer(x2, c, positions, seq, ada_w[layer], ada_b[layer], norm_mix_pre[layer], norm_mix_post[layer],
                    norm_ffn_pre[layer], norm_ffn_post[layer], w_in[layer], w_branch_a[layer], w_branch_b[layer],
                    w_out[layer], w_router[layer], b_router[layer], w_up[layer], b_up[layer], w_down[layer],
                    b_down[layer])
    return x2.reshape(batch, seq, d)
```

```python
import functools

import numpy as np
import jax
import jax.numpy as jnp
from jax import lax
from jax.experimental import pallas as pl
from jax.experimental.pallas import tpu as pltpu
from jax.experimental.pallas import tpu_sc as plsc

F32 = jnp.float32
BF16 = jnp.bfloat16
I32 = jnp.int32
U32 = jnp.uint32

HEAD_DIM = 64
DIL_GROUPS = ((128, 1), (512, 4), (2048, 16))
DIL_HEADS_PER_GROUP = 4
DIL_HEADS = DIL_HEADS_PER_GROUP * len(DIL_GROUPS)
DIL_W = DIL_HEADS * HEAD_DIM
DIL_OUT = DIL_HEADS_PER_GROUP * HEAD_DIM
SB_HEADS = 8
SB_W = SB_HEADS * HEAD_DIM
ROPE_THETA = 500000.0
ROPE_DIMS = HEAD_DIM // 4
Q_BLOCK = 128
N_EXPERTS = 32
TOP_K = 4
SWIGLU_ALPHA = 1.702
SWIGLU_LIMIT = 7.0
NORM_EPS = 1e-6
NEG_INF = -1e30

LANES = 128
ROW_BLOCK = 512
MOE_ROWS = 512
MOE_PARTS = 2
DIL_TOKENS = 512
SB_TQ = 256
SB_LOG_FLOOR = -105.0
VMEM_LIMIT = 56 * 1024 * 1024

SC_CORES = 2
SC_SUBCORES = 16
SC_WORKERS = SC_CORES * SC_SUBCORES
SC_DISPATCH_CHUNK = 64
SC_COMBINE_CHUNK = 32


def _cparams(sem):
    return pltpu.CompilerParams(dimension_semantics=sem, vmem_limit_bytes=VMEM_LIMIT)


def _rms(x, gain):
    ms = jnp.mean(x * x, axis=-1, keepdims=True)
    return x * lax.rsqrt(ms + NORM_EPS) * gain


def _pack_bf16_pairs(x):
    n = x.shape[1] // 2
    u = lax.bitcast_convert_type(x, U32)
    r = (u + jnp.uint32(0x7FFF) + ((u >> 16) & jnp.uint32(1))) >> 16
    return r[:, :n] | (r[:, n:] << 16)


def _unpack_bf16_pairs(w):
    lo = lax.bitcast_convert_type(w << 16, F32)
    hi = lax.bitcast_convert_type(w & jnp.uint32(0xFFFF0000), F32)
    return lo, hi


def _adaln_kernel(c_ref, w_ref, b_ref, o_ref):
    c = c_ref[...]
    s = c * jax.nn.sigmoid(c)
    o_ref[...] = jnp.dot(s, w_ref[...], preferred_element_type=F32,
                         precision=lax.Precision.HIGHEST) + b_ref[...]


def _adaln(c, ada_w, ada_b):
    b, d = c.shape
    n = ada_w.shape[1]
    return pl.pallas_call(
        _adaln_kernel,
        grid=(n // d,),
        in_specs=[pl.BlockSpec((b, d), lambda j: (0, 0)),
                  pl.BlockSpec((d, d), lambda j: (0, j)),
                  pl.BlockSpec((1, d), lambda j: (0, j))],
        out_specs=pl.BlockSpec((b, d), lambda j: (0, j)),
        out_shape=jax.ShapeDtypeStruct((b, n), F32),
        compiler_params=_cparams(("arbitrary",)),
        name="adaln",
    )(c, ada_w, ada_b.reshape(1, n))


def _rope_table():
    half = ROPE_DIMS // 2
    inv_freq = ROPE_THETA ** (-(np.arange(half, dtype=np.float32) * 2.0 / ROPE_DIMS))
    d = np.arange(LANES) % HEAD_DIM
    tab = np.zeros((8, LANES), np.float32)
    tab[0] = np.where(d < ROPE_DIMS, inv_freq[d % half], 0.0)
    tab[1] = np.where(d < half, -1.0, 1.0)
    tab[2] = np.where(d < half, 1.0, 0.0)
    return jnp.asarray(tab)


def _inproj_kernel(x_ref, mod_ref, g_ref, pos_ref, rope_ref, w_ref,
                   qa_ref, ka_ref, va_ref, qb_ref, kb_ref, vb_ref, ga_ref, gb_ref):
    x = x_ref[...]
    tm = x.shape[0]
    shift = mod_ref[0, 0:1, :]
    scale = mod_ref[0, 1:2, :]
    hb = (_rms(x, g_ref[...]) * (1.0 + scale) + shift).astype(BF16)

    pos = pos_ref[0].astype(F32)
    pos_rows = jnp.broadcast_to(pos, (LANES, tm)).T
    ang = pos_rows * rope_ref[0:1, :]
    cos = jnp.cos(ang)
    sin = jnp.sin(ang) * rope_ref[1:2, :]
    first = rope_ref[2:3, :] > 0.5
    half = ROPE_DIMS // 2

    def rope(t):
        partner = jnp.where(first, pltpu.roll(t, LANES - half, 1), pltpu.roll(t, half, 1))
        return t * cos + partner * sin

    qk_scale = HEAD_DIM ** -0.5
    col = 0
    plan = ((qa_ref, DIL_W, "rope_q"), (ka_ref, DIL_W, "rope"), (va_ref, DIL_W, "id"),
            (qb_ref, SB_W, "scale"), (kb_ref, SB_W, "id"), (vb_ref, SB_W, "id"),
            (ga_ref, x.shape[1], "sig"), (gb_ref, x.shape[1], "sig"))
    for ref, width, mode in plan:
        for c0 in range(0, width, 2 * LANES):
            t = jnp.dot(hb, w_ref[:, col + c0:col + c0 + 2 * LANES], preferred_element_type=F32)
            if mode in ("rope", "rope_q"):
                parts = [rope(t[:, :LANES]), rope(t[:, LANES:])]
                t = jnp.concatenate(parts, axis=1)
                if mode == "rope_q":
                    t = t * qk_scale
            elif mode == "scale":
                t = t * qk_scale
            elif mode == "sig":
                t = jax.nn.sigmoid(t)
            ref[:, c0:c0 + 2 * LANES] = t.astype(ref.dtype)
        col += width


def _in_proj(x2, mod3, gain, positions, w_in_bf16, seq):
    t, d = x2.shape
    tm = ROW_BLOCK
    nb = t // tm
    per_seq = seq // tm
    pos3 = positions.reshape(nb, 1, tm)
    widths = (DIL_W, DIL_W, DIL_W, SB_W, SB_W, SB_W, d, d)
    row = lambda i: (i, 0)
    return pl.pallas_call(
        _inproj_kernel,
        grid=(nb,),
        in_specs=[pl.BlockSpec((tm, d), row),
                  pl.BlockSpec((1, 6, d), lambda i: (i // per_seq, 0, 0)),
                  pl.BlockSpec((1, d), lambda i: (0, 0)),
                  pl.BlockSpec((1, 1, tm), lambda i: (i, 0, 0)),
                  pl.BlockSpec((8, LANES), lambda i: (0, 0)),
                  pl.BlockSpec(w_in_bf16.shape, lambda i: (0, 0))],
        out_specs=[pl.BlockSpec((tm, w), row) for w in widths],
        out_shape=[jax.ShapeDtypeStruct((t, w), BF16) for w in widths],
        compiler_params=_cparams(("parallel",)),
        name="in_proj",
    )(x2, mod3, gain.reshape(1, d), pos3, _rope_table(), w_in_bf16)


def _dilated_kernel(q_ref, kp_ref, kc_ref, vp_ref, vc_ref, o_ref, l_ref,
                    qf_ref, kf_ref, vf_ref, of_ref, lf_ref, *, r, nq):
    n = pl.program_id(1)
    prev_rows = Q_BLOCK * r
    halves = DIL_OUT // LANES
    for c in range(halves):
        cols = slice(c * LANES, (c + 1) * LANES)
        qf_ref[c] = q_ref[:, cols].astype(F32)
        kf_ref[c, 0:prev_rows, :] = kp_ref[:, cols].astype(F32)
        kf_ref[c, prev_rows:, :] = kc_ref[:, cols].astype(F32)
        vf_ref[c, 0:prev_rows, :] = vp_ref[:, cols].astype(F32)
        vf_ref[c, prev_rows:, :] = vc_ref[:, cols].astype(F32)

    def gather_rows(ref, start, size):
        idx = pl.ds(start, size, stride=r) if r > 1 else pl.ds(start, size)
        return jnp.concatenate([ref[c, idx, :] for c in range(halves)], axis=1).astype(BF16)

    def scatter_rows(ref, start, val):
        idx = pl.ds(start, Q_BLOCK, stride=r) if r > 1 else pl.ds(start, Q_BLOCK)
        for c in range(halves):
            ref[c, idx, :] = val[:, c * LANES:(c + 1) * LANES]

    qi = lax.broadcasted_iota(I32, (Q_BLOCK, 2 * Q_BLOCK), 0) + Q_BLOCK
    ki = lax.broadcasted_iota(I32, (Q_BLOCK, 2 * Q_BLOCK), 1)
    dist = qi - ki
    band = (dist >= 0) & (dist <= Q_BLOCK)
    lane = lax.broadcasted_iota(I32, (1, DIL_OUT), 1)

    heads = [(lane >= h * HEAD_DIM) & (lane < (h + 1) * HEAD_DIM) for h in range(DIL_HEADS_PER_GROUP)]
    ones = jnp.ones((2 * Q_BLOCK, LANES), BF16)

    def sub_block(idx, carry):
        rho = idx % r
        qb = idx // r
        base = r * Q_BLOCK * qb + rho
        q = gather_rows(qf_ref, base, Q_BLOCK)
        k = gather_rows(kf_ref, base, 2 * Q_BLOCK)
        v = gather_rows(vf_ref, base, 2 * Q_BLOCK)
        valid = band & ((n * nq + qb - 1) * Q_BLOCK + ki >= 0)
        qs = jnp.concatenate([jnp.where(heads[h], q, jnp.zeros_like(q)) for h in range(DIL_HEADS_PER_GROUP)], axis=0)
        s = lax.dot_general(qs, k, (((1,), (1,)), ((), ())), preferred_element_type=F32)
        s = jnp.where(jnp.concatenate([valid] * DIL_HEADS_PER_GROUP, axis=0), s, NEG_INF)
        m = jnp.max(s, axis=1, keepdims=True)
        p = jnp.exp(s - m).astype(BF16)
        pv = jnp.dot(p, v, preferred_element_type=F32)
        den = jnp.dot(p, ones, preferred_element_type=F32)
        inv = 1.0 / den
        lse = m + jnp.log(den)
        o = jnp.zeros((Q_BLOCK, DIL_OUT), F32)
        l = jnp.zeros((Q_BLOCK, DIL_OUT), F32)
        for h in range(DIL_HEADS_PER_GROUP):
            blk = slice(h * Q_BLOCK, (h + 1) * Q_BLOCK)
            o = jnp.where(heads[h], pv[blk] * jnp.concatenate([inv[blk]] * halves, axis=1), o)
            l = jnp.where(heads[h], jnp.concatenate([lse[blk]] * halves, axis=1), l)
        scatter_rows(of_ref, base, o)
        scatter_rows(lf_ref, base, l)
        return carry

    lax.fori_loop(0, r * nq, sub_block, 0, unroll=2)
    for c in range(halves):
        o_ref[:, c * LANES:(c + 1) * LANES] = of_ref[c]
        l_ref[:, c * LANES:(c + 1) * LANES] = lf_ref[c]


def _dilated_group(qa, ka, va, batch, seq, group):
    _, r = DIL_GROUPS[group]
    nq = max(1, DIL_TOKENS // (Q_BLOCK * r))
    prev_rows = Q_BLOCK * r
    cur_rows = prev_rows * nq
    halves = DIL_OUT // LANES
    view = lambda a: a.reshape(batch, seq, DIL_W)
    cur = lambda b, n: (b, n, group)
    prev = lambda b, n: (b, jnp.maximum(n * nq - 1, 0), group)
    cur_spec = pl.BlockSpec((None, cur_rows, DIL_OUT), cur)
    prev_spec = pl.BlockSpec((None, prev_rows, DIL_OUT), prev)
    out_spec = pl.BlockSpec((None, cur_rows, DIL_OUT), lambda b, n: (b, n, 0))
    o, l = pl.pallas_call(
        functools.partial(_dilated_kernel, r=r, nq=nq),
        grid=(batch, seq // cur_rows),
        in_specs=[cur_spec, prev_spec, cur_spec, prev_spec, cur_spec],
        out_specs=[out_spec, out_spec],
        out_shape=[jax.ShapeDtypeStruct((batch, seq, DIL_OUT), F32)] * 2,
        scratch_shapes=[pltpu.VMEM((halves, cur_rows, LANES), F32),
                        pltpu.VMEM((halves, prev_rows + cur_rows, LANES), F32),
                        pltpu.VMEM((halves, prev_rows + cur_rows, LANES), F32),
                        pltpu.VMEM((halves, cur_rows, LANES), F32),
                        pltpu.VMEM((halves, cur_rows, LANES), F32)],
        compiler_params=_cparams(("parallel", "arbitrary")),
        name=f"dilated_g{group}",
    )(view(qa), view(ka), view(ka), view(va), view(va))
    return o.reshape(batch * seq, DIL_OUT), l.reshape(batch * seq, DIL_OUT)


def _stickbreak_kernel(q_ref, k_ref, v_ref, o_ref, acc_ref, csum_ref):
    i = pl.program_id(2)
    tq = SB_TQ
    q = q_ref[...]
    lane = lax.broadcasted_iota(I32, (1, LANES), 1)
    zero = jnp.zeros_like(q)
    qs = jnp.concatenate([jnp.where(lane < HEAD_DIM, q, zero), jnp.where(lane >= HEAD_DIM, q, zero)], axis=0)
    row = lax.broadcasted_iota(I32, (2 * tq, tq), 0)
    causal = lax.broadcasted_iota(I32, (2 * tq, tq), 1) < jnp.where(row >= tq, row - tq, row)
    later = (lax.broadcasted_iota(I32, (tq, tq), 0) > lax.broadcasted_iota(I32, (tq, tq), 1))
    later = jnp.where(later, 1.0, 0.0).astype(BF16)
    acc_ref[...] = jnp.zeros_like(acc_ref)
    csum_ref[...] = jnp.zeros_like(csum_ref)

    def sweep(j, diagonal):
        start = pl.multiple_of(j * tq, tq)
        k = k_ref[pl.ds(start, tq), :]
        v = v_ref[pl.ds(start, tq), :]
        z = lax.dot_general(qs, k, (((1,), (1,)), ((), ())), preferred_element_type=F32)
        sp = jnp.maximum(z, 0.0) + jnp.log(1.0 + jnp.exp(-jnp.abs(z)))
        log_not = jnp.where(causal, -sp, 0.0) if diagonal else -sp
        hi = log_not.astype(BF16)
        lo = (log_not - hi.astype(F32)).astype(BF16)
        inner = (jnp.dot(hi, later, preferred_element_type=F32)
                 + jnp.dot(lo, later, preferred_element_type=F32))
        csum = csum_ref[...]
        w = jnp.exp((z - sp) + inner + csum)
        if diagonal:
            w = jnp.where(causal, w, 0.0)
        acc_ref[...] += jnp.dot(w.astype(BF16), v, preferred_element_type=F32)
        csum = csum + jnp.sum(log_not, axis=1, keepdims=True)
        csum_ref[...] = csum
        return jnp.max(csum)

    def cond(state):
        j, top = state
        return (j >= 0) & (top > SB_LOG_FLOOR)

    def body(state):
        j, _ = state
        return j - 1, sweep(j, False)

    lax.while_loop(cond, body, (i - 1, sweep(i, True)))
    acc = acc_ref[...]
    o_ref[...] = jnp.where(lane < HEAD_DIM, acc[:tq], acc[tq:]).astype(o_ref.dtype)


def _stickbreak(qb, kb, vb, batch, seq):
    view = lambda a: a.reshape(batch, seq, SB_W)
    nq = seq // SB_TQ
    pairs = SB_W // LANES
    kv_spec = pl.BlockSpec((None, seq, LANES), lambda b, hp, i: (b, 0, hp))
    q_spec = pl.BlockSpec((None, SB_TQ, LANES), lambda b, hp, i: (b, i, hp))
    out = pl.pallas_call(
        _stickbreak_kernel,
        grid=(batch, pairs, nq),
        in_specs=[q_spec, kv_spec, kv_spec],
        out_specs=q_spec,
        out_shape=jax.ShapeDtypeStruct((batch, seq, SB_W), BF16),
        scratch_shapes=[pltpu.VMEM((2 * SB_TQ, LANES), F32), pltpu.VMEM((2 * SB_TQ, 1), F32)],
        compiler_params=_cparams(("parallel", "parallel", "arbitrary")),
        name="stickbreak",
    )(view(qb), view(kb), view(vb))
    return out.reshape(batch * seq, SB_W)


def _postmix_kernel(o0_ref, o1_ref, o2_ref, l0_ref, l1_ref, l2_ref, sb_ref, ga_ref, gb_ref, x_ref,
                    mod_ref, gpost_ref, gffn_ref, wa_ref, wb_ref, wo_ref, wr_ref, br_ref,
                    x1_ref, h2_ref, idx_ref, gate_ref, rank_ref, cnt_ref, carry_ref):
    step = pl.program_id(0)
    tm = x_ref.shape[0]

    @pl.when(step == 0)
    def _():
        carry_ref[...] = jnp.zeros_like(carry_ref)

    l0, l1, l2 = l0_ref[...], l1_ref[...], l2_ref[...]
    m = jnp.maximum(jnp.maximum(l0, l1), l2)
    e0, e1, e2 = jnp.exp(l0 - m), jnp.exp(l1 - m), jnp.exp(l2 - m)
    merged = (e0 * o0_ref[...] + e1 * o1_ref[...] + e2 * o2_ref[...]) / (e0 + e1 + e2)
    ya = jnp.dot(merged.astype(BF16), wa_ref[...], preferred_element_type=F32)
    yb = jnp.dot(sb_ref[...], wb_ref[...], preferred_element_type=F32)
    mix = ga_ref[...].astype(F32) * ya + gb_ref[...].astype(F32) * yb
    y = jnp.dot(mix.astype(BF16), wo_ref[...], preferred_element_type=F32)
    gate1 = mod_ref[0, 2:3, :]
    x1 = x_ref[...] + gate1 * _rms(y, gpost_ref[...])
    x1_ref[...] = x1
    h2 = _rms(x1, gffn_ref[...]) * (1.0 + mod_ref[0, 4:5, :]) + mod_ref[0, 3:4, :]
    h2_ref[...] = _pack_bf16_pairs(h2)

    logits = lax.dot_general(wr_ref[...], h2, (((1,), (1,)), ((), ())), preferred_element_type=F32,
                             precision=lax.Precision.HIGHEST) + br_ref[...]
    e_iota = lax.broadcasted_iota(I32, (N_EXPERTS, tm), 0)
    before = (lax.broadcasted_iota(I32, (tm, tm), 0) < lax.broadcasted_iota(I32, (tm, tm), 1))
    before = jnp.where(before, 1.0, 0.0).astype(BF16)

    picks, vals = [], []
    work = logits
    chosen = jnp.zeros((N_EXPERTS, tm), F32)
    for _ in range(TOP_K):
        top = jnp.max(work, axis=0, keepdims=True)
        idx = jnp.min(jnp.where(work == top, e_iota, N_EXPERTS), axis=0, keepdims=True)
        one = e_iota == idx
        picks.append((idx, one))
        vals.append(top)
        chosen = jnp.where(one, 1.0, chosen)
        work = jnp.where(one, -jnp.inf, work)
    exps = [jnp.exp(v - vals[0]) for v in vals]
    den = exps[0] + exps[1] + exps[2] + exps[3]

    rank = jnp.dot(chosen.astype(BF16), before, preferred_element_type=F32) + carry_ref[...]
    carry_ref[...] += jnp.sum(chosen, axis=1, keepdims=True)
    cnt_ref[...] = carry_ref[...].astype(I32)
    for kk, (idx, one) in enumerate(picks):
        idx_ref[kk:kk + 1, :] = idx
        gate_ref[kk:kk + 1, :] = exps[kk] / den
        rank_ref[kk:kk + 1, :] = jnp.sum(jnp.where(one, rank, 0.0), axis=0, keepdims=True).astype(I32)


def _post_mix(dil, sb, ga, gb, x2, mod3, gpost, gffn, wa, wb, wo, w_router, b_router, seq, part):
    t_all, d = x2.shape
    t = t_all // MOE_PARTS
    tm = ROW_BLOCK
    nb = t // tm
    off = part * nb
    per_seq = seq // tm
    row_in = lambda i: (i + off, 0)
    row = lambda i: (i, 0)
    const = lambda i: (0, 0)
    colblk = lambda i: (0, i)
    (o0, l0), (o1, l1), (o2, l2) = dil
    in_specs = ([pl.BlockSpec((tm, DIL_OUT), row_in)] * 6
                + [pl.BlockSpec((tm, SB_W), row_in), pl.BlockSpec((tm, d), row_in), pl.BlockSpec((tm, d), row_in),
                   pl.BlockSpec((tm, d), row_in),
                   pl.BlockSpec((1, 6, d), lambda i: ((i + off) // per_seq, 0, 0)),
                   pl.BlockSpec((1, d), const), pl.BlockSpec((1, d), const),
                   pl.BlockSpec(wa.shape, const), pl.BlockSpec(wb.shape, const), pl.BlockSpec(wo.shape, const),
                   pl.BlockSpec((N_EXPERTS, d), const), pl.BlockSpec((N_EXPERTS, 1), const)])
    out_specs = [pl.BlockSpec((tm, d), row), pl.BlockSpec((tm, d // 2), row),
                 pl.BlockSpec((TOP_K, tm), colblk), pl.BlockSpec((TOP_K, tm), colblk),
                 pl.BlockSpec((TOP_K, tm), colblk), pl.BlockSpec((N_EXPERTS, 1), const)]
    out_shape = [jax.ShapeDtypeStruct((t, d), F32), jax.ShapeDtypeStruct((t, d // 2), U32),
                 jax.ShapeDtypeStruct((TOP_K, t), I32), jax.ShapeDtypeStruct((TOP_K, t), F32),
                 jax.ShapeDtypeStruct((TOP_K, t), I32), jax.ShapeDtypeStruct((N_EXPERTS, 1), I32)]
    return pl.pallas_call(
        _postmix_kernel,
        grid=(nb,),
        in_specs=in_specs,
        out_specs=out_specs,
        out_shape=out_shape,
        scratch_shapes=[pltpu.VMEM((N_EXPERTS, 1), F32)],
        compiler_params=_cparams(("arbitrary",)),
        name="post_mix",
    )(o0, o1, o2, l0, l1, l2, sb, ga, gb, x2, mod3, gpost.reshape(1, d), gffn.reshape(1, d),
      wa, wb, wo, w_router.T, b_router.reshape(N_EXPERTS, 1))


def _slots_kernel(start_ref, idx_ref, rank_ref, slot_ref):
    idx = idx_ref[...]
    base = jnp.zeros_like(idx)
    for e in range(N_EXPERTS):
        base = jnp.where(idx == e, start_ref[e], base)
    slot_ref[...] = base + rank_ref[...]


def _slots(pad_start, idx, rank):
    k, t = idx.shape
    tb = min(t, 8192)
    blk = pl.BlockSpec((k, tb), lambda i, s: (0, i))
    return pl.pallas_call(
        _slots_kernel,
        grid_spec=pltpu.PrefetchScalarGridSpec(
            num_scalar_prefetch=1, grid=(t // tb,), in_specs=[blk, blk], out_specs=blk),
        out_shape=jax.ShapeDtypeStruct((k, t), I32),
        compiler_params=_cparams(("parallel",)),
        name="slots",
    )(pad_start, idx, rank)


def _sc_mesh():
    return plsc.VectorSubcoreMesh(core_axis_name="c", subcore_axis_name="s")


def _sc_dispatch(rows, slot, n_slots):
    t, d = rows.shape
    chunk = SC_DISPATCH_CHUNK
    per_worker = t // SC_WORKERS
    n_chunks = per_worker // chunk

    @functools.partial(
        pl.kernel, mesh=_sc_mesh(),
        out_type=jax.ShapeDtypeStruct((n_slots, d), rows.dtype),
        scratch_types=[pltpu.VMEM((TOP_K, chunk), I32), pltpu.VMEM((chunk, d), rows.dtype),
                       pltpu.SemaphoreType.DMA],
        name="dispatch",
    )
    def run(rows_hbm, slot_hbm, out_hbm, idx_v, rows_v, sem):
        wid = lax.axis_index("s") * SC_CORES + lax.axis_index("c")

        @pl.loop(0, n_chunks)
        def _(ci):
            base = pl.multiple_of(wid * per_worker + ci * chunk, chunk)
            loads = [pltpu.make_async_copy(rows_hbm.at[pl.ds(base, chunk)], rows_v, sem)]
            loads += [pltpu.make_async_copy(slot_hbm.at[kk, pl.ds(base, chunk)], idx_v.at[kk], sem)
                      for kk in range(TOP_K)]
            for cp in loads:
                cp.start()
            for cp in loads:
                cp.wait()
            copies = [pltpu.make_async_copy(rows_v, out_hbm.at[idx_v.at[kk]], sem) for kk in range(TOP_K)]
            for cp in copies:
                cp.start()
            for cp in copies:
                cp.wait()

    return run(rows, slot)


def _sc_combine(ys, slot):
    _, d = ys.shape
    k, t = slot.shape
    chunk = SC_COMBINE_CHUNK
    per_worker = t // SC_WORKERS
    n_chunks = per_worker // chunk

    @functools.partial(
        pl.kernel, mesh=_sc_mesh(),
        out_type=jax.ShapeDtypeStruct((k, t, d), ys.dtype),
        scratch_types=[pltpu.VMEM((TOP_K, chunk), I32), pltpu.VMEM((TOP_K, chunk, d), ys.dtype),
                       pltpu.SemaphoreType.DMA],
        name="combine",
    )
    def run(ys_hbm, slot_hbm, out_hbm, idx_v, rows_v, sem):
        wid = lax.axis_index("s") * SC_CORES + lax.axis_index("c")

        @pl.loop(0, n_chunks)
        def _(ci):
            base = pl.multiple_of(wid * per_worker + ci * chunk, chunk)
            loads = [pltpu.make_async_copy(slot_hbm.at[kk, pl.ds(base, chunk)], idx_v.at[kk], sem)
                     for kk in range(TOP_K)]
            for cp in loads:
                cp.start()
            for cp in loads:
                cp.wait()
            gathers = [pltpu.make_async_copy(ys_hbm.at[idx_v.at[kk]], rows_v.at[kk], sem) for kk in range(TOP_K)]
            for cp in gathers:
                cp.start()
            for cp in gathers:
                cp.wait()
            stores = [pltpu.make_async_copy(rows_v.at[kk], out_hbm.at[kk, pl.ds(base, chunk)], sem)
                      for kk in range(TOP_K)]
            for cp in stores:
                cp.start()
            for cp in stores:
                cp.wait()

    return run(ys, slot)


def _experts_kernel(be_ref, nused_ref, x_ref, wu_ref, bg_ref, bl_ref, wd_ref, bd_ref, y_ref,
                    glu_w_ref, lin_w_ref, down_w_ref):
    i = pl.program_id(0)
    live = i < nused_ref[0]
    new_expert = (i == 0) | (be_ref[i] != be_ref[jnp.maximum(i - 1, 0)])

    @pl.when(live & new_expert)
    def _():
        tile = 2 * LANES
        src = lax.broadcasted_iota(I32, (tile, tile), 0)
        dst = lax.broadcasted_iota(I32, (tile, tile), 1)
        wanted = jnp.where(dst < LANES, 2 * dst, 2 * (dst - LANES) + 1)
        pick = jnp.where(src == wanted, 1.0, 0.0).astype(BF16)
        for c in range(wu_ref.shape[2] // tile):
            w = wu_ref[0, :, c * tile:(c + 1) * tile].astype(BF16)
            both = jnp.dot(w, pick, preferred_element_type=F32)
            glu_w_ref[:, c * LANES:(c + 1) * LANES] = both[:, :LANES].astype(BF16)
            lin_w_ref[:, c * LANES:(c + 1) * LANES] = both[:, LANES:].astype(BF16)
        down_w_ref[...] = wd_ref[0].astype(BF16)

    @pl.when(live)
    def _():
        lo, hi = _unpack_bf16_pairs(x_ref[...])
        x = jnp.concatenate([lo, hi], axis=1).astype(BF16)
        glu = jnp.dot(x, glu_w_ref[...], preferred_element_type=F32) + bg_ref[0]
        lin = jnp.dot(x, lin_w_ref[...], preferred_element_type=F32) + bl_ref[0]
        glu = jnp.minimum(glu, SWIGLU_LIMIT)
        lin = jnp.clip(lin, -SWIGLU_LIMIT, SWIGLU_LIMIT)
        act = glu * jax.nn.sigmoid(SWIGLU_ALPHA * glu) * (lin + 1.0)
        y = jnp.dot(act.astype(BF16), down_w_ref[...], preferred_element_type=F32) + bd_ref[0]
        y_ref[...] = _pack_bf16_pairs(y)


def _experts(xs, block_expert, n_used, w_up, b_glu, b_lin, w_down, b_down):
    n_slots, half_d = xs.shape
    _, d, f2 = w_up.shape
    f = f2 // 2
    nblk = n_slots // MOE_ROWS
    rows = lambda i, be, nu: (i, 0)
    by_e = lambda i, be, nu: (be[i], 0, 0)
    return pl.pallas_call(
        _experts_kernel,
        grid_spec=pltpu.PrefetchScalarGridSpec(
            num_scalar_prefetch=2, grid=(nblk,),
            in_specs=[pl.BlockSpec((MOE_ROWS, half_d), rows),
                      pl.BlockSpec((1, d, f2), by_e),
                      pl.BlockSpec((1, 1, f), by_e), pl.BlockSpec((1, 1, f), by_e),
                      pl.BlockSpec((1, f, d), by_e), pl.BlockSpec((1, 1, d), by_e)],
            out_specs=pl.BlockSpec((MOE_ROWS, half_d), rows),
            scratch_shapes=[pltpu.VMEM((d, f), BF16), pltpu.VMEM((d, f), BF16), pltpu.VMEM((f, d), BF16)]),
        out_shape=jax.ShapeDtypeStruct((n_slots, half_d), U32),
        compiler_params=_cparams(("arbitrary",)),
        name="experts",
    )(block_expert, n_used, xs, w_up, b_glu, b_lin, w_down, b_down)


def _final_kernel(g_ref, gate_ref, x1_ref, mod_ref, gain_ref, *rest):
    o_ref = rest[-1]
    tm = x1_ref.shape[0]
    gates = gate_ref[...]
    padded = jnp.concatenate([gates, jnp.zeros((LANES - TOP_K, tm), F32)], axis=0)
    gate_cols = padded.T
    y_lo = jnp.zeros(g_ref.shape[1:], F32)
    y_hi = jnp.zeros(g_ref.shape[1:], F32)
    for kk in range(TOP_K):
        lo, hi = _unpack_bf16_pairs(g_ref[kk])
        y_lo = y_lo + lo * gate_cols[:, kk:kk + 1]
        y_hi = y_hi + hi * gate_cols[:, kk:kk + 1]
    y = jnp.concatenate([y_lo, y_hi], axis=1)
    o_ref[...] = x1_ref[...] + mod_ref[0, 5:6, :] * _rms(y, gain_ref[...])


def _final(g, gates, x1, mod3, gain, seq, part, out_so_far):
    t, d = x1.shape
    tm = ROW_BLOCK
    nb = t // tm
    off = part * nb
    per_seq = seq // tm
    row = lambda i: (i, 0)
    in_specs = [pl.BlockSpec((TOP_K, tm, d // 2), lambda i: (0, i, 0)),
                pl.BlockSpec((TOP_K, tm), lambda i: (0, i)),
                pl.BlockSpec((tm, d), row),
                pl.BlockSpec((1, 6, d), lambda i: ((i + off) // per_seq, 0, 0)),
                pl.BlockSpec((1, d), lambda i: (0, 0))]
    args = [g, gates, x1, mod3, gain.reshape(1, d)]
    aliases = {}
    if out_so_far is not None:
        in_specs.append(pl.BlockSpec(memory_space=pl.ANY))
        args.append(out_so_far)
        aliases = {len(args) - 1: 0}
    return pl.pallas_call(
        _final_kernel,
        grid=(nb,),
        in_specs=in_specs,
        out_specs=pl.BlockSpec((tm, d), lambda i: (i + off, 0)),
        out_shape=jax.ShapeDtypeStruct((t * MOE_PARTS, d), F32),
        input_output_aliases=aliases,
        compiler_params=_cparams(("parallel",)),
        name="final",
    )(*args)


def _layer(x2, c, positions, seq, ada_w, ada_b, norm_mix_pre, norm_mix_post, norm_ffn_pre, norm_ffn_post,
           w_in, w_branch_a, w_branch_b, w_out, w_router, b_router, w_up, b_up, w_down, b_down):
    t, d = x2.shape
    batch = t // seq
    mod3 = _adaln(c, ada_w, ada_b).reshape(batch, 6, d)

    qa, ka, va, qb, kb, vb, ga, gb = _in_proj(x2, mod3, norm_mix_pre, positions, w_in.astype(BF16), seq)
    dil = [_dilated_group(qa, ka, va, batch, seq, g) for g in range(len(DIL_GROUPS))]
    sb = _stickbreak(qb, kb, vb, batch, seq)
    wa, wb, wo = w_branch_a.astype(BF16), w_branch_b.astype(BF16), w_out.astype(BF16)
    f = w_up.shape[2] // 2
    b_glu, b_lin = b_up[:, 0::2].reshape(N_EXPERTS, 1, f), b_up[:, 1::2].reshape(N_EXPERTS, 1, f)
    b_down3 = b_down.reshape(N_EXPERTS, 1, d)
    nblk = (t // MOE_PARTS * TOP_K) // MOE_ROWS + N_EXPERTS

    routed = []
    for part in range(MOE_PARTS):
        x1, h2, idx, gates, rank, counts = _post_mix(
            dil, sb, ga, gb, x2, mod3, norm_mix_post, norm_ffn_pre, wa, wb, wo, w_router, b_router, seq, part)
        counts = counts.reshape(N_EXPERTS)
        padded = (counts + MOE_ROWS - 1) // MOE_ROWS * MOE_ROWS
        pad_end = jnp.cumsum(padded)
        pad_start = (pad_end - padded).astype(I32)
        block_first_row = jnp.arange(nblk, dtype=I32)[:, None] * MOE_ROWS
        block_expert = jnp.minimum(jnp.sum(pad_end[None, :] <= block_first_row, axis=1), N_EXPERTS - 1).astype(I32)
        n_used = (pad_end[-1:] // MOE_ROWS).astype(I32)
        slot = _slots(pad_start, idx, rank)
        xs = _sc_dispatch(h2, slot, nblk * MOE_ROWS)
        routed.append((x1, gates, slot, xs, block_expert, n_used))

    gathered = []
    for x1, gates, slot, xs, block_expert, n_used in routed:
        ys = _experts(xs, block_expert, n_used, w_up, b_glu, b_lin, w_down, b_down3)
        gathered.append(_sc_combine(ys, slot))

    out = None
    for part, ((x1, gates, *_), g) in enumerate(zip(routed, gathered)):
        out = _final(g, gates, x1, mod3, norm_ffn_post, seq, part, out)
    return out


def kernel(x, c, positions, ada_w, ada_b, norm_mix_pre, norm_mix_post, norm_ffn_pre, norm_ffn_post,
           w_in, w_branch_a, w_branch_b, w_out, w_router, b_router, w_up, b_up, w_down, b_down):
    batch, seq, d = x.shape
    x2 = x.reshape(batch * seq, d)
    for layer in range(ada_w.shape[0]):
        x2 = _layer(x2, c, positions, seq, ada_w[layer], ada_b[layer], norm_mix_pre[layer], norm_mix_post[layer],
                    norm_ffn_pre[layer], norm_ffn_post[layer], w_in[layer], w_branch_a[layer], w_branch_b[layer],
                    w_out[layer], w_router[layer], b_router[layer], w_up[layer], b_up[layer], w_down[layer],
                    b_down[layer])
    return x2.reshape(batch, seq, d)
```

```python
import functools

import numpy as np
import jax
import jax.numpy as jnp
from jax import lax
from jax.experimental import pallas as pl
from jax.experimental.pallas import tpu as pltpu
from jax.experimental.pallas import tpu_sc as plsc

F32 = jnp.float32
BF16 = jnp.bfloat16
I32 = jnp.int32
U32 = jnp.uint32

HEAD_DIM = 64
DIL_GROUPS = ((128, 1), (512, 4), (2048, 16))
DIL_HEADS_PER_GROUP = 4
DIL_HEADS = DIL_HEADS_PER_GROUP * len(DIL_GROUPS)
DIL_W = DIL_HEADS * HEAD_DIM
DIL_OUT = DIL_HEADS_PER_GROUP * HEAD_DIM
SB_HEADS = 8
SB_W = SB_HEADS * HEAD_DIM
ROPE_THETA = 500000.0
ROPE_DIMS = HEAD_DIM // 4
Q_BLOCK = 128
N_EXPERTS = 32
TOP_K = 4
SWIGLU_ALPHA = 1.702
SWIGLU_LIMIT = 7.0
NORM_EPS = 1e-6
NEG_INF = -1e30

LANES = 128
ROW_BLOCK = 512
MOE_ROWS = 512
MOE_PARTS = 2
DIL_TOKENS = 512
SB_TQ = 256
SB_SUB = 4
SB_LOG_FLOOR = -105.0
VMEM_LIMIT = 56 * 1024 * 1024

SC_CORES = 2
SC_SUBCORES = 16
SC_WORKERS = SC_CORES * SC_SUBCORES
SC_DISPATCH_CHUNK = 64
SC_COMBINE_CHUNK = 32


def _cparams(sem):
    return pltpu.CompilerParams(dimension_semantics=sem, vmem_limit_bytes=VMEM_LIMIT)


def _rms(x, gain):
    ms = jnp.mean(x * x, axis=-1, keepdims=True)
    return x * lax.rsqrt(ms + NORM_EPS) * gain


def _pack_bf16_pairs(x):
    n = x.shape[1] // 2
    u = lax.bitcast_convert_type(x, U32)
    r = (u + jnp.uint32(0x7FFF) + ((u >> 16) & jnp.uint32(1))) >> 16
    return r[:, :n] | (r[:, n:] << 16)


def _unpack_bf16_pairs(w):
    lo = lax.bitcast_convert_type(w << 16, F32)
    hi = lax.bitcast_convert_type(w & jnp.uint32(0xFFFF0000), F32)
    return lo, hi


def _adaln_kernel(c_ref, w_ref, b_ref, o_ref):
    c = c_ref[...]
    s = c * jax.nn.sigmoid(c)
    o_ref[...] = jnp.dot(s, w_ref[...], preferred_element_type=F32,
                         precision=lax.Precision.HIGHEST) + b_ref[...]


def _adaln(c, ada_w, ada_b):
    b, d = c.shape
    n = ada_w.shape[1]
    return pl.pallas_call(
        _adaln_kernel,
        grid=(n // d,),
        in_specs=[pl.BlockSpec((b, d), lambda j: (0, 0)),
                  pl.BlockSpec((d, d), lambda j: (0, j)),
                  pl.BlockSpec((1, d), lambda j: (0, j))],
        out_specs=pl.BlockSpec((b, d), lambda j: (0, j)),
        out_shape=jax.ShapeDtypeStruct((b, n), F32),
        compiler_params=_cparams(("arbitrary",)),
        name="adaln",
    )(c, ada_w, ada_b.reshape(1, n))


def _rope_table():
    half = ROPE_DIMS // 2
    inv_freq = ROPE_THETA ** (-(np.arange(half, dtype=np.float32) * 2.0 / ROPE_DIMS))
    d = np.arange(LANES) % HEAD_DIM
    tab = np.zeros((8, LANES), np.float32)
    tab[0] = np.where(d < ROPE_DIMS, inv_freq[d % half], 0.0)
    tab[1] = np.where(d < half, -1.0, 1.0)
    tab[2] = np.where(d < half, 1.0, 0.0)
    return jnp.asarray(tab)


def _inproj_kernel(x_ref, mod_ref, g_ref, pos_ref, rope_ref, w_ref,
                   qa_ref, ka_ref, va_ref, qb_ref, kb_ref, vb_ref, ga_ref, gb_ref):
    x = x_ref[...]
    tm = x.shape[0]
    shift = mod_ref[0, 0:1, :]
    scale = mod_ref[0, 1:2, :]
    hb = (_rms(x, g_ref[...]) * (1.0 + scale) + shift).astype(BF16)

    pos = pos_ref[0].astype(F32)
    pos_rows = jnp.broadcast_to(pos, (LANES, tm)).T
    ang = pos_rows * rope_ref[0:1, :]
    cos = jnp.cos(ang)
    sin = jnp.sin(ang) * rope_ref[1:2, :]
    first = rope_ref[2:3, :] > 0.5
    half = ROPE_DIMS // 2

    def rope(t):
        partner = jnp.where(first, pltpu.roll(t, LANES - half, 1), pltpu.roll(t, half, 1))
        return t * cos + partner * sin

    qk_scale = HEAD_DIM ** -0.5
    col = 0
    plan = ((qa_ref, DIL_W, "rope_q"), (ka_ref, DIL_W, "rope"), (va_ref, DIL_W, "id"),
            (qb_ref, SB_W, "scale"), (kb_ref, SB_W, "id"), (vb_ref, SB_W, "id"),
            (ga_ref, x.shape[1], "sig"), (gb_ref, x.shape[1], "sig"))
    for ref, width, mode in plan:
        for c0 in range(0, width, 2 * LANES):
            t = jnp.dot(hb, w_ref[:, col + c0:col + c0 + 2 * LANES], preferred_element_type=F32)
            if mode in ("rope", "rope_q"):
                parts = [rope(t[:, :LANES]), rope(t[:, LANES:])]
                t = jnp.concatenate(parts, axis=1)
                if mode == "rope_q":
                    t = t * qk_scale
            elif mode == "scale":
                t = t * qk_scale
            elif mode == "sig":
                t = jax.nn.sigmoid(t)
            ref[:, c0:c0 + 2 * LANES] = t.astype(ref.dtype)
        col += width


def _in_proj(x2, mod3, gain, positions, w_in_bf16, seq):
    t, d = x2.shape
    tm = ROW_BLOCK
    nb = t // tm
    per_seq = seq // tm
    pos3 = positions.reshape(nb, 1, tm)
    widths = (DIL_W, DIL_W, DIL_W, SB_W, SB_W, SB_W, d, d)
    row = lambda i: (i, 0)
    return pl.pallas_call(
        _inproj_kernel,
        grid=(nb,),
        in_specs=[pl.BlockSpec((tm, d), row),
                  pl.BlockSpec((1, 6, d), lambda i: (i // per_seq, 0, 0)),
                  pl.BlockSpec((1, d), lambda i: (0, 0)),
                  pl.BlockSpec((1, 1, tm), lambda i: (i, 0, 0)),
                  pl.BlockSpec((8, LANES), lambda i: (0, 0)),
                  pl.BlockSpec(w_in_bf16.shape, lambda i: (0, 0))],
        out_specs=[pl.BlockSpec((tm, w), row) for w in widths],
        out_shape=[jax.ShapeDtypeStruct((t, w), BF16) for w in widths],
        compiler_params=_cparams(("parallel",)),
        name="in_proj",
    )(x2, mod3, gain.reshape(1, d), pos3, _rope_table(), w_in_bf16)


def _dilated_kernel(q_ref, kp_ref, kc_ref, vp_ref, vc_ref, o_ref, l_ref,
                    qf_ref, kf_ref, vf_ref, of_ref, lf_ref, *, r, nq):
    n = pl.program_id(1)
    prev_rows = Q_BLOCK * r
    halves = DIL_OUT // LANES
    for c in range(halves):
        cols = slice(c * LANES, (c + 1) * LANES)
        qf_ref[c] = q_ref[:, cols].astype(F32)
        kf_ref[c, 0:prev_rows, :] = kp_ref[:, cols].astype(F32)
        kf_ref[c, prev_rows:, :] = kc_ref[:, cols].astype(F32)
        vf_ref[c, 0:prev_rows, :] = vp_ref[:, cols].astype(F32)
        vf_ref[c, prev_rows:, :] = vc_ref[:, cols].astype(F32)

    def gather_rows(ref, start, size):
        idx = pl.ds(start, size, stride=r) if r > 1 else pl.ds(start, size)
        return jnp.concatenate([ref[c, idx, :] for c in range(halves)], axis=1).astype(BF16)

    def scatter_rows(ref, start, val):
        idx = pl.ds(start, Q_BLOCK, stride=r) if r > 1 else pl.ds(start, Q_BLOCK)
        for c in range(halves):
            ref[c, idx, :] = val[:, c * LANES:(c + 1) * LANES]

    qi = lax.broadcasted_iota(I32, (Q_BLOCK, 2 * Q_BLOCK), 0) + Q_BLOCK
    ki = lax.broadcasted_iota(I32, (Q_BLOCK, 2 * Q_BLOCK), 1)
    dist = qi - ki
    band = (dist >= 0) & (dist <= Q_BLOCK)
    lane = lax.broadcasted_iota(I32, (1, DIL_OUT), 1)

    heads = [(lane >= h * HEAD_DIM) & (lane < (h + 1) * HEAD_DIM) for h in range(DIL_HEADS_PER_GROUP)]
    ones = jnp.ones((2 * Q_BLOCK, LANES), BF16)

    def sub_block(idx, carry):
        rho = idx % r
        qb = idx // r
        base = r * Q_BLOCK * qb + rho
        q = gather_rows(qf_ref, base, Q_BLOCK)
        k = gather_rows(kf_ref, base, 2 * Q_BLOCK)
        v = gather_rows(vf_ref, base, 2 * Q_BLOCK)
        valid = band & ((n * nq + qb - 1) * Q_BLOCK + ki >= 0)
        qs = jnp.concatenate([jnp.where(heads[h], q, jnp.zeros_like(q)) for h in range(DIL_HEADS_PER_GROUP)], axis=0)
        s = lax.dot_general(qs, k, (((1,), (1,)), ((), ())), preferred_element_type=F32)
        s = jnp.where(jnp.concatenate([valid] * DIL_HEADS_PER_GROUP, axis=0), s, NEG_INF)
        m = jnp.max(s, axis=1, keepdims=True)
        p = jnp.exp(s - m).astype(BF16)
        pv = jnp.dot(p, v, preferred_element_type=F32)
        den = jnp.dot(p, ones, preferred_element_type=F32)
        inv = 1.0 / den
        lse = m + jnp.log(den)
        o = jnp.zeros((Q_BLOCK, DIL_OUT), F32)
        l = jnp.zeros((Q_BLOCK, DIL_OUT), F32)
        for h in range(DIL_HEADS_PER_GROUP):
            blk = slice(h * Q_BLOCK, (h + 1) * Q_BLOCK)
            o = jnp.where(heads[h], pv[blk] * jnp.concatenate([inv[blk]] * halves, axis=1), o)
            l = jnp.where(heads[h], jnp.concatenate([lse[blk]] * halves, axis=1), l)
        scatter_rows(of_ref, base, o)
        scatter_rows(lf_ref, base, l)
        return carry

    lax.fori_loop(0, r * nq, sub_block, 0, unroll=2)
    for c in range(halves):
        o_ref[:, c * LANES:(c + 1) * LANES] = of_ref[c]
        l_ref[:, c * LANES:(c + 1) * LANES] = lf_ref[c]


def _dilated_group(qa, ka, va, batch, seq, group):
    _, r = DIL_GROUPS[group]
    nq = max(1, DIL_TOKENS // (Q_BLOCK * r))
    prev_rows = Q_BLOCK * r
    cur_rows = prev_rows * nq
    halves = DIL_OUT // LANES
    view = lambda a: a.reshape(batch, seq, DIL_W)
    cur = lambda b, n: (b, n, group)
    prev = lambda b, n: (b, jnp.maximum(n * nq - 1, 0), group)
    cur_spec = pl.BlockSpec((None, cur_rows, DIL_OUT), cur)
    prev_spec = pl.BlockSpec((None, prev_rows, DIL_OUT), prev)
    out_spec = pl.BlockSpec((None, cur_rows, DIL_OUT), lambda b, n: (b, n, 0))
    o, l = pl.pallas_call(
        functools.partial(_dilated_kernel, r=r, nq=nq),
        grid=(batch, seq // cur_rows),
        in_specs=[cur_spec, prev_spec, cur_spec, prev_spec, cur_spec],
        out_specs=[out_spec, out_spec],
        out_shape=[jax.ShapeDtypeStruct((batch, seq, DIL_OUT), F32)] * 2,
        scratch_shapes=[pltpu.VMEM((halves, cur_rows, LANES), F32),
                        pltpu.VMEM((halves, prev_rows + cur_rows, LANES), F32),
                        pltpu.VMEM((halves, prev_rows + cur_rows, LANES), F32),
                        pltpu.VMEM((halves, cur_rows, LANES), F32),
                        pltpu.VMEM((halves, cur_rows, LANES), F32)],
        compiler_params=_cparams(("parallel", "arbitrary")),
        name=f"dilated_g{group}",
    )(view(qa), view(ka), view(ka), view(va), view(va))
    return o.reshape(batch * seq, DIL_OUT), l.reshape(batch * seq, DIL_OUT)


def _stickbreak_kernel(q_ref, k_ref, v_ref, o_ref, acc_ref, csum_ref, terms_ref, lb_ref):
    step = pl.program_id(2)
    tq = SB_TQ
    lane = lax.broadcasted_iota(I32, (1, LANES), 1)
    row = lax.broadcasted_iota(I32, (2 * tq, tq), 0)
    causal = lax.broadcasted_iota(I32, (2 * tq, tq), 1) < jnp.where(row >= tq, row - tq, row)
    later = (lax.broadcasted_iota(I32, (tq, tq), 0) > lax.broadcasted_iota(I32, (tq, tq), 1))
    later = jnp.where(later, 1.0, 0.0).astype(BF16)
    later2 = jnp.concatenate([later, later], axis=0)

    def stacked_queries(u):
        q = q_ref[u * tq:(u + 1) * tq, :]
        zero = jnp.zeros_like(q)
        return jnp.concatenate([jnp.where(lane < HEAD_DIM, q, zero), jnp.where(lane >= HEAD_DIM, q, zero)], axis=0)

    def emit(u, acc):
        o_ref[u * tq:(u + 1) * tq, :] = jnp.where(lane < HEAD_DIM, acc[:tq], acc[tq:]).astype(o_ref.dtype)


    def raw_scores(qs, j):
        start = pl.multiple_of(j * tq, tq)
        k = k_ref[pl.ds(start, tq), :]
        return lax.dot_general(qs, k, (((1,), (1,)), ((), ())), preferred_element_type=F32)

    def log_terms(z, diagonal):
        sp = jnp.maximum(z, 0.0) + jnp.log(1.0 + jnp.exp(-jnp.abs(z)))
        log_not = jnp.where(causal, -sp, 0.0) if diagonal else -sp
        log_beta = jnp.where(causal, z - sp, NEG_INF) if diagonal else z - sp
        hi = log_not.astype(BF16)
        terms_ref[:, :tq] = hi
        terms_ref[:, tq:] = (log_not - hi.astype(F32)).astype(BF16)
        csum = csum_ref[...]
        lb_ref[...] = log_beta + csum
        csum = csum + jnp.sum(log_not, axis=1, keepdims=True)
        csum_ref[...] = csum
        return jnp.max(csum)

    def later_sums():
        return jnp.dot(terms_ref[...], later2, preferred_element_type=F32)

    def weighted_values(j, inner):
        start = pl.multiple_of(j * tq, tq)
        v = v_ref[pl.ds(start, tq), :]
        w = jnp.exp(lb_ref[...] + inner)
        return jnp.dot(w.astype(BF16), v, preferred_element_type=F32)

    def cond(state):
        j, top = state
        return (j >= 0) & (top > SB_LOG_FLOOR)

    pending = None
    for u in range(SB_SUB):
        i = step * SB_SUB + u
        qs = stacked_queries(u)
        if pending is None:
            z = raw_scores(qs, i)
            pv = None
        else:
            inner = later_sums()
            z = raw_scores(qs, i)
            pv = weighted_values(pending[1] + 1, inner)
        csum_ref[...] = jnp.zeros_like(csum_ref)
        acc_ref[u] = jnp.zeros(acc_ref.shape[1:], F32)
        top = log_terms(z, True)
        if pending is not None:
            emit(pending[0], acc_ref[pending[0]] + pv)

        def body(state, qs=qs, u=u):
            j, _ = state
            inner = later_sums()
            z = raw_scores(qs, j)
            pv = weighted_values(j + 1, inner)
            top = log_terms(z, False)
            acc_ref[u] += pv
            return j - 1, top

        last, _ = lax.while_loop(cond, body, (i - 1, top))
        pending = (u, last)
    emit(pending[0], acc_ref[pending[0]] + weighted_values(pending[1] + 1, later_sums()))


def _stickbreak(qb, kb, vb, batch, seq):
    view = lambda a: a.reshape(batch, seq, SB_W)
    rows = SB_TQ * SB_SUB
    pairs = SB_W // LANES
    kv_spec = pl.BlockSpec((None, seq, LANES), lambda b, hp, i: (b, 0, hp))
    q_spec = pl.BlockSpec((None, rows, LANES), lambda b, hp, i: (b, i, hp))
    out = pl.pallas_call(
        _stickbreak_kernel,
        grid=(batch, pairs, seq // rows),
        in_specs=[q_spec, kv_spec, kv_spec],
        out_specs=q_spec,
        out_shape=jax.ShapeDtypeStruct((batch, seq, SB_W), BF16),
        scratch_shapes=[pltpu.VMEM((SB_SUB, 2 * SB_TQ, LANES), F32), pltpu.VMEM((2 * SB_TQ, 1), F32),
                        pltpu.VMEM((2 * SB_TQ, 2 * SB_TQ), BF16), pltpu.VMEM((2 * SB_TQ, SB_TQ), F32)],
        compiler_params=_cparams(("parallel", "parallel", "arbitrary")),
        name="stickbreak",
    )(view(qb), view(kb), view(vb))
    return out.reshape(batch * seq, SB_W)


def _postmix_kernel(o0_ref, o1_ref, o2_ref, l0_ref, l1_ref, l2_ref, sb_ref, ga_ref, gb_ref, x_ref,
                    mod_ref, gpost_ref, gffn_ref, wa_ref, wb_ref, wo_ref, wr_ref, br_ref,
                    x1_ref, h2_ref, idx_ref, gate_ref, rank_ref, cnt_ref, carry_ref):
    step = pl.program_id(0)
    tm = x_ref.shape[0]

    @pl.when(step == 0)
    def _():
        carry_ref[...] = jnp.zeros_like(carry_ref)

    l0, l1, l2 = l0_ref[...], l1_ref[...], l2_ref[...]
    m = jnp.maximum(jnp.maximum(l0, l1), l2)
    e0, e1, e2 = jnp.exp(l0 - m), jnp.exp(l1 - m), jnp.exp(l2 - m)
    merged = (e0 * o0_ref[...] + e1 * o1_ref[...] + e2 * o2_ref[...]) / (e0 + e1 + e2)
    ya = jnp.dot(merged.astype(BF16), wa_ref[...], preferred_element_type=F32)
    yb = jnp.dot(sb_ref[...], wb_ref[...], preferred_element_type=F32)
    mix = ga_ref[...].astype(F32) * ya + gb_ref[...].astype(F32) * yb
    y = jnp.dot(mix.astype(BF16), wo_ref[...], preferred_element_type=F32)
    gate1 = mod_ref[0, 2:3, :]
    x1 = x_ref[...] + gate1 * _rms(y, gpost_ref[...])
    x1_ref[...] = x1
    h2 = _rms(x1, gffn_ref[...]) * (1.0 + mod_ref[0, 4:5, :]) + mod_ref[0, 3:4, :]
    h2_ref[...] = _pack_bf16_pairs(h2)

    logits = lax.dot_general(wr_ref[...], h2, (((1,), (1,)), ((), ())), preferred_element_type=F32,
                             precision=lax.Precision.HIGHEST) + br_ref[...]
    e_iota = lax.broadcasted_iota(I32, (N_EXPERTS, tm), 0)
    before = (lax.broadcasted_iota(I32, (tm, tm), 0) < lax.broadcasted_iota(I32, (tm, tm), 1))
    before = jnp.where(before, 1.0, 0.0).astype(BF16)

    picks, vals = [], []
    work = logits
    chosen = jnp.zeros((N_EXPERTS, tm), F32)
    for _ in range(TOP_K):
        top = jnp.max(work, axis=0, keepdims=True)
        idx = jnp.min(jnp.where(work == top, e_iota, N_EXPERTS), axis=0, keepdims=True)
        one = e_iota == idx
        picks.append((idx, one))
        vals.append(top)
        chosen = jnp.where(one, 1.0, chosen)
        work = jnp.where(one, -jnp.inf, work)
    exps = [jnp.exp(v - vals[0]) for v in vals]
    den = exps[0] + exps[1] + exps[2] + exps[3]

    rank = jnp.dot(chosen.astype(BF16), before, preferred_element_type=F32) + carry_ref[...]
    carry_ref[...] += jnp.sum(chosen, axis=1, keepdims=True)
    cnt_ref[...] = carry_ref[...].astype(I32)
    for kk, (idx, one) in enumerate(picks):
        idx_ref[kk:kk + 1, :] = idx
        gate_ref[kk:kk + 1, :] = exps[kk] / den
        rank_ref[kk:kk + 1, :] = jnp.sum(jnp.where(one, rank, 0.0), axis=0, keepdims=True).astype(I32)


def _post_mix(dil, sb, ga, gb, x2, mod3, gpost, gffn, wa, wb, wo, w_router, b_router, seq, part):
    t_all, d = x2.shape
    t = t_all // MOE_PARTS
    tm = ROW_BLOCK
    nb = t // tm
    off = part * nb
    per_seq = seq // tm
    row_in = lambda i: (i + off, 0)
    row = lambda i: (i, 0)
    const = lambda i: (0, 0)
    colblk = lambda i: (0, i)
    (o0, l0), (o1, l1), (o2, l2) = dil
    in_specs = ([pl.BlockSpec((tm, DIL_OUT), row_in)] * 6
                + [pl.BlockSpec((tm, SB_W), row_in), pl.BlockSpec((tm, d), row_in), pl.BlockSpec((tm, d), row_in),
                   pl.BlockSpec((tm, d), row_in),
                   pl.BlockSpec((1, 6, d), lambda i: ((i + off) // per_seq, 0, 0)),
                   pl.BlockSpec((1, d), const), pl.BlockSpec((1, d), const),
                   pl.BlockSpec(wa.shape, const), pl.BlockSpec(wb.shape, const), pl.BlockSpec(wo.shape, const),
                   pl.BlockSpec((N_EXPERTS, d), const), pl.BlockSpec((N_EXPERTS, 1), const)])
    out_specs = [pl.BlockSpec((tm, d), row), pl.BlockSpec((tm, d // 2), row),
                 pl.BlockSpec((TOP_K, tm), colblk), pl.BlockSpec((TOP_K, tm), colblk),
                 pl.BlockSpec((TOP_K, tm), colblk), pl.BlockSpec((N_EXPERTS, 1), const)]
    out_shape = [jax.ShapeDtypeStruct((t, d), F32), jax.ShapeDtypeStruct((t, d // 2), U32),
                 jax.ShapeDtypeStruct((TOP_K, t), I32), jax.ShapeDtypeStruct((TOP_K, t), F32),
                 jax.ShapeDtypeStruct((TOP_K, t), I32), jax.ShapeDtypeStruct((N_EXPERTS, 1), I32)]
    return pl.pallas_call(
        _postmix_kernel,
        grid=(nb,),
        in_specs=in_specs,
        out_specs=out_specs,
        out_shape=out_shape,
        scratch_shapes=[pltpu.VMEM((N_EXPERTS, 1), F32)],
        compiler_params=_cparams(("arbitrary",)),
        name="post_mix",
    )(o0, o1, o2, l0, l1, l2, sb, ga, gb, x2, mod3, gpost.reshape(1, d), gffn.reshape(1, d),
      wa, wb, wo, w_router.T, b_router.reshape(N_EXPERTS, 1))


def _slots_kernel(start_ref, idx_ref, rank_ref, slot_ref):
    idx = idx_ref[...]
    base = jnp.zeros_like(idx)
    for e in range(N_EXPERTS):
        base = jnp.where(idx == e, start_ref[e], base)
    slot_ref[...] = base + rank_ref[...]


def _slots(pad_start, idx, rank):
    k, t = idx.shape
    tb = min(t, 8192)
    blk = pl.BlockSpec((k, tb), lambda i, s: (0, i))
    return pl.pallas_call(
        _slots_kernel,
        grid_spec=pltpu.PrefetchScalarGridSpec(
            num_scalar_prefetch=1, grid=(t // tb,), in_specs=[blk, blk], out_specs=blk),
        out_shape=jax.ShapeDtypeStruct((k, t), I32),
        compiler_params=_cparams(("parallel",)),
        name="slots",
    )(pad_start, idx, rank)


def _sc_mesh():
    return plsc.VectorSubcoreMesh(core_axis_name="c", subcore_axis_name="s")


def _sc_dispatch(rows, slot, n_slots):
    t, d = rows.shape
    chunk = SC_DISPATCH_CHUNK
    per_worker = t // SC_WORKERS
    n_chunks = per_worker // chunk

    @functools.partial(
        pl.kernel, mesh=_sc_mesh(),
        out_type=jax.ShapeDtypeStruct((n_slots, d), rows.dtype),
        scratch_types=[pltpu.VMEM((TOP_K, chunk), I32), pltpu.VMEM((chunk, d), rows.dtype),
                       pltpu.SemaphoreType.DMA],
        name="dispatch",
    )
    def run(rows_hbm, slot_hbm, out_hbm, idx_v, rows_v, sem):
        wid = lax.axis_index("s") * SC_CORES + lax.axis_index("c")

        @pl.loop(0, n_chunks)
        def _(ci):
            base = pl.multiple_of(wid * per_worker + ci * chunk, chunk)
            loads = [pltpu.make_async_copy(rows_hbm.at[pl.ds(base, chunk)], rows_v, sem)]
            loads += [pltpu.make_async_copy(slot_hbm.at[kk, pl.ds(base, chunk)], idx_v.at[kk], sem)
                      for kk in range(TOP_K)]
            for cp in loads:
                cp.start()
            for cp in loads:
                cp.wait()
            copies = [pltpu.make_async_copy(rows_v, out_hbm.at[idx_v.at[kk]], sem) for kk in range(TOP_K)]
            for cp in copies:
                cp.start()
            for cp in copies:
                cp.wait()

    return run(rows, slot)


def _sc_combine(ys, slot):
    _, d = ys.shape
    k, t = slot.shape
    chunk = SC_COMBINE_CHUNK
    per_worker = t // SC_WORKERS
    n_chunks = per_worker // chunk

    @functools.partial(
        pl.kernel, mesh=_sc_mesh(),
        out_type=jax.ShapeDtypeStruct((k, t, d), ys.dtype),
        scratch_types=[pltpu.VMEM((TOP_K, chunk), I32), pltpu.VMEM((TOP_K, chunk, d), ys.dtype),
                       pltpu.SemaphoreType.DMA],
        name="combine",
    )
    def run(ys_hbm, slot_hbm, out_hbm, idx_v, rows_v, sem):
        wid = lax.axis_index("s") * SC_CORES + lax.axis_index("c")

        @pl.loop(0, n_chunks)
        def _(ci):
            base = pl.multiple_of(wid * per_worker + ci * chunk, chunk)
            loads = [pltpu.make_async_copy(slot_hbm.at[kk, pl.ds(base, chunk)], idx_v.at[kk], sem)
                     for kk in range(TOP_K)]
            for cp in loads:
                cp.start()
            for cp in loads:
                cp.wait()
            gathers = [pltpu.make_async_copy(ys_hbm.at[idx_v.at[kk]], rows_v.at[kk], sem) for kk in range(TOP_K)]
            for cp in gathers:
                cp.start()
            for cp in gathers:
                cp.wait()
            stores = [pltpu.make_async_copy(rows_v.at[kk], out_hbm.at[kk, pl.ds(base, chunk)], sem)
                      for kk in range(TOP_K)]
            for cp in stores:
                cp.start()
            for cp in stores:
                cp.wait()

    return run(ys, slot)


def _experts_kernel(be_ref, nused_ref, x_ref, wu_ref, bg_ref, bl_ref, wd_ref, bd_ref, y_ref,
                    glu_w_ref, lin_w_ref, down_w_ref):
    i = pl.program_id(0)
    live = i < nused_ref[0]
    new_expert = (i == 0) | (be_ref[i] != be_ref[jnp.maximum(i - 1, 0)])

    @pl.when(live & new_expert)
    def _():
        tile = 2 * LANES
        src = lax.broadcasted_iota(I32, (tile, tile), 0)
        dst = lax.broadcasted_iota(I32, (tile, tile), 1)
        wanted = jnp.where(dst < LANES, 2 * dst, 2 * (dst - LANES) + 1)
        pick = jnp.where(src == wanted, 1.0, 0.0).astype(BF16)
        for c in range(wu_ref.shape[2] // tile):
            w = wu_ref[0, :, c * tile:(c + 1) * tile].astype(BF16)
            both = jnp.dot(w, pick, preferred_element_type=F32)
            glu_w_ref[:, c * LANES:(c + 1) * LANES] = both[:, :LANES].astype(BF16)
            lin_w_ref[:, c * LANES:(c + 1) * LANES] = both[:, LANES:].astype(BF16)
        down_w_ref[...] = wd_ref[0].astype(BF16)

    @pl.when(live)
    def _():
        lo, hi = _unpack_bf16_pairs(x_ref[...])
        x = jnp.concatenate([lo, hi], axis=1).astype(BF16)
        glu = jnp.dot(x, glu_w_ref[...], preferred_element_type=F32) + bg_ref[0]
        lin = jnp.dot(x, lin_w_ref[...], preferred_element_type=F32) + bl_ref[0]
        glu = jnp.minimum(glu, SWIGLU_LIMIT)
        lin = jnp.clip(lin, -SWIGLU_LIMIT, SWIGLU_LIMIT)
        act = glu * jax.nn.sigmoid(SWIGLU_ALPHA * glu) * (lin + 1.0)
        y = jnp.dot(act.astype(BF16), down_w_ref[...], preferred_element_type=F32) + bd_ref[0]
        y_ref[...] = _pack_bf16_pairs(y)


def _experts(xs, block_expert, n_used, w_up, b_glu, b_lin, w_down, b_down):
    n_slots, half_d = xs.shape
    _, d, f2 = w_up.shape
    f = f2 // 2
    nblk = n_slots // MOE_ROWS
    rows = lambda i, be, nu: (i, 0)
    by_e = lambda i, be, nu: (be[i], 0, 0)
    return pl.pallas_call(
        _experts_kernel,
        grid_spec=pltpu.PrefetchScalarGridSpec(
            num_scalar_prefetch=2, grid=(nblk,),
            in_specs=[pl.BlockSpec((MOE_ROWS, half_d), rows),
                      pl.BlockSpec((1, d, f2), by_e),
                      pl.BlockSpec((1, 1, f), by_e), pl.BlockSpec((1, 1, f), by_e),
                      pl.BlockSpec((1, f, d), by_e), pl.BlockSpec((1, 1, d), by_e)],
            out_specs=pl.BlockSpec((MOE_ROWS, half_d), rows),
            scratch_shapes=[pltpu.VMEM((d, f), BF16), pltpu.VMEM((d, f), BF16), pltpu.VMEM((f, d), BF16)]),
        out_shape=jax.ShapeDtypeStruct((n_slots, half_d), U32),
        compiler_params=_cparams(("arbitrary",)),
        name="experts",
    )(block_expert, n_used, xs, w_up, b_glu, b_lin, w_down, b_down)


def _final_kernel(g_ref, gate_ref, x1_ref, mod_ref, gain_ref, *rest):
    o_ref = rest[-1]
    tm = x1_ref.shape[0]
    gates = gate_ref[...]
    padded = jnp.concatenate([gates, jnp.zeros((LANES - TOP_K, tm), F32)], axis=0)
    gate_cols = padded.T
    y_lo = jnp.zeros(g_ref.shape[1:], F32)
    y_hi = jnp.zeros(g_ref.shape[1:], F32)
    for kk in range(TOP_K):
        lo, hi = _unpack_bf16_pairs(g_ref[kk])
        y_lo = y_lo + lo * gate_cols[:, kk:kk + 1]
        y_hi = y_hi + hi * gate_cols[:, kk:kk + 1]
    y = jnp.concatenate([y_lo, y_hi], axis=1)
    o_ref[...] = x1_ref[...] + mod_ref[0, 5:6, :] * _rms(y, gain_ref[...])


def _final(g, gates, x1, mod3, gain, seq, part, out_so_far):
    t, d = x1.shape
    tm = ROW_BLOCK
    nb = t // tm
    off = part * nb
    per_seq = seq // tm
    row = lambda i: (i, 0)
    in_specs = [pl.BlockSpec((TOP_K, tm, d // 2), lambda i: (0, i, 0)),
                pl.BlockSpec((TOP_K, tm), lambda i: (0, i)),
                pl.BlockSpec((tm, d), row),
                pl.BlockSpec((1, 6, d), lambda i: ((i + off) // per_seq, 0, 0)),
                pl.BlockSpec((1, d), lambda i: (0, 0))]
    args = [g, gates, x1, mod3, gain.reshape(1, d)]
    aliases = {}
    if out_so_far is not None:
        in_specs.append(pl.BlockSpec(memory_space=pl.ANY))
        args.append(out_so_far)
        aliases = {len(args) - 1: 0}
    return pl.pallas_call(
        _final_kernel,
        grid=(nb,),
        in_specs=in_specs,
        out_specs=pl.BlockSpec((tm, d), lambda i: (i + off, 0)),
        out_shape=jax.ShapeDtypeStruct((t * MOE_PARTS, d), F32),
        input_output_aliases=aliases,
        compiler_params=_cparams(("parallel",)),
        name="final",
    )(*args)


def _layer(x2, c, positions, seq, ada_w, ada_b, norm_mix_pre, norm_mix_post, norm_ffn_pre, norm_ffn_post,
           w_in, w_branch_a, w_branch_b, w_out, w_router, b_router, w_up, b_up, w_down, b_down):
    t, d = x2.shape
    batch = t // seq
    mod3 = _adaln(c, ada_w, ada_b).reshape(batch, 6, d)

    qa, ka, va, qb, kb, vb, ga, gb = _in_proj(x2, mod3, norm_mix_pre, positions, w_in.astype(BF16), seq)
    dil = [_dilated_group(qa, ka, va, batch, seq, g) for g in range(len(DIL_GROUPS))]
    sb = _stickbreak(qb, kb, vb, batch, seq)
    wa, wb, wo = w_branch_a.astype(BF16), w_branch_b.astype(BF16), w_out.astype(BF16)
    f = w_up.shape[2] // 2
    b_glu, b_lin = b_up[:, 0::2].reshape(N_EXPERTS, 1, f), b_up[:, 1::2].reshape(N_EXPERTS, 1, f)
    b_down3 = b_down.reshape(N_EXPERTS, 1, d)
    nblk = (t // MOE_PARTS * TOP_K) // MOE_ROWS + N_EXPERTS

    routed = []
    for part in range(MOE_PARTS):
        x1, h2, idx, gates, rank, counts = _post_mix(
            dil, sb, ga, gb, x2, mod3, norm_mix_post, norm_ffn_pre, wa, wb, wo, w_router, b_router, seq, part)
        counts = counts.reshape(N_EXPERTS)
        padded = (counts + MOE_ROWS - 1) // MOE_ROWS * MOE_ROWS
        pad_end = jnp.cumsum(padded)
        pad_start = (pad_end - padded).astype(I32)
        block_first_row = jnp.arange(nblk, dtype=I32)[:, None] * MOE_ROWS
        block_expert = jnp.minimum(jnp.sum(pad_end[None, :] <= block_first_row, axis=1), N_EXPERTS - 1).astype(I32)
        n_used = (pad_end[-1:] // MOE_ROWS).astype(I32)
        slot = _slots(pad_start, idx, rank)
        xs = _sc_dispatch(h2, slot, nblk * MOE_ROWS)
        routed.append((x1, gates, slot, xs, block_expert, n_used))

    gathered = []
    for x1, gates, slot, xs, block_expert, n_used in routed:
        ys = _experts(xs, block_expert, n_used, w_up, b_glu, b_lin, w_down, b_down3)
        gathered.append(_sc_combine(ys, slot))

    out = None
    for part, ((x1, gates, *_), g) in enumerate(zip(routed, gathered)):
        out = _final(g, gates, x1, mod3, norm_ffn_post, seq, part, out)
    return out


def kernel(x, c, positions, ada_w, ada_b, norm_mix_pre, norm_mix_post, norm_ffn_pre, norm_ffn_post,
           w_in, w_branch_a, w_branch_b, w_out, w_router, b_router, w_up, b_up, w_down, b_down):
    batch, seq, d = x.shape
    x2 = x.reshape(batch * seq, d)
    for layer in range(ada_w.shape[0]):
        x2 = _layer(x2, c, positions, seq, ada_w[layer], ada_b[layer], norm_mix_pre[layer], norm_mix_post[layer],
                    norm_ffn_pre[layer], norm_ffn_post[layer], w_in[layer], w_branch_a[layer], w_branch_b[layer],
                    w_out[layer], w_router[layer], b_router[layer], w_up[layer], b_up[layer], w_down[layer],
                    b_down[layer])
    return x2.reshape(batch, seq, d)
```

```python
import functools

import numpy as np
import jax
import jax.numpy as jnp
from jax import lax
from jax.experimental import pallas as pl
from jax.experimental.pallas import tpu as pltpu
from jax.experimental.pallas import tpu_sc as plsc

F32 = jnp.float32
BF16 = jnp.bfloat16
I32 = jnp.int32
U32 = jnp.uint32

HEAD_DIM = 64
DIL_GROUPS = ((128, 1), (512, 4), (2048, 16))
DIL_HEADS_PER_GROUP = 4
DIL_HEADS = DIL_HEADS_PER_GROUP * len(DIL_GROUPS)
DIL_W = DIL_HEADS * HEAD_DIM
DIL_OUT = DIL_HEADS_PER_GROUP * HEAD_DIM
SB_HEADS = 8
SB_W = SB_HEADS * HEAD_DIM
ROPE_THETA = 500000.0
ROPE_DIMS = HEAD_DIM // 4
Q_BLOCK = 128
N_EXPERTS = 32
TOP_K = 4
SWIGLU_ALPHA = 1.702
SWIGLU_LIMIT = 7.0
NORM_EPS = 1e-6
NEG_INF = -1e30

LANES = 128
ROW_BLOCK = 512
MOE_ROWS = 512
MOE_PARTS = 2
DIL_TOKENS = 512
SB_TQ = 256
SB_SUB = 4
SB_LOG_FLOOR = -105.0
VMEM_LIMIT = 56 * 1024 * 1024

SC_CORES = 2
SC_SUBCORES = 16
SC_WORKERS = SC_CORES * SC_SUBCORES
SC_DISPATCH_CHUNK = 64
SC_COMBINE_CHUNK = 32


def _cparams(sem):
    return pltpu.CompilerParams(dimension_semantics=sem, vmem_limit_bytes=VMEM_LIMIT)


def _rms(x, gain):
    ms = jnp.mean(x * x, axis=-1, keepdims=True)
    return x * lax.rsqrt(ms + NORM_EPS) * gain


def _pack_bf16_pairs(x):
    n = x.shape[1] // 2
    u = lax.bitcast_convert_type(x, U32)
    r = (u + jnp.uint32(0x7FFF) + ((u >> 16) & jnp.uint32(1))) >> 16
    return r[:, :n] | (r[:, n:] << 16)


def _unpack_bf16_pairs(w):
    lo = lax.bitcast_convert_type(w << 16, F32)
    hi = lax.bitcast_convert_type(w & jnp.uint32(0xFFFF0000), F32)
    return lo, hi


def _adaln_kernel(c_ref, w_ref, b_ref, o_ref):
    c = c_ref[...]
    s = c * jax.nn.sigmoid(c)
    o_ref[...] = jnp.dot(s, w_ref[...], preferred_element_type=F32,
                         precision=lax.Precision.HIGHEST) + b_ref[...]


def _adaln(c, ada_w, ada_b):
    b, d = c.shape
    n = ada_w.shape[1]
    return pl.pallas_call(
        _adaln_kernel,
        grid=(n // d,),
        in_specs=[pl.BlockSpec((b, d), lambda j: (0, 0)),
                  pl.BlockSpec((d, d), lambda j: (0, j)),
                  pl.BlockSpec((1, d), lambda j: (0, j))],
        out_specs=pl.BlockSpec((b, d), lambda j: (0, j)),
        out_shape=jax.ShapeDtypeStruct((b, n), F32),
        compiler_params=_cparams(("arbitrary",)),
        name="adaln",
    )(c, ada_w, ada_b.reshape(1, n))


def _rope_table():
    half = ROPE_DIMS // 2
    inv_freq = ROPE_THETA ** (-(np.arange(half, dtype=np.float32) * 2.0 / ROPE_DIMS))
    d = np.arange(LANES) % HEAD_DIM
    tab = np.zeros((8, LANES), np.float32)
    tab[0] = np.where(d < ROPE_DIMS, inv_freq[d % half], 0.0)
    tab[1] = np.where(d < half, -1.0, 1.0)
    tab[2] = np.where(d < half, 1.0, 0.0)
    return jnp.asarray(tab)


def _inproj_kernel(x_ref, mod_ref, g_ref, pos_ref, rope_ref, w_ref,
                   qa_ref, ka_ref, va_ref, qb_ref, kb_ref, vb_ref, ga_ref, gb_ref):
    x = x_ref[...]
    tm = x.shape[0]
    shift = mod_ref[0, 0:1, :]
    scale = mod_ref[0, 1:2, :]
    hb = (_rms(x, g_ref[...]) * (1.0 + scale) + shift).astype(BF16)

    pos = pos_ref[0].astype(F32)
    pos_rows = jnp.broadcast_to(pos, (LANES, tm)).T
    ang = pos_rows * rope_ref[0:1, :]
    cos = jnp.cos(ang)
    sin = jnp.sin(ang) * rope_ref[1:2, :]
    first = rope_ref[2:3, :] > 0.5
    half = ROPE_DIMS // 2

    def rope(t):
        partner = jnp.where(first, pltpu.roll(t, LANES - half, 1), pltpu.roll(t, half, 1))
        return t * cos + partner * sin

    qk_scale = HEAD_DIM ** -0.5
    col = 0
    plan = ((qa_ref, DIL_W, "rope_q"), (ka_ref, DIL_W, "rope"), (va_ref, DIL_W, "id"),
            (qb_ref, SB_W, "scale"), (kb_ref, SB_W, "id"), (vb_ref, SB_W, "id"),
            (ga_ref, x.shape[1], "sig"), (gb_ref, x.shape[1], "sig"))
    for ref, width, mode in plan:
        for c0 in range(0, width, 2 * LANES):
            t = jnp.dot(hb, w_ref[:, col + c0:col + c0 + 2 * LANES], preferred_element_type=F32)
            if mode in ("rope", "rope_q"):
                parts = [rope(t[:, :LANES]), rope(t[:, LANES:])]
                t = jnp.concatenate(parts, axis=1)
                if mode == "rope_q":
                    t = t * qk_scale
            elif mode == "scale":
                t = t * qk_scale
            elif mode == "sig":
                t = jax.nn.sigmoid(t)
            ref[:, c0:c0 + 2 * LANES] = t.astype(ref.dtype)
        col += width


def _in_proj(x2, mod3, gain, positions, w_in_bf16, seq):
    t, d = x2.shape
    tm = ROW_BLOCK
    nb = t // tm
    per_seq = seq // tm
    pos3 = positions.reshape(nb, 1, tm)
    widths = (DIL_W, DIL_W, DIL_W, SB_W, SB_W, SB_W, d, d)
    row = lambda i: (i, 0)
    return pl.pallas_call(
        _inproj_kernel,
        grid=(nb,),
        in_specs=[pl.BlockSpec((tm, d), row),
                  pl.BlockSpec((1, 6, d), lambda i: (i // per_seq, 0, 0)),
                  pl.BlockSpec((1, d), lambda i: (0, 0)),
                  pl.BlockSpec((1, 1, tm), lambda i: (i, 0, 0)),
                  pl.BlockSpec((8, LANES), lambda i: (0, 0)),
                  pl.BlockSpec(w_in_bf16.shape, lambda i: (0, 0))],
        out_specs=[pl.BlockSpec((tm, w), row) for w in widths],
        out_shape=[jax.ShapeDtypeStruct((t, w), BF16) for w in widths],
        compiler_params=_cparams(("parallel",)),
        name="in_proj",
    )(x2, mod3, gain.reshape(1, d), pos3, _rope_table(), w_in_bf16)


def _dilated_kernel(q_ref, kp_ref, kc_ref, vp_ref, vc_ref, o_ref, l_ref,
                    qf_ref, kf_ref, vf_ref, of_ref, lf_ref, *, r, nq):
    n = pl.program_id(1)
    prev_rows = Q_BLOCK * r
    halves = DIL_OUT // LANES
    for c in range(halves):
        cols = slice(c * LANES, (c + 1) * LANES)
        qf_ref[c] = q_ref[:, cols].astype(F32)
        kf_ref[c, 0:prev_rows, :] = kp_ref[:, cols].astype(F32)
        kf_ref[c, prev_rows:, :] = kc_ref[:, cols].astype(F32)
        vf_ref[c, 0:prev_rows, :] = vp_ref[:, cols].astype(F32)
        vf_ref[c, prev_rows:, :] = vc_ref[:, cols].astype(F32)

    def gather_rows(ref, start, size):
        idx = pl.ds(start, size, stride=r) if r > 1 else pl.ds(start, size)
        return jnp.concatenate([ref[c, idx, :] for c in range(halves)], axis=1).astype(BF16)

    def scatter_rows(ref, start, val):
        idx = pl.ds(start, Q_BLOCK, stride=r) if r > 1 else pl.ds(start, Q_BLOCK)
        for c in range(halves):
            ref[c, idx, :] = val[:, c * LANES:(c + 1) * LANES]

    qi = lax.broadcasted_iota(I32, (Q_BLOCK, 2 * Q_BLOCK), 0) + Q_BLOCK
    ki = lax.broadcasted_iota(I32, (Q_BLOCK, 2 * Q_BLOCK), 1)
    dist = qi - ki
    band = (dist >= 0) & (dist <= Q_BLOCK)
    lane = lax.broadcasted_iota(I32, (1, DIL_OUT), 1)

    heads = [(lane >= h * HEAD_DIM) & (lane < (h + 1) * HEAD_DIM) for h in range(DIL_HEADS_PER_GROUP)]
    ones = jnp.ones((2 * Q_BLOCK, LANES), BF16)

    def sub_block(idx, carry):
        rho = idx % r
        qb = idx // r
        base = r * Q_BLOCK * qb + rho
        q = gather_rows(qf_ref, base, Q_BLOCK)
        k = gather_rows(kf_ref, base, 2 * Q_BLOCK)
        v = gather_rows(vf_ref, base, 2 * Q_BLOCK)
        valid = band & ((n * nq + qb - 1) * Q_BLOCK + ki >= 0)
        qs = jnp.concatenate([jnp.where(heads[h], q, jnp.zeros_like(q)) for h in range(DIL_HEADS_PER_GROUP)], axis=0)
        s = lax.dot_general(qs, k, (((1,), (1,)), ((), ())), preferred_element_type=F32)
        s = jnp.where(jnp.concatenate([valid] * DIL_HEADS_PER_GROUP, axis=0), s, NEG_INF)
        m = jnp.max(s, axis=1, keepdims=True)
        p = jnp.exp(s - m).astype(BF16)
        pv = jnp.dot(p, v, preferred_element_type=F32)
        den = jnp.dot(p, ones, preferred_element_type=F32)
        inv = 1.0 / den
        lse = m + jnp.log(den)
        o = jnp.zeros((Q_BLOCK, DIL_OUT), F32)
        l = jnp.zeros((Q_BLOCK, DIL_OUT), F32)
        for h in range(DIL_HEADS_PER_GROUP):
            blk = slice(h * Q_BLOCK, (h + 1) * Q_BLOCK)
            o = jnp.where(heads[h], pv[blk] * jnp.concatenate([inv[blk]] * halves, axis=1), o)
            l = jnp.where(heads[h], jnp.concatenate([lse[blk]] * halves, axis=1), l)
        scatter_rows(of_ref, base, o)
        scatter_rows(lf_ref, base, l)
        return carry

    lax.fori_loop(0, r * nq, sub_block, 0, unroll=8)
    for c in range(halves):
        o_ref[:, c * LANES:(c + 1) * LANES] = of_ref[c]
        l_ref[:, c * LANES:(c + 1) * LANES] = lf_ref[c]


def _dilated_group(qa, ka, va, batch, seq, group):
    _, r = DIL_GROUPS[group]
    nq = max(1, DIL_TOKENS // (Q_BLOCK * r))
    prev_rows = Q_BLOCK * r
    cur_rows = prev_rows * nq
    halves = DIL_OUT // LANES
    view = lambda a: a.reshape(batch, seq, DIL_W)
    cur = lambda b, n: (b, n, group)
    prev = lambda b, n: (b, jnp.maximum(n * nq - 1, 0), group)
    cur_spec = pl.BlockSpec((None, cur_rows, DIL_OUT), cur)
    prev_spec = pl.BlockSpec((None, prev_rows, DIL_OUT), prev)
    out_spec = pl.BlockSpec((None, cur_rows, DIL_OUT), lambda b, n: (b, n, 0))
    o, l = pl.pallas_call(
        functools.partial(_dilated_kernel, r=r, nq=nq),
        grid=(batch, seq // cur_rows),
        in_specs=[cur_spec, prev_spec, cur_spec, prev_spec, cur_spec],
        out_specs=[out_spec, out_spec],
        out_shape=[jax.ShapeDtypeStruct((batch, seq, DIL_OUT), F32)] * 2,
        scratch_shapes=[pltpu.VMEM((halves, cur_rows, LANES), F32),
                        pltpu.VMEM((halves, prev_rows + cur_rows, LANES), F32),
                        pltpu.VMEM((halves, prev_rows + cur_rows, LANES), F32),
                        pltpu.VMEM((halves, cur_rows, LANES), F32),
                        pltpu.VMEM((halves, cur_rows, LANES), F32)],
        compiler_params=_cparams(("parallel", "arbitrary")),
        name=f"dilated_g{group}",
    )(view(qa), view(ka), view(ka), view(va), view(va))
    return o.reshape(batch * seq, DIL_OUT), l.reshape(batch * seq, DIL_OUT)


def _stickbreak_kernel(q_ref, k_ref, v_ref, o_ref, acc_ref, csum_ref, terms_ref, lb_ref):
    step = pl.program_id(2)
    tq = SB_TQ
    lane = lax.broadcasted_iota(I32, (1, LANES), 1)
    row = lax.broadcasted_iota(I32, (2 * tq, tq), 0)
    causal = lax.broadcasted_iota(I32, (2 * tq, tq), 1) < jnp.where(row >= tq, row - tq, row)
    later = (lax.broadcasted_iota(I32, (tq, tq), 0) > lax.broadcasted_iota(I32, (tq, tq), 1))
    later = jnp.where(later, 1.0, 0.0).astype(BF16)
    later2 = jnp.concatenate([later, later], axis=0)

    def stacked_queries(u):
        q = q_ref[u * tq:(u + 1) * tq, :]
        zero = jnp.zeros_like(q)
        return jnp.concatenate([jnp.where(lane < HEAD_DIM, q, zero), jnp.where(lane >= HEAD_DIM, q, zero)], axis=0)

    def emit(u, acc):
        o_ref[u * tq:(u + 1) * tq, :] = jnp.where(lane < HEAD_DIM, acc[:tq], acc[tq:]).astype(o_ref.dtype)


    def raw_scores(qs, j):
        start = pl.multiple_of(j * tq, tq)
        k = k_ref[pl.ds(start, tq), :]
        return lax.dot_general(qs, k, (((1,), (1,)), ((), ())), preferred_element_type=F32)

    def log_terms(z, diagonal):
        sp = jnp.maximum(z, 0.0) + jnp.log(1.0 + jnp.exp(-jnp.abs(z)))
        log_not = jnp.where(causal, -sp, 0.0) if diagonal else -sp
        log_beta = jnp.where(causal, z - sp, NEG_INF) if diagonal else z - sp
        hi = log_not.astype(BF16)
        terms_ref[:, :tq] = hi
        terms_ref[:, tq:] = (log_not - hi.astype(F32)).astype(BF16)
        csum = csum_ref[...]
        lb_ref[...] = log_beta + csum
        csum = csum + jnp.sum(log_not, axis=1, keepdims=True)
        csum_ref[...] = csum
        return jnp.max(csum)

    def later_sums():
        return jnp.dot(terms_ref[...], later2, preferred_element_type=F32)

    def weighted_values(j, inner):
        start = pl.multiple_of(j * tq, tq)
        v = v_ref[pl.ds(start, tq), :]
        w = jnp.exp(lb_ref[...] + inner)
        return jnp.dot(w.astype(BF16), v, preferred_element_type=F32)

    def cond(state):
        j, top = state
        return (j >= 0) & (top > SB_LOG_FLOOR)

    pending = None
    for u in range(SB_SUB):
        i = step * SB_SUB + u
        qs = stacked_queries(u)
        if pending is None:
            z = raw_scores(qs, i)
            pv = None
        else:
            inner = later_sums()
            z = raw_scores(qs, i)
            pv = weighted_values(pending[1] + 1, inner)
        csum_ref[...] = jnp.zeros_like(csum_ref)
        acc_ref[u] = jnp.zeros(acc_ref.shape[1:], F32)
        top = log_terms(z, True)
        if pending is not None:
            emit(pending[0], acc_ref[pending[0]] + pv)

        def body(state, qs=qs, u=u):
            j, _ = state
            inner = later_sums()
            z = raw_scores(qs, j)
            pv = weighted_values(j + 1, inner)
            top = log_terms(z, False)
            acc_ref[u] += pv
            return j - 1, top

        last, _ = lax.while_loop(cond, body, (i - 1, top))
        pending = (u, last)
    emit(pending[0], acc_ref[pending[0]] + weighted_values(pending[1] + 1, later_sums()))


def _stickbreak(qb, kb, vb, batch, seq):
    view = lambda a: a.reshape(batch, seq, SB_W)
    rows = SB_TQ * SB_SUB
    pairs = SB_W // LANES
    kv_spec = pl.BlockSpec((None, seq, LANES), lambda b, hp, i: (b, 0, hp))
    q_spec = pl.BlockSpec((None, rows, LANES), lambda b, hp, i: (b, i, hp))
    out = pl.pallas_call(
        _stickbreak_kernel,
        grid=(batch, pairs, seq // rows),
        in_specs=[q_spec, kv_spec, kv_spec],
        out_specs=q_spec,
        out_shape=jax.ShapeDtypeStruct((batch, seq, SB_W), BF16),
        scratch_shapes=[pltpu.VMEM((SB_SUB, 2 * SB_TQ, LANES), F32), pltpu.VMEM((2 * SB_TQ, 1), F32),
                        pltpu.VMEM((2 * SB_TQ, 2 * SB_TQ), BF16), pltpu.VMEM((2 * SB_TQ, SB_TQ), F32)],
        compiler_params=_cparams(("parallel", "parallel", "arbitrary")),
        name="stickbreak",
    )(view(qb), view(kb), view(vb))
    return out.reshape(batch * seq, SB_W)


def _postmix_kernel(o0_ref, o1_ref, o2_ref, l0_ref, l1_ref, l2_ref, sb_ref, ga_ref, gb_ref, x_ref,
                    mod_ref, gpost_ref, gffn_ref, wa_ref, wb_ref, wo_ref, wr_ref, br_ref,
                    x1_ref, h2_ref, idx_ref, gate_ref, rank_ref, cnt_ref, carry_ref):
    step = pl.program_id(0)
    tm = x_ref.shape[0]

    @pl.when(step == 0)
    def _():
        carry_ref[...] = jnp.zeros_like(carry_ref)

    l0, l1, l2 = l0_ref[...], l1_ref[...], l2_ref[...]
    m = jnp.maximum(jnp.maximum(l0, l1), l2)
    e0, e1, e2 = jnp.exp(l0 - m), jnp.exp(l1 - m), jnp.exp(l2 - m)
    merged = (e0 * o0_ref[...] + e1 * o1_ref[...] + e2 * o2_ref[...]) / (e0 + e1 + e2)
    ya = jnp.dot(merged.astype(BF16), wa_ref[...], preferred_element_type=F32)
    yb = jnp.dot(sb_ref[...], wb_ref[...], preferred_element_type=F32)
    mix = ga_ref[...] * ya.astype(BF16) + gb_ref[...] * yb.astype(BF16)
    y = jnp.dot(mix, wo_ref[...], preferred_element_type=F32)
    x1 = x_ref[...] + _rms(y, gpost_ref[...] * mod_ref[0, 2:3, :])
    x1_ref[...] = x1
    h2 = _rms(x1, gffn_ref[...] * (1.0 + mod_ref[0, 4:5, :])) + mod_ref[0, 3:4, :]
    h2_ref[...] = _pack_bf16_pairs(h2)

    def split(a):
        hi = a.astype(BF16)
        return hi, (a - hi.astype(F32)).astype(BF16)

    w_hi, w_lo = split(wr_ref[...])
    h_hi, h_lo = split(h2)
    logits = lax.dot_general(jnp.concatenate([w_hi, w_hi, w_lo], axis=1), jnp.concatenate([h_hi, h_lo, h_hi], axis=1),
                             (((1,), (1,)), ((), ())), preferred_element_type=F32) + br_ref[...]
    e_iota = lax.broadcasted_iota(I32, (N_EXPERTS, tm), 0)
    before = (lax.broadcasted_iota(I32, (tm, tm), 0) < lax.broadcasted_iota(I32, (tm, tm), 1))
    before = jnp.where(before, 1.0, 0.0).astype(BF16)

    picks, vals = [], []
    work = logits
    chosen = jnp.zeros((N_EXPERTS, tm), F32)
    for _ in range(TOP_K):
        top = jnp.max(work, axis=0, keepdims=True)
        idx = jnp.min(jnp.where(work == top, e_iota, N_EXPERTS), axis=0, keepdims=True)
        one = e_iota == idx
        picks.append((idx, one))
        vals.append(top)
        chosen = jnp.where(one, 1.0, chosen)
        work = jnp.where(one, -jnp.inf, work)
    exps = [jnp.exp(v - vals[0]) for v in vals]
    den = exps[0] + exps[1] + exps[2] + exps[3]

    rank = jnp.dot(chosen.astype(BF16), before, preferred_element_type=F32) + carry_ref[...]
    carry_ref[...] += jnp.sum(chosen, axis=1, keepdims=True)
    cnt_ref[...] = carry_ref[...].astype(I32)
    for kk, (idx, one) in enumerate(picks):
        idx_ref[kk:kk + 1, :] = idx
        gate_ref[kk:kk + 1, :] = exps[kk] / den
        rank_ref[kk:kk + 1, :] = jnp.sum(jnp.where(one, rank, 0.0), axis=0, keepdims=True).astype(I32)


def _post_mix(dil, sb, ga, gb, x2, mod3, gpost, gffn, wa, wb, wo, w_router, b_router, seq, part):
    t_all, d = x2.shape
    t = t_all // MOE_PARTS
    tm = ROW_BLOCK
    nb = t // tm
    off = part * nb
    per_seq = seq // tm
    row_in = lambda i: (i + off, 0)
    row = lambda i: (i, 0)
    const = lambda i: (0, 0)
    colblk = lambda i: (0, i)
    (o0, l0), (o1, l1), (o2, l2) = dil
    in_specs = ([pl.BlockSpec((tm, DIL_OUT), row_in)] * 6
                + [pl.BlockSpec((tm, SB_W), row_in), pl.BlockSpec((tm, d), row_in), pl.BlockSpec((tm, d), row_in),
                   pl.BlockSpec((tm, d), row_in),
                   pl.BlockSpec((1, 6, d), lambda i: ((i + off) // per_seq, 0, 0)),
                   pl.BlockSpec((1, d), const), pl.BlockSpec((1, d), const),
                   pl.BlockSpec(wa.shape, const), pl.BlockSpec(wb.shape, const), pl.BlockSpec(wo.shape, const),
                   pl.BlockSpec((N_EXPERTS, d), const), pl.BlockSpec((N_EXPERTS, 1), const)])
    out_specs = [pl.BlockSpec((tm, d), row), pl.BlockSpec((tm, d // 2), row),
                 pl.BlockSpec((TOP_K, tm), colblk), pl.BlockSpec((TOP_K, tm), colblk),
                 pl.BlockSpec((TOP_K, tm), colblk), pl.BlockSpec((N_EXPERTS, 1), const)]
    out_shape = [jax.ShapeDtypeStruct((t, d), F32), jax.ShapeDtypeStruct((t, d // 2), U32),
                 jax.ShapeDtypeStruct((TOP_K, t), I32), jax.ShapeDtypeStruct((TOP_K, t), F32),
                 jax.ShapeDtypeStruct((TOP_K, t), I32), jax.ShapeDtypeStruct((N_EXPERTS, 1), I32)]
    return pl.pallas_call(
        _postmix_kernel,
        grid=(nb,),
        in_specs=in_specs,
        out_specs=out_specs,
        out_shape=out_shape,
        scratch_shapes=[pltpu.VMEM((N_EXPERTS, 1), F32)],
        compiler_params=_cparams(("arbitrary",)),
        name="post_mix",
    )(o0, o1, o2, l0, l1, l2, sb, ga, gb, x2, mod3, gpost.reshape(1, d), gffn.reshape(1, d),
      wa, wb, wo, w_router.T, b_router.reshape(N_EXPERTS, 1))


def _slots_kernel(start_ref, idx_ref, rank_ref, slot_ref):
    idx = idx_ref[...]
    base = jnp.zeros_like(idx)
    for e in range(N_EXPERTS):
        base = jnp.where(idx == e, start_ref[e], base)
    slot_ref[...] = base + rank_ref[...]


def _slots(pad_start, idx, rank):
    k, t = idx.shape
    tb = min(t, 8192)
    blk = pl.BlockSpec((k, tb), lambda i, s: (0, i))
    return pl.pallas_call(
        _slots_kernel,
        grid_spec=pltpu.PrefetchScalarGridSpec(
            num_scalar_prefetch=1, grid=(t // tb,), in_specs=[blk, blk], out_specs=blk),
        out_shape=jax.ShapeDtypeStruct((k, t), I32),
        compiler_params=_cparams(("parallel",)),
        name="slots",
    )(pad_start, idx, rank)


def _sc_mesh():
    return plsc.VectorSubcoreMesh(core_axis_name="c", subcore_axis_name="s")


def _sc_dispatch(rows, slot, n_slots):
    t, d = rows.shape
    chunk = SC_DISPATCH_CHUNK
    per_worker = t // SC_WORKERS
    n_chunks = per_worker // chunk

    @functools.partial(
        pl.kernel, mesh=_sc_mesh(),
        out_type=jax.ShapeDtypeStruct((n_slots, d), rows.dtype),
        scratch_types=[pltpu.VMEM((TOP_K, chunk), I32), pltpu.VMEM((chunk, d), rows.dtype),
                       pltpu.SemaphoreType.DMA],
        name="dispatch",
    )
    def run(rows_hbm, slot_hbm, out_hbm, idx_v, rows_v, sem):
        wid = lax.axis_index("s") * SC_CORES + lax.axis_index("c")

        @pl.loop(0, n_chunks)
        def _(ci):
            base = pl.multiple_of(wid * per_worker + ci * chunk, chunk)
            loads = [pltpu.make_async_copy(rows_hbm.at[pl.ds(base, chunk)], rows_v, sem)]
            loads += [pltpu.make_async_copy(slot_hbm.at[kk, pl.ds(base, chunk)], idx_v.at[kk], sem)
                      for kk in range(TOP_K)]
            for cp in loads:
                cp.start()
            for cp in loads:
                cp.wait()
            copies = [pltpu.make_async_copy(rows_v, out_hbm.at[idx_v.at[kk]], sem) for kk in range(TOP_K)]
            for cp in copies:
                cp.start()
            for cp in copies:
                cp.wait()

    return run(rows, slot)


def _sc_combine(ys, slot):
    _, d = ys.shape
    k, t = slot.shape
    chunk = SC_COMBINE_CHUNK
    per_worker = t // SC_WORKERS
    n_chunks = per_worker // chunk

    @functools.partial(
        pl.kernel, mesh=_sc_mesh(),
        out_type=jax.ShapeDtypeStruct((k, t, d), ys.dtype),
        scratch_types=[pltpu.VMEM((TOP_K, chunk), I32), pltpu.VMEM((TOP_K, chunk, d), ys.dtype),
                       pltpu.SemaphoreType.DMA],
        name="combine",
    )
    def run(ys_hbm, slot_hbm, out_hbm, idx_v, rows_v, sem):
        wid = lax.axis_index("s") * SC_CORES + lax.axis_index("c")

        @pl.loop(0, n_chunks)
        def _(ci):
            base = pl.multiple_of(wid * per_worker + ci * chunk, chunk)
            loads = [pltpu.make_async_copy(slot_hbm.at[kk, pl.ds(base, chunk)], idx_v.at[kk], sem)
                     for kk in range(TOP_K)]
            for cp in loads:
                cp.start()
            for cp in loads:
                cp.wait()
            gathers = [pltpu.make_async_copy(ys_hbm.at[idx_v.at[kk]], rows_v.at[kk], sem) for kk in range(TOP_K)]
            for cp in gathers:
                cp.start()
            for cp in gathers:
                cp.wait()
            stores = [pltpu.make_async_copy(rows_v.at[kk], out_hbm.at[kk, pl.ds(base, chunk)], sem)
                      for kk in range(TOP_K)]
            for cp in stores:
                cp.start()
            for cp in stores:
                cp.wait()

    return run(ys, slot)


def _experts_kernel(be_ref, nused_ref, x_ref, wu_ref, bg_ref, bl_ref, wd_ref, bd_ref, y_ref,
                    glu_w_ref, lin_w_ref, down_w_ref):
    i = pl.program_id(0)
    live = i < nused_ref[0]
    new_expert = (i == 0) | (be_ref[i] != be_ref[jnp.maximum(i - 1, 0)])

    @pl.when(live & new_expert)
    def _():
        tile = 2 * LANES
        src = lax.broadcasted_iota(I32, (tile, tile), 0)
        dst = lax.broadcasted_iota(I32, (tile, tile), 1)
        wanted = jnp.where(dst < LANES, 2 * dst, 2 * (dst - LANES) + 1)
        pick = jnp.where(src == wanted, 1.0, 0.0).astype(BF16)
        for c in range(wu_ref.shape[2] // tile):
            w = wu_ref[0, :, c * tile:(c + 1) * tile].astype(BF16)
            both = jnp.dot(w, pick, preferred_element_type=F32)
            glu_w_ref[:, c * LANES:(c + 1) * LANES] = both[:, :LANES].astype(BF16)
            lin_w_ref[:, c * LANES:(c + 1) * LANES] = both[:, LANES:].astype(BF16)
        down_w_ref[...] = wd_ref[0].astype(BF16)

    @pl.when(live)
    def _():
        lo, hi = _unpack_bf16_pairs(x_ref[...])
        x = jnp.concatenate([lo, hi], axis=1).astype(BF16)
        glu = jnp.dot(x, glu_w_ref[...], preferred_element_type=F32) + bg_ref[0]
        lin = jnp.dot(x, lin_w_ref[...], preferred_element_type=F32) + bl_ref[0]
        glu = jnp.minimum(glu, SWIGLU_LIMIT)
        lin = jnp.clip(lin, -SWIGLU_LIMIT, SWIGLU_LIMIT)
        act = glu * jax.nn.sigmoid(SWIGLU_ALPHA * glu) * (lin + 1.0)
        y = jnp.dot(act.astype(BF16), down_w_ref[...], preferred_element_type=F32) + bd_ref[0]
        y_ref[...] = _pack_bf16_pairs(y)


def _experts(xs, block_expert, n_used, w_up, b_glu, b_lin, w_down, b_down):
    n_slots, half_d = xs.shape
    _, d, f2 = w_up.shape
    f = f2 // 2
    nblk = n_slots // MOE_ROWS
    rows = lambda i, be, nu: (i, 0)
    by_e = lambda i, be, nu: (be[i], 0, 0)
    return pl.pallas_call(
        _experts_kernel,
        grid_spec=pltpu.PrefetchScalarGridSpec(
            num_scalar_prefetch=2, grid=(nblk,),
            in_specs=[pl.BlockSpec((MOE_ROWS, half_d), rows),
                      pl.BlockSpec((1, d, f2), by_e),
                      pl.BlockSpec((1, 1, f), by_e), pl.BlockSpec((1, 1, f), by_e),
                      pl.BlockSpec((1, f, d), by_e), pl.BlockSpec((1, 1, d), by_e)],
            out_specs=pl.BlockSpec((MOE_ROWS, half_d), rows),
            scratch_shapes=[pltpu.VMEM((d, f), BF16), pltpu.VMEM((d, f), BF16), pltpu.VMEM((f, d), BF16)]),
        out_shape=jax.ShapeDtypeStruct((n_slots, half_d), U32),
        compiler_params=_cparams(("arbitrary",)),
        name="experts",
    )(block_expert, n_used, xs, w_up, b_glu, b_lin, w_down, b_down)


def _final_kernel(g_ref, gate_ref, x1_ref, mod_ref, gain_ref, *rest):
    o_ref = rest[-1]
    tm = x1_ref.shape[0]
    gates = gate_ref[...]
    padded = jnp.concatenate([gates, jnp.zeros((LANES - TOP_K, tm), F32)], axis=0)
    gate_cols = padded.T
    y_lo = jnp.zeros(g_ref.shape[1:], F32)
    y_hi = jnp.zeros(g_ref.shape[1:], F32)
    for kk in range(TOP_K):
        lo, hi = _unpack_bf16_pairs(g_ref[kk])
        y_lo = y_lo + lo * gate_cols[:, kk:kk + 1]
        y_hi = y_hi + hi * gate_cols[:, kk:kk + 1]
    y = jnp.concatenate([y_lo, y_hi], axis=1)
    o_ref[...] = x1_ref[...] + mod_ref[0, 5:6, :] * _rms(y, gain_ref[...])


def _final(g, gates, x1, mod3, gain, seq, part, out_so_far):
    t, d = x1.shape
    tm = ROW_BLOCK
    nb = t // tm
    off = part * nb
    per_seq = seq // tm
    row = lambda i: (i, 0)
    in_specs = [pl.BlockSpec((TOP_K, tm, d // 2), lambda i: (0, i, 0)),
                pl.BlockSpec((TOP_K, tm), lambda i: (0, i)),
                pl.BlockSpec((tm, d), row),
                pl.BlockSpec((1, 6, d), lambda i: ((i + off) // per_seq, 0, 0)),
                pl.BlockSpec((1, d), lambda i: (0, 0))]
    args = [g, gates, x1, mod3, gain.reshape(1, d)]
    aliases = {}
    if out_so_far is not None:
        in_specs.append(pl.BlockSpec(memory_space=pl.ANY))
        args.append(out_so_far)
        aliases = {len(args) - 1: 0}
    return pl.pallas_call(
        _final_kernel,
        grid=(nb,),
        in_specs=in_specs,
        out_specs=pl.BlockSpec((tm, d), lambda i: (i + off, 0)),
        out_shape=jax.ShapeDtypeStruct((t * MOE_PARTS, d), F32),
        input_output_aliases=aliases,
        compiler_params=_cparams(("parallel",)),
        name="final",
    )(*args)


def _layer(x2, c, positions, seq, ada_w, ada_b, norm_mix_pre, norm_mix_post, norm_ffn_pre, norm_ffn_post,
           w_in, w_branch_a, w_branch_b, w_out, w_router, b_router, w_up, b_up, w_down, b_down):
    t, d = x2.shape
    batch = t // seq
    mod3 = _adaln(c, ada_w, ada_b).reshape(batch, 6, d)

    qa, ka, va, qb, kb, vb, ga, gb = _in_proj(x2, mod3, norm_mix_pre, positions, w_in.astype(BF16), seq)
    dil = [_dilated_group(qa, ka, va, batch, seq, g) for g in range(len(DIL_GROUPS))]
    sb = _stickbreak(qb, kb, vb, batch, seq)
    wa, wb, wo = w_branch_a.astype(BF16), w_branch_b.astype(BF16), w_out.astype(BF16)
    f = w_up.shape[2] // 2
    b_glu, b_lin = b_up[:, 0::2].reshape(N_EXPERTS, 1, f), b_up[:, 1::2].reshape(N_EXPERTS, 1, f)
    b_down3 = b_down.reshape(N_EXPERTS, 1, d)
    nblk = (t // MOE_PARTS * TOP_K) // MOE_ROWS + N_EXPERTS

    routed = []
    for part in range(MOE_PARTS):
        x1, h2, idx, gates, rank, counts = _post_mix(
            dil, sb, ga, gb, x2, mod3, norm_mix_post, norm_ffn_pre, wa, wb, wo, w_router, b_router, seq, part)
        counts = counts.reshape(N_EXPERTS)
        padded = (counts + MOE_ROWS - 1) // MOE_ROWS * MOE_ROWS
        pad_end = jnp.cumsum(padded)
        pad_start = (pad_end - padded).astype(I32)
        block_first_row = jnp.arange(nblk, dtype=I32)[:, None] * MOE_ROWS
        block_expert = jnp.minimum(jnp.sum(pad_end[None, :] <= block_first_row, axis=1), N_EXPERTS - 1).astype(I32)
        n_used = (pad_end[-1:] // MOE_ROWS).astype(I32)
        slot = _slots(pad_start, idx, rank)
        xs = _sc_dispatch(h2, slot, nblk * MOE_ROWS)
        routed.append((x1, gates, slot, xs, block_expert, n_used))

    gathered = []
    for x1, gates, slot, xs, block_expert, n_used in routed:
        ys = _experts(xs, block_expert, n_used, w_up, b_glu, b_lin, w_down, b_down3)
        gathered.append(_sc_combine(ys, slot))

    out = None
    for part, ((x1, gates, *_), g) in enumerate(zip(routed, gathered)):
        out = _final(g, gates, x1, mod3, norm_ffn_post, seq, part, out)
    return out


def kernel(x, c, positions, ada_w, ada_b, norm_mix_pre, norm_mix_post, norm_ffn_pre, norm_ffn_post,
           w_in, w_branch_a, w_branch_b, w_out, w_router, b_router, w_up, b_up, w_down, b_down):
    batch, seq, d = x.shape
    x2 = x.reshape(batch * seq, d)
    for layer in range(ada_w.shape[0]):
        x2 = _layer(x2, c, positions, seq, ada_w[layer], ada_b[layer], norm_mix_pre[layer], norm_mix_post[layer],
                    norm_ffn_pre[layer], norm_ffn_post[layer], w_in[layer], w_branch_a[layer], w_branch_b[layer],
                    w_out[layer], w_router[layer], b_router[layer], w_up[layer], b_up[layer], w_down[layer],
                    b_down[layer])
    return x2.reshape(batch, seq, d)
```

```python
import functools

import numpy as np
import jax
import jax.numpy as jnp
from jax import lax
from jax.experimental import pallas as pl
from jax.experimental.pallas import tpu as pltpu
from jax.experimental.pallas import tpu_sc as plsc

F32 = jnp.float32
BF16 = jnp.bfloat16
I32 = jnp.int32
U32 = jnp.uint32

HEAD_DIM = 64
DIL_GROUPS = ((128, 1), (512, 4), (2048, 16))
DIL_HEADS_PER_GROUP = 4
DIL_HEADS = DIL_HEADS_PER_GROUP * len(DIL_GROUPS)
DIL_W = DIL_HEADS * HEAD_DIM
DIL_OUT = DIL_HEADS_PER_GROUP * HEAD_DIM
SB_HEADS = 8
SB_W = SB_HEADS * HEAD_DIM
ROPE_THETA = 500000.0
ROPE_DIMS = HEAD_DIM // 4
Q_BLOCK = 128
N_EXPERTS = 32
TOP_K = 4
SWIGLU_ALPHA = 1.702
SWIGLU_LIMIT = 7.0
NORM_EPS = 1e-6
NEG_INF = -1e30

LANES = 128
ROW_BLOCK = 512
MOE_ROWS = 512
MOE_PARTS = 2
DIL_TOKENS = 512
SB_TQ = 256
SB_SUB = 4
SB_LOG_FLOOR = -105.0
VMEM_LIMIT = 56 * 1024 * 1024

SC_CORES = 2
SC_SUBCORES = 16
SC_WORKERS = SC_CORES * SC_SUBCORES
SC_DISPATCH_CHUNK = 64
SC_COMBINE_CHUNK = 32


def _cparams(sem):
    return pltpu.CompilerParams(dimension_semantics=sem, vmem_limit_bytes=VMEM_LIMIT)


def _rms(x, gain):
    ms = jnp.mean(x * x, axis=-1, keepdims=True)
    return x * lax.rsqrt(ms + NORM_EPS) * gain


def _pack_bf16_pairs(x):
    n = x.shape[1] // 2
    u = lax.bitcast_convert_type(x, U32)
    r = (u + jnp.uint32(0x7FFF) + ((u >> 16) & jnp.uint32(1))) >> 16
    return r[:, :n] | (r[:, n:] << 16)


def _unpack_bf16_pairs(w):
    lo = lax.bitcast_convert_type(w << 16, F32)
    hi = lax.bitcast_convert_type(w & jnp.uint32(0xFFFF0000), F32)
    return lo, hi


def _adaln_kernel(c_ref, w_ref, b_ref, o_ref):
    c = c_ref[...]
    s = c * jax.nn.sigmoid(c)
    o_ref[...] = jnp.dot(s, w_ref[...], preferred_element_type=F32,
                         precision=lax.Precision.HIGHEST) + b_ref[...]


def _adaln(c, ada_w, ada_b):
    b, d = c.shape
    n = ada_w.shape[1]
    return pl.pallas_call(
        _adaln_kernel,
        grid=(n // d,),
        in_specs=[pl.BlockSpec((b, d), lambda j: (0, 0)),
                  pl.BlockSpec((d, d), lambda j: (0, j)),
                  pl.BlockSpec((1, d), lambda j: (0, j))],
        out_specs=pl.BlockSpec((b, d), lambda j: (0, j)),
        out_shape=jax.ShapeDtypeStruct((b, n), F32),
        compiler_params=_cparams(("arbitrary",)),
        name="adaln",
    )(c, ada_w, ada_b.reshape(1, n))


def _rope_table():
    half = ROPE_DIMS // 2
    inv_freq = ROPE_THETA ** (-(np.arange(half, dtype=np.float32) * 2.0 / ROPE_DIMS))
    return jnp.asarray(np.broadcast_to(inv_freq[:, None], (half, LANES)).astype(np.float32))


def _inproj_kernel(x_ref, mod_ref, g_ref, pos_ref, rope_ref, w_ref,
                   qa_ref, ka_ref, va_ref, qb_ref, kb_ref, vb_ref, ga_ref, gb_ref):
    x = x_ref[...]
    tm = x.shape[0]
    shift = mod_ref[0, 0:1, :]
    scale = mod_ref[0, 1:2, :]
    hb = (_rms(x, g_ref[...]) * (1.0 + scale) + shift).astype(BF16)
    qk_scale = HEAD_DIM ** -0.5
    half = ROPE_DIMS // 2

    ang = rope_ref[:, 0:1] * pos_ref[0].astype(F32)
    cos_c, sin_c = jnp.cos(ang), jnp.sin(ang)
    rest = HEAD_DIM - ROPE_DIMS
    ones, zeros = jnp.ones((rest, tm), F32), jnp.zeros((rest, tm), F32)
    cos = jnp.concatenate([cos_c, cos_c, ones] * (LANES // HEAD_DIM), axis=0).T
    sin = jnp.concatenate([-sin_c, sin_c, zeros] * (LANES // HEAD_DIM), axis=0).T
    first = (lax.broadcasted_iota(I32, (1, LANES), 1) & (HEAD_DIM - 1)) < half

    def rope(t):
        def one(tile):
            partner = jnp.where(first, pltpu.roll(tile, LANES - half, 1), pltpu.roll(tile, half, 1))
            return tile * cos + partner * sin
        return jnp.concatenate([one(t[:, :LANES]), one(t[:, LANES:])], axis=1)

    def project(ref, col, width, finish):
        for c0 in range(0, width, 2 * LANES):
            t = jnp.dot(hb, w_ref[:, col + c0:col + c0 + 2 * LANES], preferred_element_type=F32)
            ref[:, c0:c0 + 2 * LANES] = finish(t).astype(ref.dtype)

    d = x.shape[1]
    col = 0
    for ref, width, finish in ((qa_ref, DIL_W, lambda t: rope(t) * qk_scale), (ka_ref, DIL_W, rope),
                               (va_ref, DIL_W, lambda t: t), (qb_ref, SB_W, lambda t: t * qk_scale),
                               (kb_ref, SB_W, lambda t: t), (vb_ref, SB_W, lambda t: t),
                               (ga_ref, d, jax.nn.sigmoid), (gb_ref, d, jax.nn.sigmoid)):
        project(ref, col, width, finish)
        col += width


def _in_proj(x2, mod3, gain, positions, w_in_bf16, seq):
    t, d = x2.shape
    tm = ROW_BLOCK
    nb = t // tm
    per_seq = seq // tm
    pos3 = positions.reshape(nb, 1, tm)
    widths = (DIL_W, DIL_W, DIL_W, SB_W, SB_W, SB_W, d, d)
    row = lambda i: (i, 0)
    return pl.pallas_call(
        _inproj_kernel,
        grid=(nb,),
        in_specs=[pl.BlockSpec((tm, d), row),
                  pl.BlockSpec((1, 6, d), lambda i: (i // per_seq, 0, 0)),
                  pl.BlockSpec((1, d), lambda i: (0, 0)),
                  pl.BlockSpec((1, 1, tm), lambda i: (i, 0, 0)),
                  pl.BlockSpec((8, LANES), lambda i: (0, 0)),
                  pl.BlockSpec(w_in_bf16.shape, lambda i: (0, 0))],
        out_specs=[pl.BlockSpec((tm, w), row) for w in widths],
        out_shape=[jax.ShapeDtypeStruct((t, w), BF16) for w in widths],
        compiler_params=_cparams(("parallel",)),
        name="in_proj",
    )(x2, mod3, gain.reshape(1, d), pos3, _rope_table(), w_in_bf16)


def _dilated_kernel(q_ref, kp_ref, kc_ref, vp_ref, vc_ref, o_ref, l_ref,
                    qf_ref, kf_ref, vf_ref, of_ref, lf_ref, *, r, nq):
    n = pl.program_id(1)
    prev_rows = Q_BLOCK * r
    halves = DIL_OUT // LANES
    for c in range(halves):
        cols = slice(c * LANES, (c + 1) * LANES)
        qf_ref[c] = q_ref[:, cols].astype(F32)
        kf_ref[c, 0:prev_rows, :] = kp_ref[:, cols].astype(F32)
        kf_ref[c, prev_rows:, :] = kc_ref[:, cols].astype(F32)
        vf_ref[c, 0:prev_rows, :] = vp_ref[:, cols].astype(F32)
        vf_ref[c, prev_rows:, :] = vc_ref[:, cols].astype(F32)

    def gather_rows(ref, start, size):
        idx = pl.ds(start, size, stride=r) if r > 1 else pl.ds(start, size)
        return jnp.concatenate([ref[c, idx, :] for c in range(halves)], axis=1).astype(BF16)

    def scatter_rows(ref, start, val):
        idx = pl.ds(start, Q_BLOCK, stride=r) if r > 1 else pl.ds(start, Q_BLOCK)
        for c in range(halves):
            ref[c, idx, :] = val[:, c * LANES:(c + 1) * LANES]

    qi = lax.broadcasted_iota(I32, (Q_BLOCK, 2 * Q_BLOCK), 0) + Q_BLOCK
    ki = lax.broadcasted_iota(I32, (Q_BLOCK, 2 * Q_BLOCK), 1)
    dist = qi - ki
    band = (dist >= 0) & (dist <= Q_BLOCK)
    lane = lax.broadcasted_iota(I32, (1, DIL_OUT), 1)

    heads = [(lane >= h * HEAD_DIM) & (lane < (h + 1) * HEAD_DIM) for h in range(DIL_HEADS_PER_GROUP)]
    ones = jnp.ones((2 * Q_BLOCK, LANES), BF16)

    def sub_block(idx, carry):
        rho = idx % r
        qb = idx // r
        base = r * Q_BLOCK * qb + rho
        q = gather_rows(qf_ref, base, Q_BLOCK)
        k = gather_rows(kf_ref, base, 2 * Q_BLOCK)
        v = gather_rows(vf_ref, base, 2 * Q_BLOCK)
        valid = band & ((n * nq + qb - 1) * Q_BLOCK + ki >= 0)
        qs = jnp.concatenate([jnp.where(heads[h], q, jnp.zeros_like(q)) for h in range(DIL_HEADS_PER_GROUP)], axis=0)
        s = lax.dot_general(qs, k, (((1,), (1,)), ((), ())), preferred_element_type=F32)
        s = jnp.where(jnp.concatenate([valid] * DIL_HEADS_PER_GROUP, axis=0), s, NEG_INF)
        m = jnp.max(s, axis=1, keepdims=True)
        p = jnp.exp(s - m).astype(BF16)
        pv = jnp.dot(p, v, preferred_element_type=F32)
        den = jnp.dot(p, ones, preferred_element_type=F32)
        inv = 1.0 / den
        lse = m + jnp.log(den)
        o = jnp.zeros((Q_BLOCK, DIL_OUT), F32)
        l = jnp.zeros((Q_BLOCK, DIL_OUT), F32)
        for h in range(DIL_HEADS_PER_GROUP):
            blk = slice(h * Q_BLOCK, (h + 1) * Q_BLOCK)
            o = jnp.where(heads[h], pv[blk] * jnp.concatenate([inv[blk]] * halves, axis=1), o)
            l = jnp.where(heads[h], jnp.concatenate([lse[blk]] * halves, axis=1), l)
        scatter_rows(of_ref, base, o)
        scatter_rows(lf_ref, base, l)
        return carry

    lax.fori_loop(0, r * nq, sub_block, 0, unroll=8)
    for c in range(halves):
        o_ref[:, c * LANES:(c + 1) * LANES] = of_ref[c]
        l_ref[:, c * LANES:(c + 1) * LANES] = lf_ref[c]


def _dilated_group(qa, ka, va, batch, seq, group):
    _, r = DIL_GROUPS[group]
    nq = max(1, DIL_TOKENS // (Q_BLOCK * r))
    prev_rows = Q_BLOCK * r
    cur_rows = prev_rows * nq
    halves = DIL_OUT // LANES
    view = lambda a: a.reshape(batch, seq, DIL_W)
    cur = lambda b, n: (b, n, group)
    prev = lambda b, n: (b, jnp.maximum(n * nq - 1, 0), group)
    cur_spec = pl.BlockSpec((None, cur_rows, DIL_OUT), cur)
    prev_spec = pl.BlockSpec((None, prev_rows, DIL_OUT), prev)
    out_spec = pl.BlockSpec((None, cur_rows, DIL_OUT), lambda b, n: (b, n, 0))
    o, l = pl.pallas_call(
        functools.partial(_dilated_kernel, r=r, nq=nq),
        grid=(batch, seq // cur_rows),
        in_specs=[cur_spec, prev_spec, cur_spec, prev_spec, cur_spec],
        out_specs=[out_spec, out_spec],
        out_shape=[jax.ShapeDtypeStruct((batch, seq, DIL_OUT), F32)] * 2,
        scratch_shapes=[pltpu.VMEM((halves, cur_rows, LANES), F32),
                        pltpu.VMEM((halves, prev_rows + cur_rows, LANES), F32),
                        pltpu.VMEM((halves, prev_rows + cur_rows, LANES), F32),
                        pltpu.VMEM((halves, cur_rows, LANES), F32),
                        pltpu.VMEM((halves, cur_rows, LANES), F32)],
        compiler_params=_cparams(("parallel", "arbitrary")),
        name=f"dilated_g{group}",
    )(view(qa), view(ka), view(ka), view(va), view(va))
    return o.reshape(batch * seq, DIL_OUT), l.reshape(batch * seq, DIL_OUT)


def _stickbreak_kernel(q_ref, k_ref, v_ref, o_ref, acc_ref, csum_ref, terms_ref, lb_ref):
    step = pl.program_id(2)
    tq = SB_TQ
    lane = lax.broadcasted_iota(I32, (1, LANES), 1)
    row = lax.broadcasted_iota(I32, (2 * tq, tq), 0)
    causal = lax.broadcasted_iota(I32, (2 * tq, tq), 1) < jnp.where(row >= tq, row - tq, row)
    later = (lax.broadcasted_iota(I32, (tq, tq), 0) > lax.broadcasted_iota(I32, (tq, tq), 1))
    later = jnp.where(later, 1.0, 0.0).astype(BF16)
    later2 = jnp.concatenate([later, later], axis=0)

    def stacked_queries(u):
        q = q_ref[u * tq:(u + 1) * tq, :]
        zero = jnp.zeros_like(q)
        return jnp.concatenate([jnp.where(lane < HEAD_DIM, q, zero), jnp.where(lane >= HEAD_DIM, q, zero)], axis=0)

    def emit(u, acc):
        o_ref[u * tq:(u + 1) * tq, :] = jnp.where(lane < HEAD_DIM, acc[:tq], acc[tq:]).astype(o_ref.dtype)


    def raw_scores(qs, j):
        start = pl.multiple_of(j * tq, tq)
        k = k_ref[pl.ds(start, tq), :]
        return lax.dot_general(qs, k, (((1,), (1,)), ((), ())), preferred_element_type=F32)

    def log_terms(z, diagonal):
        sp = jnp.maximum(z, 0.0) + jnp.log(1.0 + jnp.exp(-jnp.abs(z)))
        log_not = jnp.where(causal, -sp, 0.0) if diagonal else -sp
        log_beta = jnp.where(causal, z - sp, NEG_INF) if diagonal else z - sp
        hi = log_not.astype(BF16)
        terms_ref[:, :tq] = hi
        terms_ref[:, tq:] = (log_not - hi.astype(F32)).astype(BF16)
        csum = csum_ref[...]
        lb_ref[...] = log_beta + csum
        csum = csum + jnp.sum(log_not, axis=1, keepdims=True)
        csum_ref[...] = csum
        return jnp.max(csum)

    def later_sums():
        return jnp.dot(terms_ref[...], later2, preferred_element_type=F32)

    def weighted_values(j, inner):
        start = pl.multiple_of(j * tq, tq)
        v = v_ref[pl.ds(start, tq), :]
        w = jnp.exp(lb_ref[...] + inner)
        return jnp.dot(w.astype(BF16), v, preferred_element_type=F32)

    def cond(state):
        j, top = state
        return (j >= 0) & (top > SB_LOG_FLOOR)

    pending = None
    for u in range(SB_SUB):
        i = step * SB_SUB + u
        qs = stacked_queries(u)
        if pending is None:
            z = raw_scores(qs, i)
            pv = None
        else:
            inner = later_sums()
            z = raw_scores(qs, i)
            pv = weighted_values(pending[1] + 1, inner)
        csum_ref[...] = jnp.zeros_like(csum_ref)
        acc_ref[u] = jnp.zeros(acc_ref.shape[1:], F32)
        top = log_terms(z, True)
        if pending is not None:
            emit(pending[0], acc_ref[pending[0]] + pv)

        def body(state, qs=qs, u=u):
            j, _ = state
            inner = later_sums()
            z = raw_scores(qs, j)
            pv = weighted_values(j + 1, inner)
            top = log_terms(z, False)
            acc_ref[u] += pv
            return j - 1, top

        last, _ = lax.while_loop(cond, body, (i - 1, top))
        pending = (u, last)
    emit(pending[0], acc_ref[pending[0]] + weighted_values(pending[1] + 1, later_sums()))


def _stickbreak(qb, kb, vb, batch, seq):
    view = lambda a: a.reshape(batch, seq, SB_W)
    rows = SB_TQ * SB_SUB
    pairs = SB_W // LANES
    kv_spec = pl.BlockSpec((None, seq, LANES), lambda b, hp, i: (b, 0, hp))
    q_spec = pl.BlockSpec((None, rows, LANES), lambda b, hp, i: (b, i, hp))
    out = pl.pallas_call(
        _stickbreak_kernel,
        grid=(batch, pairs, seq // rows),
        in_specs=[q_spec, kv_spec, kv_spec],
        out_specs=q_spec,
        out_shape=jax.ShapeDtypeStruct((batch, seq, SB_W), BF16),
        scratch_shapes=[pltpu.VMEM((SB_SUB, 2 * SB_TQ, LANES), F32), pltpu.VMEM((2 * SB_TQ, 1), F32),
                        pltpu.VMEM((2 * SB_TQ, 2 * SB_TQ), BF16), pltpu.VMEM((2 * SB_TQ, SB_TQ), F32)],
        compiler_params=_cparams(("parallel", "parallel", "arbitrary")),
        name="stickbreak",
    )(view(qb), view(kb), view(vb))
    return out.reshape(batch * seq, SB_W)


def _postmix_kernel(o0_ref, o1_ref, o2_ref, l0_ref, l1_ref, l2_ref, sb_ref, ga_ref, gb_ref, x_ref,
                    mod_ref, gpost_ref, gffn_ref, wa_ref, wb_ref, wo_ref, wr_ref, br_ref,
                    x1_ref, h2_ref, idx_ref, gate_ref, rank_ref, cnt_ref, carry_ref):
    step = pl.program_id(0)
    tm = x_ref.shape[0]

    @pl.when(step == 0)
    def _():
        carry_ref[...] = jnp.zeros_like(carry_ref)

    l0, l1, l2 = l0_ref[...], l1_ref[...], l2_ref[...]
    m = jnp.maximum(jnp.maximum(l0, l1), l2)
    e0, e1, e2 = jnp.exp(l0 - m), jnp.exp(l1 - m), jnp.exp(l2 - m)
    merged = (e0 * o0_ref[...] + e1 * o1_ref[...] + e2 * o2_ref[...]) / (e0 + e1 + e2)
    ya = jnp.dot(merged.astype(BF16), wa_ref[...], preferred_element_type=F32)
    yb = jnp.dot(sb_ref[...], wb_ref[...], preferred_element_type=F32)
    mix = ga_ref[...] * ya.astype(BF16) + gb_ref[...] * yb.astype(BF16)
    y = jnp.dot(mix, wo_ref[...], preferred_element_type=F32)
    x1 = x_ref[...] + _rms(y, gpost_ref[...] * mod_ref[0, 2:3, :])
    x1_ref[...] = x1
    h2 = _rms(x1, gffn_ref[...] * (1.0 + mod_ref[0, 4:5, :])) + mod_ref[0, 3:4, :]
    h2_ref[...] = _pack_bf16_pairs(h2)

    def split(a):
        hi = a.astype(BF16)
        return hi, (a - hi.astype(F32)).astype(BF16)

    w_hi, w_lo = split(wr_ref[...])
    h_hi, h_lo = split(h2)
    logits = lax.dot_general(jnp.concatenate([w_hi, w_hi, w_lo], axis=1), jnp.concatenate([h_hi, h_lo, h_hi], axis=1),
                             (((1,), (1,)), ((), ())), preferred_element_type=F32) + br_ref[...]
    e_iota = lax.broadcasted_iota(I32, (N_EXPERTS, tm), 0)
    before = (lax.broadcasted_iota(I32, (tm, tm), 0) < lax.broadcasted_iota(I32, (tm, tm), 1))
    before = jnp.where(before, 1.0, 0.0).astype(BF16)

    picks, vals = [], []
    work = logits
    chosen = jnp.zeros((N_EXPERTS, tm), F32)
    for _ in range(TOP_K):
        top = jnp.max(work, axis=0, keepdims=True)
        idx = jnp.min(jnp.where(work == top, e_iota, N_EXPERTS), axis=0, keepdims=True)
        one = e_iota == idx
        picks.append((idx, one))
        vals.append(top)
        chosen = jnp.where(one, 1.0, chosen)
        work = jnp.where(one, -jnp.inf, work)
    exps = [jnp.exp(v - vals[0]) for v in vals]
    den = exps[0] + exps[1] + exps[2] + exps[3]

    rank = jnp.dot(chosen.astype(BF16), before, preferred_element_type=F32) + carry_ref[...]
    carry_ref[...] += jnp.sum(chosen, axis=1, keepdims=True)
    cnt_ref[...] = carry_ref[...].astype(I32)
    for kk, (idx, one) in enumerate(picks):
        idx_ref[kk:kk + 1, :] = idx
        gate_ref[kk:kk + 1, :] = exps[kk] / den
        rank_ref[kk:kk + 1, :] = jnp.sum(jnp.where(one, rank, 0.0), axis=0, keepdims=True).astype(I32)


def _post_mix(dil, sb, ga, gb, x2, mod3, gpost, gffn, wa, wb, wo, w_router, b_router, seq, part):
    t_all, d = x2.shape
    t = t_all // MOE_PARTS
    tm = ROW_BLOCK
    nb = t // tm
    off = part * nb
    per_seq = seq // tm
    row_in = lambda i: (i + off, 0)
    row = lambda i: (i, 0)
    const = lambda i: (0, 0)
    colblk = lambda i: (0, i)
    (o0, l0), (o1, l1), (o2, l2) = dil
    in_specs = ([pl.BlockSpec((tm, DIL_OUT), row_in)] * 6
                + [pl.BlockSpec((tm, SB_W), row_in), pl.BlockSpec((tm, d), row_in), pl.BlockSpec((tm, d), row_in),
                   pl.BlockSpec((tm, d), row_in),
                   pl.BlockSpec((1, 6, d), lambda i: ((i + off) // per_seq, 0, 0)),
                   pl.BlockSpec((1, d), const), pl.BlockSpec((1, d), const),
                   pl.BlockSpec(wa.shape, const), pl.BlockSpec(wb.shape, const), pl.BlockSpec(wo.shape, const),
                   pl.BlockSpec((N_EXPERTS, d), const), pl.BlockSpec((N_EXPERTS, 1), const)])
    out_specs = [pl.BlockSpec((tm, d), row), pl.BlockSpec((tm, d // 2), row),
                 pl.BlockSpec((TOP_K, tm), colblk), pl.BlockSpec((TOP_K, tm), colblk),
                 pl.BlockSpec((TOP_K, tm), colblk), pl.BlockSpec((N_EXPERTS, 1), const)]
    out_shape = [jax.ShapeDtypeStruct((t, d), F32), jax.ShapeDtypeStruct((t, d // 2), U32),
                 jax.ShapeDtypeStruct((TOP_K, t), I32), jax.ShapeDtypeStruct((TOP_K, t), F32),
                 jax.ShapeDtypeStruct((TOP_K, t), I32), jax.ShapeDtypeStruct((N_EXPERTS, 1), I32)]
    return pl.pallas_call(
        _postmix_kernel,
        grid=(nb,),
        in_specs=in_specs,
        out_specs=out_specs,
        out_shape=out_shape,
        scratch_shapes=[pltpu.VMEM((N_EXPERTS, 1), F32)],
        compiler_params=_cparams(("arbitrary",)),
        name="post_mix",
    )(o0, o1, o2, l0, l1, l2, sb, ga, gb, x2, mod3, gpost.reshape(1, d), gffn.reshape(1, d),
      wa, wb, wo, w_router.T, b_router.reshape(N_EXPERTS, 1))


def _slots_kernel(start_ref, idx_ref, rank_ref, slot_ref):
    idx = idx_ref[...]
    base = jnp.zeros_like(idx)
    for e in range(N_EXPERTS):
        base = jnp.where(idx == e, start_ref[e], base)
    slot_ref[...] = base + rank_ref[...]


def _slots(pad_start, idx, rank):
    k, t = idx.shape
    tb = min(t, 8192)
    blk = pl.BlockSpec((k, tb), lambda i, s: (0, i))
    return pl.pallas_call(
        _slots_kernel,
        grid_spec=pltpu.PrefetchScalarGridSpec(
            num_scalar_prefetch=1, grid=(t // tb,), in_specs=[blk, blk], out_specs=blk),
        out_shape=jax.ShapeDtypeStruct((k, t), I32),
        compiler_params=_cparams(("parallel",)),
        name="slots",
    )(pad_start, idx, rank)


def _sc_mesh():
    return plsc.VectorSubcoreMesh(core_axis_name="c", subcore_axis_name="s")


def _sc_dispatch(rows, slot, n_slots):
    t, d = rows.shape
    chunk = SC_DISPATCH_CHUNK
    per_worker = t // SC_WORKERS
    n_chunks = per_worker // chunk

    @functools.partial(
        pl.kernel, mesh=_sc_mesh(),
        out_type=jax.ShapeDtypeStruct((n_slots, d), rows.dtype),
        scratch_types=[pltpu.VMEM((TOP_K, chunk), I32), pltpu.VMEM((chunk, d), rows.dtype),
                       pltpu.SemaphoreType.DMA],
        name="dispatch",
    )
    def run(rows_hbm, slot_hbm, out_hbm, idx_v, rows_v, sem):
        wid = lax.axis_index("s") * SC_CORES + lax.axis_index("c")

        @pl.loop(0, n_chunks)
        def _(ci):
            base = pl.multiple_of(wid * per_worker + ci * chunk, chunk)
            loads = [pltpu.make_async_copy(rows_hbm.at[pl.ds(base, chunk)], rows_v, sem)]
            loads += [pltpu.make_async_copy(slot_hbm.at[kk, pl.ds(base, chunk)], idx_v.at[kk], sem)
                      for kk in range(TOP_K)]
            for cp in loads:
                cp.start()
            for cp in loads:
                cp.wait()
            copies = [pltpu.make_async_copy(rows_v, out_hbm.at[idx_v.at[kk]], sem) for kk in range(TOP_K)]
            for cp in copies:
                cp.start()
            for cp in copies:
                cp.wait()

    return run(rows, slot)


def _sc_combine(ys, slot):
    _, d = ys.shape
    k, t = slot.shape
    chunk = SC_COMBINE_CHUNK
    per_worker = t // SC_WORKERS
    n_chunks = per_worker // chunk

    @functools.partial(
        pl.kernel, mesh=_sc_mesh(),
        out_type=jax.ShapeDtypeStruct((k, t, d), ys.dtype),
        scratch_types=[pltpu.VMEM((TOP_K, chunk), I32), pltpu.VMEM((TOP_K, chunk, d), ys.dtype),
                       pltpu.SemaphoreType.DMA],
        name="combine",
    )
    def run(ys_hbm, slot_hbm, out_hbm, idx_v, rows_v, sem):
        wid = lax.axis_index("s") * SC_CORES + lax.axis_index("c")

        @pl.loop(0, n_chunks)
        def _(ci):
            base = pl.multiple_of(wid * per_worker + ci * chunk, chunk)
            loads = [pltpu.make_async_copy(slot_hbm.at[kk, pl.ds(base, chunk)], idx_v.at[kk], sem)
                     for kk in range(TOP_K)]
            for cp in loads:
                cp.start()
            for cp in loads:
                cp.wait()
            gathers = [pltpu.make_async_copy(ys_hbm.at[idx_v.at[kk]], rows_v.at[kk], sem) for kk in range(TOP_K)]
            for cp in gathers:
                cp.start()
            for cp in gathers:
                cp.wait()
            stores = [pltpu.make_async_copy(rows_v.at[kk], out_hbm.at[kk, pl.ds(base, chunk)], sem)
                      for kk in range(TOP_K)]
            for cp in stores:
                cp.start()
            for cp in stores:
                cp.wait()

    return run(ys, slot)


def _experts_kernel(be_ref, nused_ref, x_ref, wu_ref, bg_ref, bl_ref, wd_ref, bd_ref, y_ref,
                    glu_w_ref, lin_w_ref, down_w_ref):
    i = pl.program_id(0)
    live = i < nused_ref[0]
    new_expert = (i == 0) | (be_ref[i] != be_ref[jnp.maximum(i - 1, 0)])

    @pl.when(live & new_expert)
    def _():
        tile = 2 * LANES
        src = lax.broadcasted_iota(I32, (tile, tile), 0)
        dst = lax.broadcasted_iota(I32, (tile, tile), 1)
        wanted = jnp.where(dst < LANES, 2 * dst, 2 * (dst - LANES) + 1)
        pick = jnp.where(src == wanted, 1.0, 0.0).astype(BF16)
        for c in range(wu_ref.shape[2] // tile):
            w = wu_ref[0, :, c * tile:(c + 1) * tile].astype(BF16)
            both = jnp.dot(w, pick, preferred_element_type=F32)
            glu_w_ref[:, c * LANES:(c + 1) * LANES] = both[:, :LANES].astype(BF16)
            lin_w_ref[:, c * LANES:(c + 1) * LANES] = both[:, LANES:].astype(BF16)
        down_w_ref[...] = wd_ref[0].astype(BF16)

    @pl.when(live)
    def _():
        lo, hi = _unpack_bf16_pairs(x_ref[...])
        x = jnp.concatenate([lo, hi], axis=1).astype(BF16)
        glu = jnp.dot(x, glu_w_ref[...], preferred_element_type=F32) + bg_ref[0]
        lin = jnp.dot(x, lin_w_ref[...], preferred_element_type=F32) + bl_ref[0]
        glu = jnp.minimum(glu, SWIGLU_LIMIT)
        lin = jnp.clip(lin, -SWIGLU_LIMIT, SWIGLU_LIMIT)
        act = glu * jax.nn.sigmoid(SWIGLU_ALPHA * glu) * (lin + 1.0)
        y = jnp.dot(act.astype(BF16), down_w_ref[...], preferred_element_type=F32) + bd_ref[0]
        y_ref[...] = _pack_bf16_pairs(y)


def _experts(xs, block_expert, n_used, w_up, b_glu, b_lin, w_down, b_down):
    n_slots, half_d = xs.shape
    _, d, f2 = w_up.shape
    f = f2 // 2
    nblk = n_slots // MOE_ROWS
    rows = lambda i, be, nu: (i, 0)
    by_e = lambda i, be, nu: (be[i], 0, 0)
    return pl.pallas_call(
        _experts_kernel,
        grid_spec=pltpu.PrefetchScalarGridSpec(
            num_scalar_prefetch=2, grid=(nblk,),
            in_specs=[pl.BlockSpec((MOE_ROWS, half_d), rows),
                      pl.BlockSpec((1, d, f2), by_e),
                      pl.BlockSpec((1, 1, f), by_e), pl.BlockSpec((1, 1, f), by_e),
                      pl.BlockSpec((1, f, d), by_e), pl.BlockSpec((1, 1, d), by_e)],
            out_specs=pl.BlockSpec((MOE_ROWS, half_d), rows),
            scratch_shapes=[pltpu.VMEM((d, f), BF16), pltpu.VMEM((d, f), BF16), pltpu.VMEM((f, d), BF16)]),
        out_shape=jax.ShapeDtypeStruct((n_slots, half_d), U32),
        compiler_params=_cparams(("arbitrary",)),
        name="experts",
    )(block_expert, n_used, xs, w_up, b_glu, b_lin, w_down, b_down)


def _final_kernel(g_ref, gate_ref, x1_ref, mod_ref, gain_ref, *rest):
    o_ref = rest[-1]
    tm = x1_ref.shape[0]
    gates = gate_ref[...]
    padded = jnp.concatenate([gates, jnp.zeros((LANES - TOP_K, tm), F32)], axis=0)
    gate_cols = padded.T
    y_lo = jnp.zeros(g_ref.shape[1:], F32)
    y_hi = jnp.zeros(g_ref.shape[1:], F32)
    for kk in range(TOP_K):
        lo, hi = _unpack_bf16_pairs(g_ref[kk])
        y_lo = y_lo + lo * gate_cols[:, kk:kk + 1]
        y_hi = y_hi + hi * gate_cols[:, kk:kk + 1]
    y = jnp.concatenate([y_lo, y_hi], axis=1)
    o_ref[...] = x1_ref[...] + mod_ref[0, 5:6, :] * _rms(y, gain_ref[...])


def _final(g, gates, x1, mod3, gain, seq, part, out_so_far):
    t, d = x1.shape
    tm = ROW_BLOCK
    nb = t // tm
    off = part * nb
    per_seq = seq // tm
    row = lambda i: (i, 0)
    in_specs = [pl.BlockSpec((TOP_K, tm, d // 2), lambda i: (0, i, 0)),
                pl.BlockSpec((TOP_K, tm), lambda i: (0, i)),
                pl.BlockSpec((tm, d), row),
                pl.BlockSpec((1, 6, d), lambda i: ((i + off) // per_seq, 0, 0)),
                pl.BlockSpec((1, d), lambda i: (0, 0))]
    args = [g, gates, x1, mod3, gain.reshape(1, d)]
    aliases = {}
    if out_so_far is not None:
        in_specs.append(pl.BlockSpec(memory_space=pl.ANY))
        args.append(out_so_far)
        aliases = {len(args) - 1: 0}
    return pl.pallas_call(
        _final_kernel,
        grid=(nb,),
        in_specs=in_specs,
        out_specs=pl.BlockSpec((tm, d), lambda i: (i + off, 0)),
        out_shape=jax.ShapeDtypeStruct((t * MOE_PARTS, d), F32),
        input_output_aliases=aliases,
        compiler_params=_cparams(("parallel",)),
        name="final",
    )(*args)


def _layer(x2, c, positions, seq, ada_w, ada_b, norm_mix_pre, norm_mix_post, norm_ffn_pre, norm_ffn_post,
           w_in, w_branch_a, w_branch_b, w_out, w_router, b_router, w_up, b_up, w_down, b_down):
    t, d = x2.shape
    batch = t // seq
    mod3 = _adaln(c, ada_w, ada_b).reshape(batch, 6, d)

    qa, ka, va, qb, kb, vb, ga, gb = _in_proj(x2, mod3, norm_mix_pre, positions, w_in.astype(BF16), seq)
    dil = [_dilated_group(qa, ka, va, batch, seq, g) for g in range(len(DIL_GROUPS))]
    sb = _stickbreak(qb, kb, vb, batch, seq)
    wa, wb, wo = w_branch_a.astype(BF16), w_branch_b.astype(BF16), w_out.astype(BF16)
    f = w_up.shape[2] // 2
    b_glu, b_lin = b_up[:, 0::2].reshape(N_EXPERTS, 1, f), b_up[:, 1::2].reshape(N_EXPERTS, 1, f)
    b_down3 = b_down.reshape(N_EXPERTS, 1, d)
    nblk = (t // MOE_PARTS * TOP_K) // MOE_ROWS + N_EXPERTS

    routed = []
    for part in range(MOE_PARTS):
        x1, h2, idx, gates, rank, counts = _post_mix(
            dil, sb, ga, gb, x2, mod3, norm_mix_post, norm_ffn_pre, wa, wb, wo, w_router, b_router, seq, part)
        counts = counts.reshape(N_EXPERTS)
        padded = (counts + MOE_ROWS - 1) // MOE_ROWS * MOE_ROWS
        pad_end = jnp.cumsum(padded)
        pad_start = (pad_end - padded).astype(I32)
        block_first_row = jnp.arange(nblk, dtype=I32)[:, None] * MOE_ROWS
        block_expert = jnp.minimum(jnp.sum(pad_end[None, :] <= block_first_row, axis=1), N_EXPERTS - 1).astype(I32)
        n_used = (pad_end[-1:] // MOE_ROWS).astype(I32)
        slot = _slots(pad_start, idx, rank)
        xs = _sc_dispatch(h2, slot, nblk * MOE_ROWS)
        routed.append((x1, gates, slot, xs, block_expert, n_used))

    gathered = []
    for x1, gates, slot, xs, block_expert, n_used in routed:
        ys = _experts(xs, block_expert, n_used, w_up, b_glu, b_lin, w_down, b_down3)
        gathered.append(_sc_combine(ys, slot))

    out = None
    for part, ((x1, gates, *_), g) in enumerate(zip(routed, gathered)):
        out = _final(g, gates, x1, mod3, norm_ffn_post, seq, part, out)
    return out


def kernel(x, c, positions, ada_w, ada_b, norm_mix_pre, norm_mix_post, norm_ffn_pre, norm_ffn_post,
           w_in, w_branch_a, w_branch_b, w_out, w_router, b_router, w_up, b_up, w_down, b_down):
    batch, seq, d = x.shape
    x2 = x.reshape(batch * seq, d)
    for layer in range(ada_w.shape[0]):
        x2 = _layer(x2, c, positions, seq, ada_w[layer], ada_b[layer], norm_mix_pre[layer], norm_mix_post[layer],
                    norm_ffn_pre[layer], norm_ffn_post[layer], w_in[layer], w_branch_a[layer], w_branch_b[layer],
                    w_out[layer], w_router[layer], b_router[layer], w_up[layer], b_up[layer], w_down[layer],
                    b_down[layer])
    return x2.reshape(batch, seq, d)
```

```python
import functools

import numpy as np
import jax
import jax.numpy as jnp
from jax import lax
from jax.experimental import pallas as pl
from jax.experimental.pallas import tpu as pltpu
from jax.experimental.pallas import tpu_sc as plsc

F32 = jnp.float32
BF16 = jnp.bfloat16
I32 = jnp.int32
U32 = jnp.uint32

HEAD_DIM = 64
DIL_GROUPS = ((128, 1), (512, 4), (2048, 16))
DIL_HEADS_PER_GROUP = 4
DIL_HEADS = DIL_HEADS_PER_GROUP * len(DIL_GROUPS)
DIL_W = DIL_HEADS * HEAD_DIM
DIL_OUT = DIL_HEADS_PER_GROUP * HEAD_DIM
SB_HEADS = 8
SB_W = SB_HEADS * HEAD_DIM
ROPE_THETA = 500000.0
ROPE_DIMS = HEAD_DIM // 4
Q_BLOCK = 128
N_EXPERTS = 32
TOP_K = 4
SWIGLU_ALPHA = 1.702
SWIGLU_LIMIT = 7.0
NORM_EPS = 1e-6
NEG_INF = -1e30

LANES = 128
ROW_BLOCK = 512
MOE_ROWS = 512
MOE_PARTS = 2
DIL_TOKENS = 1024
SB_TQ = 256
SB_SUB = 8
SB_LOG_FLOOR = -105.0
V7X_VMEM_BYTES = 64 * 1024 * 1024
VMEM_LIMIT = V7X_VMEM_BYTES * 7 // 8

SC_CORES = 2
SC_SUBCORES = 16
SC_WORKERS = SC_CORES * SC_SUBCORES
SC_DISPATCH_CHUNK = 64
SC_COMBINE_CHUNK = 32


def _cparams(sem):
    return pltpu.CompilerParams(dimension_semantics=sem, vmem_limit_bytes=VMEM_LIMIT)


def _rms(x, gain):
    ms = jnp.mean(x * x, axis=-1, keepdims=True)
    return x * lax.rsqrt(ms + NORM_EPS) * gain


def _pack_bf16_pairs(x):
    n = x.shape[1] // 2
    u = lax.bitcast_convert_type(x, U32)
    r = (u + jnp.uint32(0x7FFF) + ((u >> 16) & jnp.uint32(1))) >> 16
    return r[:, :n] | (r[:, n:] << 16)


def _unpack_bf16_pairs(w):
    lo = lax.bitcast_convert_type(w << 16, F32)
    hi = lax.bitcast_convert_type(w & jnp.uint32(0xFFFF0000), F32)
    return lo, hi


def _adaln_kernel(c_ref, w_ref, b_ref, o_ref):
    c = c_ref[...]
    s = c * jax.nn.sigmoid(c)
    o_ref[...] = jnp.dot(s, w_ref[...], preferred_element_type=F32,
                         precision=lax.Precision.HIGHEST) + b_ref[...]


def _adaln(c, ada_w, ada_b):
    b, d = c.shape
    n = ada_w.shape[1]
    return pl.pallas_call(
        _adaln_kernel,
        grid=(n // d,),
        in_specs=[pl.BlockSpec((b, d), lambda j: (0, 0)),
                  pl.BlockSpec((d, d), lambda j: (0, j)),
                  pl.BlockSpec((1, d), lambda j: (0, j))],
        out_specs=pl.BlockSpec((b, d), lambda j: (0, j)),
        out_shape=jax.ShapeDtypeStruct((b, n), F32),
        compiler_params=_cparams(("arbitrary",)),
        name="adaln",
    )(c, ada_w, ada_b.reshape(1, n))


def _rope_table():
    half = ROPE_DIMS // 2
    inv_freq = ROPE_THETA ** (-(np.arange(half, dtype=np.float32) * 2.0 / ROPE_DIMS))
    return jnp.asarray(np.broadcast_to(inv_freq[:, None], (half, LANES)).astype(np.float32))


def _inproj_kernel(x_ref, mod_ref, g_ref, pos_ref, rope_ref, w_ref,
                   qa_ref, ka_ref, va_ref, qb_ref, kb_ref, vb_ref, ga_ref, gb_ref):
    x = x_ref[...]
    tm = x.shape[0]
    shift = mod_ref[0, 0:1, :]
    scale = mod_ref[0, 1:2, :]
    hb = (_rms(x, g_ref[...]) * (1.0 + scale) + shift).astype(BF16)
    qk_scale = HEAD_DIM ** -0.5
    half = ROPE_DIMS // 2

    ang = rope_ref[:, 0:1] * pos_ref[0].astype(F32)
    cos_c, sin_c = jnp.cos(ang), jnp.sin(ang)
    rest = HEAD_DIM - ROPE_DIMS
    ones, zeros = jnp.ones((rest, tm), F32), jnp.zeros((rest, tm), F32)
    cos = jnp.concatenate([cos_c, cos_c, ones] * (LANES // HEAD_DIM), axis=0).T
    sin = jnp.concatenate([-sin_c, sin_c, zeros] * (LANES // HEAD_DIM), axis=0).T
    first = (lax.broadcasted_iota(I32, (1, LANES), 1) & (HEAD_DIM - 1)) < half

    def rope(t):
        def one(tile):
            partner = jnp.where(first, pltpu.roll(tile, LANES - half, 1), pltpu.roll(tile, half, 1))
            return tile * cos + partner * sin
        return jnp.concatenate([one(t[:, :LANES]), one(t[:, LANES:])], axis=1)

    def project(ref, col, width, finish):
        for c0 in range(0, width, 2 * LANES):
            t = jnp.dot(hb, w_ref[:, col + c0:col + c0 + 2 * LANES], preferred_element_type=F32)
            ref[:, c0:c0 + 2 * LANES] = finish(t).astype(ref.dtype)

    d = x.shape[1]
    col = 0
    for ref, width, finish in ((qa_ref, DIL_W, lambda t: rope(t) * qk_scale), (ka_ref, DIL_W, rope),
                               (va_ref, DIL_W, lambda t: t), (qb_ref, SB_W, lambda t: t * qk_scale),
                               (kb_ref, SB_W, lambda t: t), (vb_ref, SB_W, lambda t: t),
                               (ga_ref, d, jax.nn.sigmoid), (gb_ref, d, jax.nn.sigmoid)):
        project(ref, col, width, finish)
        col += width


def _in_proj(x2, mod3, gain, positions, w_in_bf16, seq):
    t, d = x2.shape
    tm = ROW_BLOCK
    nb = t // tm
    per_seq = seq // tm
    pos3 = positions.reshape(nb, 1, tm)
    widths = (DIL_W, DIL_W, DIL_W, SB_W, SB_W, SB_W, d, d)
    row = lambda i: (i, 0)
    return pl.pallas_call(
        _inproj_kernel,
        grid=(nb,),
        in_specs=[pl.BlockSpec((tm, d), row),
                  pl.BlockSpec((1, 6, d), lambda i: (i // per_seq, 0, 0)),
                  pl.BlockSpec((1, d), lambda i: (0, 0)),
                  pl.BlockSpec((1, 1, tm), lambda i: (i, 0, 0)),
                  pl.BlockSpec((8, LANES), lambda i: (0, 0)),
                  pl.BlockSpec(w_in_bf16.shape, lambda i: (0, 0))],
        out_specs=[pl.BlockSpec((tm, w), row) for w in widths],
        out_shape=[jax.ShapeDtypeStruct((t, w), BF16) for w in widths],
        compiler_params=_cparams(("parallel",)),
        name="in_proj",
    )(x2, mod3, gain.reshape(1, d), pos3, _rope_table(), w_in_bf16)


def _dilated_kernel(q_ref, kp_ref, kc_ref, vp_ref, vc_ref, o_ref, l_ref,
                    qf_ref, kf_ref, vf_ref, of_ref, lf_ref, *, r, nq):
    n = pl.program_id(1)
    prev_rows = Q_BLOCK * r
    halves = DIL_OUT // LANES
    for c in range(halves):
        cols = slice(c * LANES, (c + 1) * LANES)
        qf_ref[c] = q_ref[:, cols].astype(F32)
        kf_ref[c, 0:prev_rows, :] = kp_ref[:, cols].astype(F32)
        kf_ref[c, prev_rows:, :] = kc_ref[:, cols].astype(F32)
        vf_ref[c, 0:prev_rows, :] = vp_ref[:, cols].astype(F32)
        vf_ref[c, prev_rows:, :] = vc_ref[:, cols].astype(F32)

    def gather_rows(ref, start, size):
        idx = pl.ds(start, size, stride=r) if r > 1 else pl.ds(start, size)
        return jnp.concatenate([ref[c, idx, :] for c in range(halves)], axis=1).astype(BF16)

    def scatter_rows(ref, start, val):
        idx = pl.ds(start, Q_BLOCK, stride=r) if r > 1 else pl.ds(start, Q_BLOCK)
        for c in range(halves):
            ref[c, idx, :] = val[:, c * LANES:(c + 1) * LANES]

    qi = lax.broadcasted_iota(I32, (Q_BLOCK, 2 * Q_BLOCK), 0) + Q_BLOCK
    ki = lax.broadcasted_iota(I32, (Q_BLOCK, 2 * Q_BLOCK), 1)
    dist = qi - ki
    band = (dist >= 0) & (dist <= Q_BLOCK)
    lane = lax.broadcasted_iota(I32, (1, DIL_OUT), 1)

    heads = [(lane >= h * HEAD_DIM) & (lane < (h + 1) * HEAD_DIM) for h in range(DIL_HEADS_PER_GROUP)]

    def sub_block(idx, carry):
        rho = idx % r
        qb = idx // r
        base = r * Q_BLOCK * qb + rho
        q = gather_rows(qf_ref, base, Q_BLOCK)
        k = gather_rows(kf_ref, base, 2 * Q_BLOCK)
        v = gather_rows(vf_ref, base, 2 * Q_BLOCK)
        valid = band & ((n * nq + qb - 1) * Q_BLOCK + ki >= 0)
        qs = jnp.concatenate([jnp.where(heads[h], q, jnp.zeros_like(q)) for h in range(DIL_HEADS_PER_GROUP)], axis=0)
        s = lax.dot_general(qs, k, (((1,), (1,)), ((), ())), preferred_element_type=F32)
        s = jnp.where(jnp.concatenate([valid] * DIL_HEADS_PER_GROUP, axis=0), s, NEG_INF)
        m = jnp.max(s, axis=1, keepdims=True)
        p = jnp.exp(s - m).astype(BF16)
        pv = jnp.dot(p, v, preferred_element_type=F32)
        den = jnp.sum(p.astype(F32), axis=1, keepdims=True)
        inv = 1.0 / den
        lse = m + jnp.log(den)
        o = jnp.zeros((Q_BLOCK, DIL_OUT), F32)
        l = jnp.zeros((Q_BLOCK, DIL_OUT), F32)
        for h in range(DIL_HEADS_PER_GROUP):
            blk = slice(h * Q_BLOCK, (h + 1) * Q_BLOCK)
            o = jnp.where(heads[h], pv[blk] * inv[blk], o)
            l = jnp.where(heads[h], lse[blk], l)
        scatter_rows(of_ref, base, o)
        scatter_rows(lf_ref, base, l)
        return carry

    lax.fori_loop(0, r * nq, sub_block, 0, unroll=8)
    for c in range(halves):
        o_ref[:, c * LANES:(c + 1) * LANES] = of_ref[c]
        l_ref[:, c * LANES:(c + 1) * LANES] = lf_ref[c]


def _dilated_group(qa, ka, va, batch, seq, group):
    _, r = DIL_GROUPS[group]
    nq = max(1, DIL_TOKENS // (Q_BLOCK * r))
    prev_rows = Q_BLOCK * r
    cur_rows = prev_rows * nq
    halves = DIL_OUT // LANES
    view = lambda a: a.reshape(batch, seq, DIL_W)
    cur = lambda b, n: (b, n, group)
    prev = lambda b, n: (b, jnp.maximum(n * nq - 1, 0), group)
    cur_spec = pl.BlockSpec((None, cur_rows, DIL_OUT), cur)
    prev_spec = pl.BlockSpec((None, prev_rows, DIL_OUT), prev)
    out_spec = pl.BlockSpec((None, cur_rows, DIL_OUT), lambda b, n: (b, n, 0))
    o, l = pl.pallas_call(
        functools.partial(_dilated_kernel, r=r, nq=nq),
        grid=(batch, seq // cur_rows),
        in_specs=[cur_spec, prev_spec, cur_spec, prev_spec, cur_spec],
        out_specs=[out_spec, out_spec],
        out_shape=[jax.ShapeDtypeStruct((batch, seq, DIL_OUT), F32)] * 2,
        scratch_shapes=[pltpu.VMEM((halves, cur_rows, LANES), F32),
                        pltpu.VMEM((halves, prev_rows + cur_rows, LANES), F32),
                        pltpu.VMEM((halves, prev_rows + cur_rows, LANES), F32),
                        pltpu.VMEM((halves, cur_rows, LANES), F32),
                        pltpu.VMEM((halves, cur_rows, LANES), F32)],
        compiler_params=_cparams(("parallel", "arbitrary")),
        name=f"dilated_g{group}",
    )(view(qa), view(ka), view(ka), view(va), view(va))
    return o.reshape(batch * seq, DIL_OUT), l.reshape(batch * seq, DIL_OUT)


def _stickbreak_kernel(q_ref, k_ref, v_ref, o_ref, acc_ref, csum_ref, terms_ref, lb_ref):
    step = pl.program_id(2)
    tq = SB_TQ
    lane = lax.broadcasted_iota(I32, (1, LANES), 1)
    row = lax.broadcasted_iota(I32, (2 * tq, tq), 0)
    causal = lax.broadcasted_iota(I32, (2 * tq, tq), 1) < jnp.where(row >= tq, row - tq, row)
    later = (lax.broadcasted_iota(I32, (tq, tq), 0) > lax.broadcasted_iota(I32, (tq, tq), 1))
    later = jnp.where(later, 1.0, 0.0).astype(BF16)
    later2 = jnp.concatenate([later, later], axis=0)

    def stacked_queries(u):
        q = q_ref[u * tq:(u + 1) * tq, :]
        zero = jnp.zeros_like(q)
        return jnp.concatenate([jnp.where(lane < HEAD_DIM, q, zero), jnp.where(lane >= HEAD_DIM, q, zero)], axis=0)

    def emit(u, acc):
        o_ref[u * tq:(u + 1) * tq, :] = jnp.where(lane < HEAD_DIM, acc[:tq], acc[tq:]).astype(o_ref.dtype)


    def raw_scores(qs, j):
        start = pl.multiple_of(j * tq, tq)
        k = k_ref[pl.ds(start, tq), :]
        return lax.dot_general(qs, k, (((1,), (1,)), ((), ())), preferred_element_type=F32)

    def log_terms(z, diagonal):
        sp = jnp.maximum(z, 0.0) + jnp.log(1.0 + jnp.exp(-jnp.abs(z)))
        log_not = jnp.where(causal, -sp, 0.0) if diagonal else -sp
        log_beta = jnp.where(causal, z - sp, NEG_INF) if diagonal else z - sp
        hi = log_not.astype(BF16)
        terms_ref[:, :tq] = hi
        terms_ref[:, tq:] = (log_not - hi.astype(F32)).astype(BF16)
        csum = csum_ref[...]
        lb_ref[...] = log_beta + csum
        csum = csum + jnp.sum(log_not, axis=1, keepdims=True)
        csum_ref[...] = csum
        return jnp.max(csum)

    def later_sums():
        return jnp.dot(terms_ref[...], later2, preferred_element_type=F32)

    def weighted_values(j, inner):
        start = pl.multiple_of(j * tq, tq)
        v = v_ref[pl.ds(start, tq), :]
        w = jnp.exp(lb_ref[...] + inner)
        return jnp.dot(w.astype(BF16), v, preferred_element_type=F32)

    def cond(state):
        j, top = state
        return (j >= 0) & (top > SB_LOG_FLOOR)

    pending = None
    for u in range(SB_SUB):
        i = step * SB_SUB + u
        qs = stacked_queries(u)
        if pending is None:
            z = raw_scores(qs, i)
            pv = None
        else:
            inner = later_sums()
            z = raw_scores(qs, i)
            pv = weighted_values(pending[1] + 1, inner)
        csum_ref[...] = jnp.zeros_like(csum_ref)
        acc_ref[u] = jnp.zeros(acc_ref.shape[1:], F32)
        top = log_terms(z, True)
        if pending is not None:
            emit(pending[0], acc_ref[pending[0]] + pv)

        def body(state, qs=qs, u=u):
            j, _ = state
            inner = later_sums()
            z = raw_scores(qs, j)
            pv = weighted_values(j + 1, inner)
            top = log_terms(z, False)
            acc_ref[u] += pv
            return j - 1, top

        last, _ = lax.while_loop(cond, body, (i - 1, top))
        pending = (u, last)
    emit(pending[0], acc_ref[pending[0]] + weighted_values(pending[1] + 1, later_sums()))


def _stickbreak(qb, kb, vb, batch, seq):
    view = lambda a: a.reshape(batch, seq, SB_W)
    rows = SB_TQ * SB_SUB
    pairs = SB_W // LANES
    kv_spec = pl.BlockSpec((None, seq, LANES), lambda b, hp, i: (b, 0, hp))
    q_spec = pl.BlockSpec((None, rows, LANES), lambda b, hp, i: (b, i, hp))
    out = pl.pallas_call(
        _stickbreak_kernel,
        grid=(batch, pairs, seq // rows),
        in_specs=[q_spec, kv_spec, kv_spec],
        out_specs=q_spec,
        out_shape=jax.ShapeDtypeStruct((batch, seq, SB_W), BF16),
        scratch_shapes=[pltpu.VMEM((SB_SUB, 2 * SB_TQ, LANES), F32), pltpu.VMEM((2 * SB_TQ, 1), F32),
                        pltpu.VMEM((2 * SB_TQ, 2 * SB_TQ), BF16), pltpu.VMEM((2 * SB_TQ, SB_TQ), F32)],
        compiler_params=_cparams(("parallel", "parallel", "arbitrary")),
        name="stickbreak",
    )(view(qb), view(kb), view(vb))
    return out.reshape(batch * seq, SB_W)


def _postmix_kernel(o0_ref, o1_ref, o2_ref, l0_ref, l1_ref, l2_ref, sb_ref, ga_ref, gb_ref, x_ref,
                    mod_ref, gpost_ref, gffn_ref, wa_ref, wb_ref, wo_ref, wr_ref, br_ref,
                    x1_ref, h2_ref, idx_ref, gate_ref, rank_ref, cnt_ref, carry_ref, before_ref):
    step = pl.program_id(0)
    tm = x_ref.shape[0]

    @pl.when(step == 0)
    def _():
        carry_ref[...] = jnp.zeros_like(carry_ref)
        earlier = lax.broadcasted_iota(I32, (tm, tm), 0) < lax.broadcasted_iota(I32, (tm, tm), 1)
        before_ref[...] = jnp.where(earlier, 1.0, 0.0).astype(BF16)

    l0, l1, l2 = l0_ref[...], l1_ref[...], l2_ref[...]
    m = jnp.maximum(jnp.maximum(l0, l1), l2)
    e0, e1, e2 = jnp.exp(l0 - m), jnp.exp(l1 - m), jnp.exp(l2 - m)
    merged = (e0 * o0_ref[...] + e1 * o1_ref[...] + e2 * o2_ref[...]) / (e0 + e1 + e2)
    ya = jnp.dot(merged.astype(BF16), wa_ref[...], preferred_element_type=F32)
    yb = jnp.dot(sb_ref[...], wb_ref[...], preferred_element_type=F32)
    mix = ga_ref[...] * ya.astype(BF16) + gb_ref[...] * yb.astype(BF16)
    y = jnp.dot(mix, wo_ref[...], preferred_element_type=F32)
    x1 = x_ref[...] + _rms(y, gpost_ref[...] * mod_ref[0, 2:3, :])
    x1_ref[...] = x1
    h2 = _rms(x1, gffn_ref[...] * (1.0 + mod_ref[0, 4:5, :])) + mod_ref[0, 3:4, :]
    h2_ref[...] = _pack_bf16_pairs(h2)

    def split(a):
        hi = a.astype(BF16)
        return hi, (a - hi.astype(F32)).astype(BF16)

    w_hi, w_lo = split(wr_ref[...])
    h_hi, h_lo = split(h2)
    logits = lax.dot_general(jnp.concatenate([w_hi, w_hi, w_lo], axis=1), jnp.concatenate([h_hi, h_lo, h_hi], axis=1),
                             (((1,), (1,)), ((), ())), preferred_element_type=F32) + br_ref[...]
    e_iota = lax.broadcasted_iota(I32, (N_EXPERTS, tm), 0)

    picks, vals = [], []
    work = logits
    chosen = jnp.zeros((N_EXPERTS, tm), F32)
    for _ in range(TOP_K):
        top = jnp.max(work, axis=0, keepdims=True)
        idx = jnp.min(jnp.where(work == top, e_iota, N_EXPERTS), axis=0, keepdims=True)
        one = e_iota == idx
        picks.append((idx, one))
        vals.append(top)
        chosen = jnp.where(one, 1.0, chosen)
        work = jnp.where(one, -jnp.inf, work)
    exps = [jnp.exp(v - vals[0]) for v in vals]
    den = exps[0] + exps[1] + exps[2] + exps[3]

    rank = jnp.dot(chosen.astype(BF16), before_ref[...], preferred_element_type=F32) + carry_ref[...]
    carry_ref[...] += jnp.sum(chosen, axis=1, keepdims=True)
    cnt_ref[...] = carry_ref[...].astype(I32)
    for kk, (idx, one) in enumerate(picks):
        idx_ref[kk:kk + 1, :] = idx
        gate_ref[kk:kk + 1, :] = exps[kk] / den
        rank_ref[kk:kk + 1, :] = jnp.sum(jnp.where(one, rank, 0.0), axis=0, keepdims=True).astype(I32)


def _post_mix(dil, sb, ga, gb, x2, mod3, gpost, gffn, wa, wb, wo, w_router, b_router, seq, part):
    t_all, d = x2.shape
    t = t_all // MOE_PARTS
    tm = ROW_BLOCK
    nb = t // tm
    off = part * nb
    per_seq = seq // tm
    row_in = lambda i: (i + off, 0)
    row = lambda i: (i, 0)
    const = lambda i: (0, 0)
    colblk = lambda i: (0, i)
    (o0, l0), (o1, l1), (o2, l2) = dil
    in_specs = ([pl.BlockSpec((tm, DIL_OUT), row_in)] * 6
                + [pl.BlockSpec((tm, SB_W), row_in), pl.BlockSpec((tm, d), row_in), pl.BlockSpec((tm, d), row_in),
                   pl.BlockSpec((tm, d), row_in),
                   pl.BlockSpec((1, 6, d), lambda i: ((i + off) // per_seq, 0, 0)),
                   pl.BlockSpec((1, d), const), pl.BlockSpec((1, d), const),
                   pl.BlockSpec(wa.shape, const), pl.BlockSpec(wb.shape, const), pl.BlockSpec(wo.shape, const),
                   pl.BlockSpec((N_EXPERTS, d), const), pl.BlockSpec((N_EXPERTS, 1), const)])
    out_specs = [pl.BlockSpec((tm, d), row), pl.BlockSpec((tm, d // 2), row),
                 pl.BlockSpec((TOP_K, tm), colblk), pl.BlockSpec((TOP_K, tm), colblk),
                 pl.BlockSpec((TOP_K, tm), colblk), pl.BlockSpec((N_EXPERTS, 1), const)]
    out_shape = [jax.ShapeDtypeStruct((t, d), F32), jax.ShapeDtypeStruct((t, d // 2), U32),
                 jax.ShapeDtypeStruct((TOP_K, t), I32), jax.ShapeDtypeStruct((TOP_K, t), F32),
                 jax.ShapeDtypeStruct((TOP_K, t), I32), jax.ShapeDtypeStruct((N_EXPERTS, 1), I32)]
    return pl.pallas_call(
        _postmix_kernel,
        grid=(nb,),
        in_specs=in_specs,
        out_specs=out_specs,
        out_shape=out_shape,
        scratch_shapes=[pltpu.VMEM((N_EXPERTS, 1), F32), pltpu.VMEM((tm, tm), BF16)],
        compiler_params=_cparams(("arbitrary",)),
        name="post_mix",
    )(o0, o1, o2, l0, l1, l2, sb, ga, gb, x2, mod3, gpost.reshape(1, d), gffn.reshape(1, d),
      wa, wb, wo, w_router.T, b_router.reshape(N_EXPERTS, 1))


def _slots_kernel(start_ref, idx_ref, rank_ref, slot_ref):
    idx = idx_ref[...]
    base = jnp.zeros_like(idx)
    for e in range(N_EXPERTS):
        base = jnp.where(idx == e, start_ref[e], base)
    slot_ref[...] = base + rank_ref[...]


def _slots(pad_start, idx, rank):
    k, t = idx.shape
    tb = min(t, 8192)
    blk = pl.BlockSpec((k, tb), lambda i, s: (0, i))
    return pl.pallas_call(
        _slots_kernel,
        grid_spec=pltpu.PrefetchScalarGridSpec(
            num_scalar_prefetch=1, grid=(t // tb,), in_specs=[blk, blk], out_specs=blk),
        out_shape=jax.ShapeDtypeStruct((k, t), I32),
        compiler_params=_cparams(("parallel",)),
        name="slots",
    )(pad_start, idx, rank)


def _sc_mesh():
    return plsc.VectorSubcoreMesh(core_axis_name="c", subcore_axis_name="s")


def _sc_dispatch(rows, slot, n_slots):
    t, d = rows.shape
    chunk = SC_DISPATCH_CHUNK
    per_worker = t // SC_WORKERS
    n_chunks = per_worker // chunk

    @functools.partial(
        pl.kernel, mesh=_sc_mesh(),
        out_type=jax.ShapeDtypeStruct((n_slots, d), rows.dtype),
        scratch_types=[pltpu.VMEM((TOP_K, chunk), I32), pltpu.VMEM((chunk, d), rows.dtype),
                       pltpu.SemaphoreType.DMA],
        name="dispatch",
    )
    def run(rows_hbm, slot_hbm, out_hbm, idx_v, rows_v, sem):
        wid = lax.axis_index("s") * SC_CORES + lax.axis_index("c")

        @pl.loop(0, n_chunks)
        def _(ci):
            base = pl.multiple_of(wid * per_worker + ci * chunk, chunk)
            loads = [pltpu.make_async_copy(rows_hbm.at[pl.ds(base, chunk)], rows_v, sem)]
            loads += [pltpu.make_async_copy(slot_hbm.at[kk, pl.ds(base, chunk)], idx_v.at[kk], sem)
                      for kk in range(TOP_K)]
            for cp in loads:
                cp.start()
            for cp in loads:
                cp.wait()
            copies = [pltpu.make_async_copy(rows_v, out_hbm.at[idx_v.at[kk]], sem) for kk in range(TOP_K)]
            for cp in copies:
                cp.start()
            for cp in copies:
                cp.wait()

    return run(rows, slot)


def _sc_combine(ys, slot):
    _, d = ys.shape
    k, t = slot.shape
    chunk = SC_COMBINE_CHUNK
    per_worker = t // SC_WORKERS
    n_chunks = per_worker // chunk

    @functools.partial(
        pl.kernel, mesh=_sc_mesh(),
        out_type=jax.ShapeDtypeStruct((k, t, d), ys.dtype),
        scratch_types=[pltpu.VMEM((TOP_K, chunk), I32), pltpu.VMEM((TOP_K, chunk, d), ys.dtype),
                       pltpu.SemaphoreType.DMA],
        name="combine",
    )
    def run(ys_hbm, slot_hbm, out_hbm, idx_v, rows_v, sem):
        wid = lax.axis_index("s") * SC_CORES + lax.axis_index("c")

        @pl.loop(0, n_chunks)
        def _(ci):
            base = pl.multiple_of(wid * per_worker + ci * chunk, chunk)
            loads = [pltpu.make_async_copy(slot_hbm.at[kk, pl.ds(base, chunk)], idx_v.at[kk], sem)
                     for kk in range(TOP_K)]
            for cp in loads:
                cp.start()
            for cp in loads:
                cp.wait()
            gathers = [pltpu.make_async_copy(ys_hbm.at[idx_v.at[kk]], rows_v.at[kk], sem) for kk in range(TOP_K)]
            for cp in gathers:
                cp.start()
            for cp in gathers:
                cp.wait()
            stores = [pltpu.make_async_copy(rows_v.at[kk], out_hbm.at[kk, pl.ds(base, chunk)], sem)
                      for kk in range(TOP_K)]
            for cp in stores:
                cp.start()
            for cp in stores:
                cp.wait()

    return run(ys, slot)


def _experts_kernel(be_ref, nused_ref, x_ref, *refs, prepared):
    if prepared:
        glu_w_ref, lin_w_ref, bg_ref, bl_ref, down_w_ref, bd_ref, y_ref = refs
    else:
        wu_ref, bg_ref, bl_ref, wd_ref, bd_ref, y_ref, glu_w_ref, lin_w_ref, down_w_ref = refs
    i = pl.program_id(0)
    live = i < nused_ref[0]

    if not prepared:
        new_expert = (i == 0) | (be_ref[i] != be_ref[jnp.maximum(i - 1, 0)])

        @pl.when(live & new_expert)
        def _():
            tile = 2 * LANES
            src = lax.broadcasted_iota(I32, (tile, tile), 0)
            dst = lax.broadcasted_iota(I32, (tile, tile), 1)
            wanted = jnp.where(dst < LANES, 2 * dst, 2 * (dst - LANES) + 1)
            pick = jnp.where(src == wanted, 1.0, 0.0).astype(BF16)
            for c in range(wu_ref.shape[2] // tile):
                w = wu_ref[0, :, c * tile:(c + 1) * tile].astype(BF16)
                both = jnp.dot(w, pick, preferred_element_type=F32)
                glu_w_ref[0, :, c * LANES:(c + 1) * LANES] = both[:, :LANES].astype(BF16)
                lin_w_ref[0, :, c * LANES:(c + 1) * LANES] = both[:, LANES:].astype(BF16)
            down_w_ref[0] = wd_ref[0].astype(BF16)

    @pl.when(live)
    def _():
        lo, hi = _unpack_bf16_pairs(x_ref[...])
        x = jnp.concatenate([lo, hi], axis=1).astype(BF16)
        glu = jnp.dot(x, glu_w_ref[0], preferred_element_type=F32) + bg_ref[0]
        lin = jnp.dot(x, lin_w_ref[0], preferred_element_type=F32) + bl_ref[0]
        glu = jnp.minimum(glu, SWIGLU_LIMIT)
        lin = jnp.clip(lin, -SWIGLU_LIMIT, SWIGLU_LIMIT)
        act = glu * jax.nn.sigmoid(SWIGLU_ALPHA * glu) * (lin + 1.0)
        y = jnp.dot(act.astype(BF16), down_w_ref[0], preferred_element_type=F32) + bd_ref[0]
        y_ref[...] = _pack_bf16_pairs(y)


def _experts(xs, block_expert, n_used, b_glu, b_lin, b_down, raw=None, prepared=None):
    n_slots, half_d = xs.shape
    e, _, f = b_glu.shape
    d = b_down.shape[2]
    nblk = n_slots // MOE_ROWS
    rows = lambda i, be, nu: (i, 0)
    by_e = lambda i, be, nu: (be[i], 0, 0)
    x_spec = pl.BlockSpec((MOE_ROWS, half_d), rows)
    bias_f, bias_d = pl.BlockSpec((1, 1, f), by_e), pl.BlockSpec((1, 1, d), by_e)
    up_bf16, down_bf16 = pl.BlockSpec((1, d, f), by_e), pl.BlockSpec((1, f, d), by_e)
    ys_shape = jax.ShapeDtypeStruct((n_slots, half_d), U32)
    if prepared is not None:
        in_specs = [x_spec, up_bf16, up_bf16, bias_f, bias_f, down_bf16, bias_d]
        args = (xs, prepared[0], prepared[1], b_glu, b_lin, prepared[2], b_down)
        out_specs, out_shape = x_spec, ys_shape
    else:
        w_up, w_down = raw
        in_specs = [x_spec, pl.BlockSpec((1, d, 2 * f), by_e), bias_f, bias_f, pl.BlockSpec((1, f, d), by_e), bias_d]
        args = (xs, w_up, b_glu, b_lin, w_down, b_down)
        out_specs = [x_spec, up_bf16, up_bf16, down_bf16]
        out_shape = [ys_shape, jax.ShapeDtypeStruct((e, d, f), BF16), jax.ShapeDtypeStruct((e, d, f), BF16),
                     jax.ShapeDtypeStruct((e, f, d), BF16)]
    out = pl.pallas_call(
        functools.partial(_experts_kernel, prepared=prepared is not None),
        grid_spec=pltpu.PrefetchScalarGridSpec(
            num_scalar_prefetch=2, grid=(nblk,), in_specs=in_specs, out_specs=out_specs),
        out_shape=out_shape,
        compiler_params=_cparams(("arbitrary",)),
        name="experts",
    )(block_expert, n_used, *args)
    return out if prepared is not None else (out[0], tuple(out[1:]))


def _final_kernel(g_ref, gate_ref, x1_ref, mod_ref, gain_ref, *rest):
    o_ref = rest[-1]
    tm = x1_ref.shape[0]
    gates = gate_ref[...]
    padded = jnp.concatenate([gates, jnp.zeros((LANES - TOP_K, tm), F32)], axis=0)
    gate_cols = padded.T
    y_lo = jnp.zeros(g_ref.shape[1:], F32)
    y_hi = jnp.zeros(g_ref.shape[1:], F32)
    for kk in range(TOP_K):
        lo, hi = _unpack_bf16_pairs(g_ref[kk])
        y_lo = y_lo + lo * gate_cols[:, kk:kk + 1]
        y_hi = y_hi + hi * gate_cols[:, kk:kk + 1]
    y = jnp.concatenate([y_lo, y_hi], axis=1)
    o_ref[...] = x1_ref[...] + mod_ref[0, 5:6, :] * _rms(y, gain_ref[...])


def _final(g, gates, x1, mod3, gain, seq, part, out_so_far):
    t, d = x1.shape
    tm = ROW_BLOCK
    nb = t // tm
    off = part * nb
    per_seq = seq // tm
    row = lambda i: (i, 0)
    in_specs = [pl.BlockSpec((TOP_K, tm, d // 2), lambda i: (0, i, 0)),
                pl.BlockSpec((TOP_K, tm), lambda i: (0, i)),
                pl.BlockSpec((tm, d), row),
                pl.BlockSpec((1, 6, d), lambda i: ((i + off) // per_seq, 0, 0)),
                pl.BlockSpec((1, d), lambda i: (0, 0))]
    args = [g, gates, x1, mod3, gain.reshape(1, d)]
    aliases = {}
    if out_so_far is not None:
        in_specs.append(pl.BlockSpec(memory_space=pl.ANY))
        args.append(out_so_far)
        aliases = {len(args) - 1: 0}
    return pl.pallas_call(
        _final_kernel,
        grid=(nb,),
        in_specs=in_specs,
        out_specs=pl.BlockSpec((tm, d), lambda i: (i + off, 0)),
        out_shape=jax.ShapeDtypeStruct((t * MOE_PARTS, d), F32),
        input_output_aliases=aliases,
        compiler_params=_cparams(("parallel",)),
        name="final",
    )(*args)


def _layer(x2, c, positions, seq, ada_w, ada_b, norm_mix_pre, norm_mix_post, norm_ffn_pre, norm_ffn_post,
           w_in, w_branch_a, w_branch_b, w_out, w_router, b_router, w_up, b_up, w_down, b_down):
    t, d = x2.shape
    batch = t // seq
    mod3 = _adaln(c, ada_w, ada_b).reshape(batch, 6, d)

    qa, ka, va, qb, kb, vb, ga, gb = _in_proj(x2, mod3, norm_mix_pre, positions, w_in.astype(BF16), seq)
    dil = [_dilated_group(qa, ka, va, batch, seq, g) for g in range(len(DIL_GROUPS))]
    sb = _stickbreak(qb, kb, vb, batch, seq)
    wa, wb, wo = w_branch_a.astype(BF16), w_branch_b.astype(BF16), w_out.astype(BF16)
    f = w_up.shape[2] // 2
    b_glu, b_lin = b_up[:, 0::2].reshape(N_EXPERTS, 1, f), b_up[:, 1::2].reshape(N_EXPERTS, 1, f)
    b_down3 = b_down.reshape(N_EXPERTS, 1, d)
    nblk = (t // MOE_PARTS * TOP_K) // MOE_ROWS + N_EXPERTS

    routed = []
    for part in range(MOE_PARTS):
        x1, h2, idx, gates, rank, counts = _post_mix(
            dil, sb, ga, gb, x2, mod3, norm_mix_post, norm_ffn_pre, wa, wb, wo, w_router, b_router, seq, part)
        counts = counts.reshape(N_EXPERTS)
        padded = jnp.maximum((counts + MOE_ROWS - 1) // MOE_ROWS, 1) * MOE_ROWS
        pad_end = jnp.cumsum(padded)
        pad_start = (pad_end - padded).astype(I32)
        block_first_row = jnp.arange(nblk, dtype=I32)[:, None] * MOE_ROWS
        block_expert = jnp.minimum(jnp.sum(pad_end[None, :] <= block_first_row, axis=1), N_EXPERTS - 1).astype(I32)
        n_used = (pad_end[-1:] // MOE_ROWS).astype(I32)
        slot = _slots(pad_start, idx, rank)
        xs = _sc_dispatch(h2, slot, nblk * MOE_ROWS)
        routed.append((x1, gates, slot, xs, block_expert, n_used))

    gathered = []
    prepared = None
    for x1, gates, slot, xs, block_expert, n_used in routed:
        if prepared is None:
            ys, prepared = _experts(xs, block_expert, n_used, b_glu, b_lin, b_down3, raw=(w_up, w_down))
        else:
            ys = _experts(xs, block_expert, n_used, b_glu, b_lin, b_down3, prepared=prepared)
        gathered.append(_sc_combine(ys, slot))

    out = None
    for part, ((x1, gates, *_), g) in enumerate(zip(routed, gathered)):
        out = _final(g, gates, x1, mod3, norm_ffn_post, seq, part, out)
    return out


def kernel(x, c, positions, ada_w, ada_b, norm_mix_pre, norm_mix_post, norm_ffn_pre, norm_ffn_post,
           w_in, w_branch_a, w_branch_b, w_out, w_router, b_router, w_up, b_up, w_down, b_down):
    batch, seq, d = x.shape
    tokens = batch * seq
    assert w_in.shape[-1] == 3 * DIL_W + 3 * SB_W + 2 * d and d % (2 * LANES) == 0
    assert w_router.shape[-1] == N_EXPERTS and w_up.shape[-1] == 2 * w_down.shape[-2]
    assert seq % ROW_BLOCK == 0 and seq % (SB_TQ * SB_SUB) == 0
    assert all(seq % max(Q_BLOCK * r, DIL_TOKENS) == 0 for _, r in DIL_GROUPS)
    assert tokens % (MOE_PARTS * ROW_BLOCK) == 0
    assert tokens % (MOE_PARTS * SC_WORKERS * max(SC_DISPATCH_CHUNK, SC_COMBINE_CHUNK)) == 0
    x2 = x.reshape(tokens, d)
    for layer in range(ada_w.shape[0]):
        x2 = _layer(x2, c, positions, seq, ada_w[layer], ada_b[layer], norm_mix_pre[layer], norm_mix_post[layer],
                    norm_ffn_pre[layer], norm_ffn_post[layer], w_in[layer], w_branch_a[layer], w_branch_b[layer],
                    w_out[layer], w_router[layer], b_router[layer], w_up[layer], b_up[layer], w_down[layer],
                    b_down[layer])
    return x2.reshape(batch, seq, d)
```

```python
import functools

import numpy as np
import jax
import jax.numpy as jnp
from jax import lax
from jax.experimental import pallas as pl
from jax.experimental.pallas import tpu as pltpu
from jax.experimental.pallas import tpu_sc as plsc

F32 = jnp.float32
BF16 = jnp.bfloat16
I32 = jnp.int32
U32 = jnp.uint32

HEAD_DIM = 64
DIL_GROUPS = ((128, 1), (512, 4), (2048, 16))
DIL_HEADS_PER_GROUP = 4
DIL_HEADS = DIL_HEADS_PER_GROUP * len(DIL_GROUPS)
DIL_W = DIL_HEADS * HEAD_DIM
DIL_OUT = DIL_HEADS_PER_GROUP * HEAD_DIM
SB_HEADS = 8
SB_W = SB_HEADS * HEAD_DIM
ROPE_THETA = 500000.0
ROPE_DIMS = HEAD_DIM // 4
Q_BLOCK = 128
N_EXPERTS = 32
TOP_K = 4
SWIGLU_ALPHA = 1.702
SWIGLU_LIMIT = 7.0
NORM_EPS = 1e-6
NEG_INF = -1e30

LANES = 128
ROW_BLOCK = 512
MOE_ROWS = 512
MOE_PARTS = 2
DIL_TOKENS = 1024
SB_TQ = 256
SB_SUB = 8
SB_LOG_FLOOR = -105.0
V7X_VMEM_BYTES = 64 * 1024 * 1024
VMEM_LIMIT = V7X_VMEM_BYTES * 7 // 8

SC_CORES = 2
SC_SUBCORES = 16
SC_WORKERS = SC_CORES * SC_SUBCORES
SC_DISPATCH_CHUNK = 64
SC_COMBINE_CHUNK = 32


def _cparams(sem):
    return pltpu.CompilerParams(dimension_semantics=sem, vmem_limit_bytes=VMEM_LIMIT)


def _rms(x, gain):
    ms = jnp.mean(x * x, axis=-1, keepdims=True)
    return x * lax.rsqrt(ms + NORM_EPS) * gain


def _pack_bf16_pairs(x):
    n = x.shape[1] // 2
    u = lax.bitcast_convert_type(x, U32)
    r = (u + jnp.uint32(0x7FFF) + ((u >> 16) & jnp.uint32(1))) >> 16
    return r[:, :n] | (r[:, n:] << 16)


def _unpack_bf16_pairs(w):
    lo = lax.bitcast_convert_type(w << 16, F32)
    hi = lax.bitcast_convert_type(w & jnp.uint32(0xFFFF0000), F32)
    return lo, hi


def _adaln_kernel(c_ref, w_ref, b_ref, o_ref):
    c = c_ref[...]
    s = c * jax.nn.sigmoid(c)
    o_ref[...] = jnp.dot(s, w_ref[...], preferred_element_type=F32,
                         precision=lax.Precision.HIGHEST) + b_ref[...]


def _adaln(c, ada_w, ada_b):
    b, d = c.shape
    n = ada_w.shape[1]
    return pl.pallas_call(
        _adaln_kernel,
        grid=(n // d,),
        in_specs=[pl.BlockSpec((b, d), lambda j: (0, 0)),
                  pl.BlockSpec((d, d), lambda j: (0, j)),
                  pl.BlockSpec((1, d), lambda j: (0, j))],
        out_specs=pl.BlockSpec((b, d), lambda j: (0, j)),
        out_shape=jax.ShapeDtypeStruct((b, n), F32),
        compiler_params=_cparams(("arbitrary",)),
        name="adaln",
    )(c, ada_w, ada_b.reshape(1, n))


def _rope_table():
    half = ROPE_DIMS // 2
    inv_freq = ROPE_THETA ** (-(np.arange(half, dtype=np.float32) * 2.0 / ROPE_DIMS))
    return jnp.asarray(np.broadcast_to(inv_freq[:, None], (half, LANES)).astype(np.float32))


def _inproj_kernel(x_ref, mod_ref, g_ref, pos_ref, rope_ref, w_ref,
                   qa_ref, ka_ref, va_ref, qb_ref, kb_ref, vb_ref, ga_ref, gb_ref):
    x = x_ref[...]
    tm = x.shape[0]
    shift = mod_ref[0, 0:1, :]
    scale = mod_ref[0, 1:2, :]
    hb = (_rms(x, g_ref[...]) * (1.0 + scale) + shift).astype(BF16)
    qk_scale = HEAD_DIM ** -0.5
    half = ROPE_DIMS // 2

    ang = rope_ref[:, 0:1] * pos_ref[0].astype(F32)
    cos_c, sin_c = jnp.cos(ang), jnp.sin(ang)
    rest = HEAD_DIM - ROPE_DIMS
    ones, zeros = jnp.ones((rest, tm), F32), jnp.zeros((rest, tm), F32)
    cos = jnp.concatenate([cos_c, cos_c, ones] * (LANES // HEAD_DIM), axis=0).T
    sin = jnp.concatenate([-sin_c, sin_c, zeros] * (LANES // HEAD_DIM), axis=0).T
    first = (lax.broadcasted_iota(I32, (1, LANES), 1) & (HEAD_DIM - 1)) < half

    def rope(t):
        def one(tile):
            partner = jnp.where(first, pltpu.roll(tile, LANES - half, 1), pltpu.roll(tile, half, 1))
            return tile * cos + partner * sin
        return jnp.concatenate([one(t[:, :LANES]), one(t[:, LANES:])], axis=1)

    def project(ref, col, width, finish):
        for c0 in range(0, width, 2 * LANES):
            t = jnp.dot(hb, w_ref[:, col + c0:col + c0 + 2 * LANES], preferred_element_type=F32)
            ref[:, c0:c0 + 2 * LANES] = finish(t).astype(ref.dtype)

    d = x.shape[1]
    col = 0
    for ref, width, finish in ((qa_ref, DIL_W, lambda t: rope(t) * qk_scale), (ka_ref, DIL_W, rope),
                               (va_ref, DIL_W, lambda t: t), (qb_ref, SB_W, lambda t: t * qk_scale),
                               (kb_ref, SB_W, lambda t: t), (vb_ref, SB_W, lambda t: t),
                               (ga_ref, d, jax.nn.sigmoid), (gb_ref, d, jax.nn.sigmoid)):
        project(ref, col, width, finish)
        col += width


def _in_proj(x2, mod3, gain, positions, w_in_bf16, seq):
    t, d = x2.shape
    tm = ROW_BLOCK
    nb = t // tm
    per_seq = seq // tm
    pos3 = positions.reshape(nb, 1, tm)
    widths = (DIL_W, DIL_W, DIL_W, SB_W, SB_W, SB_W, d, d)
    row = lambda i: (i, 0)
    return pl.pallas_call(
        _inproj_kernel,
        grid=(nb,),
        in_specs=[pl.BlockSpec((tm, d), row),
                  pl.BlockSpec((1, 6, d), lambda i: (i // per_seq, 0, 0)),
                  pl.BlockSpec((1, d), lambda i: (0, 0)),
                  pl.BlockSpec((1, 1, tm), lambda i: (i, 0, 0)),
                  pl.BlockSpec((8, LANES), lambda i: (0, 0)),
                  pl.BlockSpec(w_in_bf16.shape, lambda i: (0, 0))],
        out_specs=[pl.BlockSpec((tm, w), row) for w in widths],
        out_shape=[jax.ShapeDtypeStruct((t, w), BF16) for w in widths],
        compiler_params=_cparams(("parallel",)),
        name="in_proj",
    )(x2, mod3, gain.reshape(1, d), pos3, _rope_table(), w_in_bf16)


def _dilated_kernel(q_ref, kp_ref, kc_ref, vp_ref, vc_ref, o_ref, l_ref,
                    qf_ref, kf_ref, vf_ref, of_ref, lf_ref, *, r, nq):
    n = pl.program_id(1)
    prev_rows = Q_BLOCK * r
    halves = DIL_OUT // LANES
    for c in range(halves):
        cols = slice(c * LANES, (c + 1) * LANES)
        qf_ref[c] = q_ref[:, cols].astype(F32)
        kf_ref[c, 0:prev_rows, :] = kp_ref[:, cols].astype(F32)
        kf_ref[c, prev_rows:, :] = kc_ref[:, cols].astype(F32)
        vf_ref[c, 0:prev_rows, :] = vp_ref[:, cols].astype(F32)
        vf_ref[c, prev_rows:, :] = vc_ref[:, cols].astype(F32)

    def gather_rows(ref, start, size):
        idx = pl.ds(start, size, stride=r) if r > 1 else pl.ds(start, size)
        return jnp.concatenate([ref[c, idx, :] for c in range(halves)], axis=1).astype(BF16)

    def scatter_rows(ref, start, val):
        idx = pl.ds(start, Q_BLOCK, stride=r) if r > 1 else pl.ds(start, Q_BLOCK)
        for c in range(halves):
            ref[c, idx, :] = val[:, c * LANES:(c + 1) * LANES]

    qi = lax.broadcasted_iota(I32, (Q_BLOCK, 2 * Q_BLOCK), 0) + Q_BLOCK
    ki = lax.broadcasted_iota(I32, (Q_BLOCK, 2 * Q_BLOCK), 1)
    dist = qi - ki
    band = (dist >= 0) & (dist <= Q_BLOCK)
    lane = lax.broadcasted_iota(I32, (1, DIL_OUT), 1)

    heads = [(lane >= h * HEAD_DIM) & (lane < (h + 1) * HEAD_DIM) for h in range(DIL_HEADS_PER_GROUP)]

    def sub_block(idx, carry):
        rho = idx % r
        qb = idx // r
        base = r * Q_BLOCK * qb + rho
        q = gather_rows(qf_ref, base, Q_BLOCK)
        k = gather_rows(kf_ref, base, 2 * Q_BLOCK)
        v = gather_rows(vf_ref, base, 2 * Q_BLOCK)
        valid = band & ((n * nq + qb - 1) * Q_BLOCK + ki >= 0)
        qs = jnp.concatenate([jnp.where(heads[h], q, jnp.zeros_like(q)) for h in range(DIL_HEADS_PER_GROUP)], axis=0)
        s = lax.dot_general(qs, k, (((1,), (1,)), ((), ())), preferred_element_type=F32)
        s = jnp.where(jnp.concatenate([valid] * DIL_HEADS_PER_GROUP, axis=0), s, NEG_INF)
        m = jnp.max(s, axis=1, keepdims=True)
        p = jnp.exp(s - m).astype(BF16)
        pv = jnp.dot(p, v, preferred_element_type=F32)
        den = jnp.sum(p.astype(F32), axis=1, keepdims=True)
        inv = 1.0 / den
        lse = m + jnp.log(den)
        o = jnp.zeros((Q_BLOCK, DIL_OUT), F32)
        l = jnp.zeros((Q_BLOCK, DIL_OUT), F32)
        for h in range(DIL_HEADS_PER_GROUP):
            blk = slice(h * Q_BLOCK, (h + 1) * Q_BLOCK)
            o = jnp.where(heads[h], pv[blk] * inv[blk], o)
            l = jnp.where(heads[h], lse[blk], l)
        scatter_rows(of_ref, base, o)
        scatter_rows(lf_ref, base, l)
        return carry

    lax.fori_loop(0, r * nq, sub_block, 0, unroll=8)
    for c in range(halves):
        o_ref[:, c * LANES:(c + 1) * LANES] = of_ref[c].astype(o_ref.dtype)
        l_ref[:, c * LANES:(c + 1) * LANES] = lf_ref[c]


def _dilated_group(qa, ka, va, batch, seq, group):
    _, r = DIL_GROUPS[group]
    nq = max(1, DIL_TOKENS // (Q_BLOCK * r))
    prev_rows = Q_BLOCK * r
    cur_rows = prev_rows * nq
    halves = DIL_OUT // LANES
    view = lambda a: a.reshape(batch, seq, DIL_W)
    cur = lambda b, n: (b, n, group)
    prev = lambda b, n: (b, jnp.maximum(n * nq - 1, 0), group)
    cur_spec = pl.BlockSpec((None, cur_rows, DIL_OUT), cur)
    prev_spec = pl.BlockSpec((None, prev_rows, DIL_OUT), prev)
    out_spec = pl.BlockSpec((None, cur_rows, DIL_OUT), lambda b, n: (b, n, 0))
    o, l = pl.pallas_call(
        functools.partial(_dilated_kernel, r=r, nq=nq),
        grid=(batch, seq // cur_rows),
        in_specs=[cur_spec, prev_spec, cur_spec, prev_spec, cur_spec],
        out_specs=[out_spec, out_spec],
        out_shape=[jax.ShapeDtypeStruct((batch, seq, DIL_OUT), BF16), jax.ShapeDtypeStruct((batch, seq, DIL_OUT), F32)],
        scratch_shapes=[pltpu.VMEM((halves, cur_rows, LANES), F32),
                        pltpu.VMEM((halves, prev_rows + cur_rows, LANES), F32),
                        pltpu.VMEM((halves, prev_rows + cur_rows, LANES), F32),
                        pltpu.VMEM((halves, cur_rows, LANES), F32),
                        pltpu.VMEM((halves, cur_rows, LANES), F32)],
        compiler_params=_cparams(("parallel", "arbitrary")),
        name=f"dilated_g{group}",
    )(view(qa), view(ka), view(ka), view(va), view(va))
    return o.reshape(batch * seq, DIL_OUT), l.reshape(batch * seq, DIL_OUT)


def _stickbreak_kernel(q_ref, k_ref, v_ref, o_ref, acc_ref, csum_ref, terms_ref, lb_ref):
    step = pl.program_id(2)
    tq = SB_TQ
    lane = lax.broadcasted_iota(I32, (1, LANES), 1)
    row = lax.broadcasted_iota(I32, (2 * tq, tq), 0)
    causal = lax.broadcasted_iota(I32, (2 * tq, tq), 1) < jnp.where(row >= tq, row - tq, row)
    later = (lax.broadcasted_iota(I32, (tq, tq), 0) > lax.broadcasted_iota(I32, (tq, tq), 1))
    later = jnp.where(later, 1.0, 0.0).astype(BF16)
    later2 = jnp.concatenate([later, later], axis=0)

    def stacked_queries(u):
        q = q_ref[u * tq:(u + 1) * tq, :]
        zero = jnp.zeros_like(q)
        return jnp.concatenate([jnp.where(lane < HEAD_DIM, q, zero), jnp.where(lane >= HEAD_DIM, q, zero)], axis=0)

    def emit(u, acc):
        o_ref[u * tq:(u + 1) * tq, :] = jnp.where(lane < HEAD_DIM, acc[:tq], acc[tq:]).astype(o_ref.dtype)


    def raw_scores(qs, j):
        start = pl.multiple_of(j * tq, tq)
        k = k_ref[pl.ds(start, tq), :]
        return lax.dot_general(qs, k, (((1,), (1,)), ((), ())), preferred_element_type=F32)

    def log_terms(z, diagonal):
        sp = jnp.maximum(z, 0.0) + jnp.log(1.0 + jnp.exp(-jnp.abs(z)))
        log_not = jnp.where(causal, -sp, 0.0) if diagonal else -sp
        log_beta = jnp.where(causal, z - sp, NEG_INF) if diagonal else z - sp
        hi = log_not.astype(BF16)
        terms_ref[:, :tq] = hi
        terms_ref[:, tq:] = (log_not - hi.astype(F32)).astype(BF16)
        csum = csum_ref[...]
        lb_ref[...] = log_beta + csum
        csum = csum + jnp.sum(log_not, axis=1, keepdims=True)
        csum_ref[...] = csum
        return jnp.max(csum)

    def later_sums():
        return jnp.dot(terms_ref[...], later2, preferred_element_type=F32)

    def weighted_values(j, inner):
        start = pl.multiple_of(j * tq, tq)
        v = v_ref[pl.ds(start, tq), :]
        w = jnp.exp(lb_ref[...] + inner)
        return jnp.dot(w.astype(BF16), v, preferred_element_type=F32)

    def cond(state):
        j, top = state
        return (j >= 0) & (top > SB_LOG_FLOOR)

    pending = None
    for u in range(SB_SUB):
        i = step * SB_SUB + u
        qs = stacked_queries(u)
        if pending is None:
            z = raw_scores(qs, i)
            pv = None
        else:
            inner = later_sums()
            z = raw_scores(qs, i)
            pv = weighted_values(pending[1] + 1, inner)
        csum_ref[...] = jnp.zeros_like(csum_ref)
        acc_ref[u] = jnp.zeros(acc_ref.shape[1:], F32)
        top = log_terms(z, True)
        if pending is not None:
            emit(pending[0], acc_ref[pending[0]] + pv)

        def body(state, qs=qs, u=u):
            j, _ = state
            inner = later_sums()
            z = raw_scores(qs, j)
            pv = weighted_values(j + 1, inner)
            top = log_terms(z, False)
            acc_ref[u] += pv
            return j - 1, top

        last, _ = lax.while_loop(cond, body, (i - 1, top))
        pending = (u, last)
    emit(pending[0], acc_ref[pending[0]] + weighted_values(pending[1] + 1, later_sums()))


def _stickbreak(qb, kb, vb, batch, seq):
    view = lambda a: a.reshape(batch, seq, SB_W)
    rows = SB_TQ * SB_SUB
    pairs = SB_W // LANES
    kv_spec = pl.BlockSpec((None, seq, LANES), lambda b, hp, i: (b, 0, hp))
    q_spec = pl.BlockSpec((None, rows, LANES), lambda b, hp, i: (b, i, hp))
    out = pl.pallas_call(
        _stickbreak_kernel,
        grid=(batch, pairs, seq // rows),
        in_specs=[q_spec, kv_spec, kv_spec],
        out_specs=q_spec,
        out_shape=jax.ShapeDtypeStruct((batch, seq, SB_W), BF16),
        scratch_shapes=[pltpu.VMEM((SB_SUB, 2 * SB_TQ, LANES), F32), pltpu.VMEM((2 * SB_TQ, 1), F32),
                        pltpu.VMEM((2 * SB_TQ, 2 * SB_TQ), BF16), pltpu.VMEM((2 * SB_TQ, SB_TQ), F32)],
        compiler_params=_cparams(("parallel", "parallel", "arbitrary")),
        name="stickbreak",
    )(view(qb), view(kb), view(vb))
    return out.reshape(batch * seq, SB_W)


def _postmix_kernel(o0_ref, o1_ref, o2_ref, l0_ref, l1_ref, l2_ref, sb_ref, ga_ref, gb_ref, x_ref,
                    mod_ref, gpost_ref, gffn_ref, wa_ref, wb_ref, wo_ref, wr_ref, br_ref,
                    x1_ref, h2_ref, idx_ref, gate_ref, rank_ref, cnt_ref, carry_ref, before_ref):
    step = pl.program_id(0)
    tm = x_ref.shape[0]

    @pl.when(step == 0)
    def _():
        carry_ref[...] = jnp.zeros_like(carry_ref)
        earlier = lax.broadcasted_iota(I32, (tm, tm), 0) < lax.broadcasted_iota(I32, (tm, tm), 1)
        before_ref[...] = jnp.where(earlier, 1.0, 0.0).astype(BF16)

    l0, l1, l2 = l0_ref[...], l1_ref[...], l2_ref[...]
    m = jnp.maximum(jnp.maximum(l0, l1), l2)
    e0, e1, e2 = jnp.exp(l0 - m), jnp.exp(l1 - m), jnp.exp(l2 - m)
    o0, o1, o2 = (ref[...].astype(F32) for ref in (o0_ref, o1_ref, o2_ref))
    merged = (e0 * o0 + e1 * o1 + e2 * o2) / (e0 + e1 + e2)
    ya = jnp.dot(merged.astype(BF16), wa_ref[...], preferred_element_type=F32)
    yb = jnp.dot(sb_ref[...], wb_ref[...], preferred_element_type=F32)
    mix = ga_ref[...] * ya.astype(BF16) + gb_ref[...] * yb.astype(BF16)
    y = jnp.dot(mix, wo_ref[...], preferred_element_type=F32)
    x1 = x_ref[...] + _rms(y, gpost_ref[...] * mod_ref[0, 2:3, :])
    x1_ref[...] = x1
    h2 = _rms(x1, gffn_ref[...] * (1.0 + mod_ref[0, 4:5, :])) + mod_ref[0, 3:4, :]
    h2_ref[...] = _pack_bf16_pairs(h2)

    def split(a):
        hi = a.astype(BF16)
        return hi, (a - hi.astype(F32)).astype(BF16)

    w_hi, w_lo = split(wr_ref[...])
    h_hi, h_lo = split(h2)
    logits = lax.dot_general(jnp.concatenate([w_hi, w_hi, w_lo], axis=1), jnp.concatenate([h_hi, h_lo, h_hi], axis=1),
                             (((1,), (1,)), ((), ())), preferred_element_type=F32) + br_ref[...]
    e_iota = lax.broadcasted_iota(I32, (N_EXPERTS, tm), 0)

    picks, vals = [], []
    work = logits
    chosen = jnp.zeros((N_EXPERTS, tm), F32)
    for _ in range(TOP_K):
        top = jnp.max(work, axis=0, keepdims=True)
        idx = jnp.min(jnp.where(work == top, e_iota, N_EXPERTS), axis=0, keepdims=True)
        one = e_iota == idx
        picks.append((idx, one))
        vals.append(top)
        chosen = jnp.where(one, 1.0, chosen)
        work = jnp.where(one, -jnp.inf, work)
    exps = [jnp.exp(v - vals[0]) for v in vals]
    den = exps[0] + exps[1] + exps[2] + exps[3]

    rank = jnp.dot(chosen.astype(BF16), before_ref[...], preferred_element_type=F32) + carry_ref[...]
    carry_ref[...] += jnp.sum(chosen, axis=1, keepdims=True)
    cnt_ref[...] = carry_ref[...].astype(I32)
    for kk, (idx, one) in enumerate(picks):
        idx_ref[kk:kk + 1, :] = idx
        gate_ref[kk:kk + 1, :] = exps[kk] / den
        rank_ref[kk:kk + 1, :] = jnp.sum(jnp.where(one, rank, 0.0), axis=0, keepdims=True).astype(I32)


def _post_mix(dil, sb, ga, gb, x2, mod3, gpost, gffn, wa, wb, wo, w_router, b_router, seq, part):
    t_all, d = x2.shape
    t = t_all // MOE_PARTS
    tm = ROW_BLOCK
    nb = t // tm
    off = part * nb
    per_seq = seq // tm
    row_in = lambda i: (i + off, 0)
    row = lambda i: (i, 0)
    const = lambda i: (0, 0)
    colblk = lambda i: (0, i)
    (o0, l0), (o1, l1), (o2, l2) = dil
    in_specs = ([pl.BlockSpec((tm, DIL_OUT), row_in)] * 6
                + [pl.BlockSpec((tm, SB_W), row_in), pl.BlockSpec((tm, d), row_in), pl.BlockSpec((tm, d), row_in),
                   pl.BlockSpec((tm, d), row_in),
                   pl.BlockSpec((1, 6, d), lambda i: ((i + off) // per_seq, 0, 0)),
                   pl.BlockSpec((1, d), const), pl.BlockSpec((1, d), const),
                   pl.BlockSpec(wa.shape, const), pl.BlockSpec(wb.shape, const), pl.BlockSpec(wo.shape, const),
                   pl.BlockSpec((N_EXPERTS, d), const), pl.BlockSpec((N_EXPERTS, 1), const)])
    out_specs = [pl.BlockSpec((tm, d), row), pl.BlockSpec((tm, d // 2), row),
                 pl.BlockSpec((TOP_K, tm), colblk), pl.BlockSpec((TOP_K, tm), colblk),
                 pl.BlockSpec((TOP_K, tm), colblk), pl.BlockSpec((N_EXPERTS, 1), const)]
    out_shape = [jax.ShapeDtypeStruct((t, d), F32), jax.ShapeDtypeStruct((t, d // 2), U32),
                 jax.ShapeDtypeStruct((TOP_K, t), I32), jax.ShapeDtypeStruct((TOP_K, t), F32),
                 jax.ShapeDtypeStruct((TOP_K, t), I32), jax.ShapeDtypeStruct((N_EXPERTS, 1), I32)]
    return pl.pallas_call(
        _postmix_kernel,
        grid=(nb,),
        in_specs=in_specs,
        out_specs=out_specs,
        out_shape=out_shape,
        scratch_shapes=[pltpu.VMEM((N_EXPERTS, 1), F32), pltpu.VMEM((tm, tm), BF16)],
        compiler_params=_cparams(("arbitrary",)),
        name="post_mix",
    )(o0, o1, o2, l0, l1, l2, sb, ga, gb, x2, mod3, gpost.reshape(1, d), gffn.reshape(1, d),
      wa, wb, wo, w_router.T, b_router.reshape(N_EXPERTS, 1))


def _slots_kernel(start_ref, idx_ref, rank_ref, slot_ref):
    idx = idx_ref[...]
    base = jnp.zeros_like(idx)
    for e in range(N_EXPERTS):
        base = jnp.where(idx == e, start_ref[e], base)
    slot_ref[...] = base + rank_ref[...]


def _slots(pad_start, idx, rank):
    k, t = idx.shape
    tb = min(t, 8192)
    blk = pl.BlockSpec((k, tb), lambda i, s: (0, i))
    return pl.pallas_call(
        _slots_kernel,
        grid_spec=pltpu.PrefetchScalarGridSpec(
            num_scalar_prefetch=1, grid=(t // tb,), in_specs=[blk, blk], out_specs=blk),
        out_shape=jax.ShapeDtypeStruct((k, t), I32),
        compiler_params=_cparams(("parallel",)),
        name="slots",
    )(pad_start, idx, rank)


def _sc_mesh():
    return plsc.VectorSubcoreMesh(core_axis_name="c", subcore_axis_name="s")


def _sc_dispatch(rows, slot, n_slots):
    t, d = rows.shape
    chunk = SC_DISPATCH_CHUNK
    per_worker = t // SC_WORKERS
    n_chunks = per_worker // chunk

    @functools.partial(
        pl.kernel, mesh=_sc_mesh(),
        out_type=jax.ShapeDtypeStruct((n_slots, d), rows.dtype),
        scratch_types=[pltpu.VMEM((TOP_K, chunk), I32), pltpu.VMEM((chunk, d), rows.dtype),
                       pltpu.SemaphoreType.DMA],
        name="dispatch",
    )
    def run(rows_hbm, slot_hbm, out_hbm, idx_v, rows_v, sem):
        wid = lax.axis_index("s") * SC_CORES + lax.axis_index("c")

        @pl.loop(0, n_chunks)
        def _(ci):
            base = pl.multiple_of(wid * per_worker + ci * chunk, chunk)
            loads = [pltpu.make_async_copy(rows_hbm.at[pl.ds(base, chunk)], rows_v, sem)]
            loads += [pltpu.make_async_copy(slot_hbm.at[kk, pl.ds(base, chunk)], idx_v.at[kk], sem)
                      for kk in range(TOP_K)]
            for cp in loads:
                cp.start()
            for cp in loads:
                cp.wait()
            copies = [pltpu.make_async_copy(rows_v, out_hbm.at[idx_v.at[kk]], sem) for kk in range(TOP_K)]
            for cp in copies:
                cp.start()
            for cp in copies:
                cp.wait()

    return run(rows, slot)


def _sc_combine(ys, slot):
    _, d = ys.shape
    k, t = slot.shape
    chunk = SC_COMBINE_CHUNK
    per_worker = t // SC_WORKERS
    n_chunks = per_worker // chunk

    @functools.partial(
        pl.kernel, mesh=_sc_mesh(),
        out_type=jax.ShapeDtypeStruct((k, t, d), ys.dtype),
        scratch_types=[pltpu.VMEM((TOP_K, chunk), I32), pltpu.VMEM((TOP_K, chunk, d), ys.dtype),
                       pltpu.SemaphoreType.DMA],
        name="combine",
    )
    def run(ys_hbm, slot_hbm, out_hbm, idx_v, rows_v, sem):
        wid = lax.axis_index("s") * SC_CORES + lax.axis_index("c")

        @pl.loop(0, n_chunks)
        def _(ci):
            base = pl.multiple_of(wid * per_worker + ci * chunk, chunk)
            loads = [pltpu.make_async_copy(slot_hbm.at[kk, pl.ds(base, chunk)], idx_v.at[kk], sem)
                     for kk in range(TOP_K)]
            for cp in loads:
                cp.start()
            for cp in loads:
                cp.wait()
            gathers = [pltpu.make_async_copy(ys_hbm.at[idx_v.at[kk]], rows_v.at[kk], sem) for kk in range(TOP_K)]
            for cp in gathers:
                cp.start()
            for cp in gathers:
                cp.wait()
            stores = [pltpu.make_async_copy(rows_v.at[kk], out_hbm.at[kk, pl.ds(base, chunk)], sem)
                      for kk in range(TOP_K)]
            for cp in stores:
                cp.start()
            for cp in stores:
                cp.wait()

    return run(ys, slot)


def _experts_kernel(be_ref, nused_ref, x_ref, *refs, prepared):
    if prepared:
        glu_w_ref, lin_w_ref, bg_ref, bl_ref, down_w_ref, bd_ref, y_ref = refs
    else:
        wu_ref, bg_ref, bl_ref, wd_ref, bd_ref, y_ref, glu_w_ref, lin_w_ref, down_w_ref = refs
    i = pl.program_id(0)
    live = i < nused_ref[0]

    if not prepared:
        new_expert = (i == 0) | (be_ref[i] != be_ref[jnp.maximum(i - 1, 0)])

        @pl.when(live & new_expert)
        def _():
            tile = 2 * LANES
            src = lax.broadcasted_iota(I32, (tile, tile), 0)
            dst = lax.broadcasted_iota(I32, (tile, tile), 1)
            wanted = jnp.where(dst < LANES, 2 * dst, 2 * (dst - LANES) + 1)
            pick = jnp.where(src == wanted, 1.0, 0.0).astype(BF16)
            for c in range(wu_ref.shape[2] // tile):
                w = wu_ref[0, :, c * tile:(c + 1) * tile].astype(BF16)
                both = jnp.dot(w, pick, preferred_element_type=F32)
                glu_w_ref[0, :, c * LANES:(c + 1) * LANES] = both[:, :LANES].astype(BF16)
                lin_w_ref[0, :, c * LANES:(c + 1) * LANES] = both[:, LANES:].astype(BF16)
            down_w_ref[0] = wd_ref[0].astype(BF16)

    @pl.when(live)
    def _():
        lo, hi = _unpack_bf16_pairs(x_ref[...])
        x = jnp.concatenate([lo, hi], axis=1).astype(BF16)
        glu = jnp.dot(x, glu_w_ref[0], preferred_element_type=F32) + bg_ref[0]
        lin = jnp.dot(x, lin_w_ref[0], preferred_element_type=F32) + bl_ref[0]
        glu = jnp.minimum(glu, SWIGLU_LIMIT)
        lin = jnp.clip(lin, -SWIGLU_LIMIT, SWIGLU_LIMIT)
        act = glu * jax.nn.sigmoid(SWIGLU_ALPHA * glu) * (lin + 1.0)
        y = jnp.dot(act.astype(BF16), down_w_ref[0], preferred_element_type=F32) + bd_ref[0]
        y_ref[...] = _pack_bf16_pairs(y)


def _experts(xs, block_expert, n_used, b_glu, b_lin, b_down, raw=None, prepared=None):
    n_slots, half_d = xs.shape
    e, _, f = b_glu.shape
    d = b_down.shape[2]
    nblk = n_slots // MOE_ROWS
    rows = lambda i, be, nu: (i, 0)
    by_e = lambda i, be, nu: (be[i], 0, 0)
    x_spec = pl.BlockSpec((MOE_ROWS, half_d), rows)
    bias_f, bias_d = pl.BlockSpec((1, 1, f), by_e), pl.BlockSpec((1, 1, d), by_e)
    up_bf16, down_bf16 = pl.BlockSpec((1, d, f), by_e), pl.BlockSpec((1, f, d), by_e)
    ys_shape = jax.ShapeDtypeStruct((n_slots, half_d), U32)
    if prepared is not None:
        in_specs = [x_spec, up_bf16, up_bf16, bias_f, bias_f, down_bf16, bias_d]
        args = (xs, prepared[0], prepared[1], b_glu, b_lin, prepared[2], b_down)
        out_specs, out_shape = x_spec, ys_shape
    else:
        w_up, w_down = raw
        in_specs = [x_spec, pl.BlockSpec((1, d, 2 * f), by_e), bias_f, bias_f, pl.BlockSpec((1, f, d), by_e), bias_d]
        args = (xs, w_up, b_glu, b_lin, w_down, b_down)
        out_specs = [x_spec, up_bf16, up_bf16, down_bf16]
        out_shape = [ys_shape, jax.ShapeDtypeStruct((e, d, f), BF16), jax.ShapeDtypeStruct((e, d, f), BF16),
                     jax.ShapeDtypeStruct((e, f, d), BF16)]
    out = pl.pallas_call(
        functools.partial(_experts_kernel, prepared=prepared is not None),
        grid_spec=pltpu.PrefetchScalarGridSpec(
            num_scalar_prefetch=2, grid=(nblk,), in_specs=in_specs, out_specs=out_specs),
        out_shape=out_shape,
        compiler_params=_cparams(("arbitrary",)),
        name="experts",
    )(block_expert, n_used, *args)
    return out if prepared is not None else (out[0], tuple(out[1:]))


def _final_kernel(g_ref, gate_ref, x1_ref, mod_ref, gain_ref, *rest):
    o_ref = rest[-1]
    tm = x1_ref.shape[0]
    gates = gate_ref[...]
    padded = jnp.concatenate([gates, jnp.zeros((LANES - TOP_K, tm), F32)], axis=0)
    gate_cols = padded.T
    y_lo = jnp.zeros(g_ref.shape[1:], F32)
    y_hi = jnp.zeros(g_ref.shape[1:], F32)
    for kk in range(TOP_K):
        lo, hi = _unpack_bf16_pairs(g_ref[kk])
        y_lo = y_lo + lo * gate_cols[:, kk:kk + 1]
        y_hi = y_hi + hi * gate_cols[:, kk:kk + 1]
    y = jnp.concatenate([y_lo, y_hi], axis=1)
    o_ref[...] = x1_ref[...] + mod_ref[0, 5:6, :] * _rms(y, gain_ref[...])


def _final(g, gates, x1, mod3, gain, seq, part, out_so_far):
    t, d = x1.shape
    tm = ROW_BLOCK
    nb = t // tm
    off = part * nb
    per_seq = seq // tm
    row = lambda i: (i, 0)
    in_specs = [pl.BlockSpec((TOP_K, tm, d // 2), lambda i: (0, i, 0)),
                pl.BlockSpec((TOP_K, tm), lambda i: (0, i)),
                pl.BlockSpec((tm, d), row),
                pl.BlockSpec((1, 6, d), lambda i: ((i + off) // per_seq, 0, 0)),
                pl.BlockSpec((1, d), lambda i: (0, 0))]
    args = [g, gates, x1, mod3, gain.reshape(1, d)]
    aliases = {}
    if out_so_far is not None:
        in_specs.append(pl.BlockSpec(memory_space=pl.ANY))
        args.append(out_so_far)
        aliases = {len(args) - 1: 0}
    return pl.pallas_call(
        _final_kernel,
        grid=(nb,),
        in_specs=in_specs,
        out_specs=pl.BlockSpec((tm, d), lambda i: (i + off, 0)),
        out_shape=jax.ShapeDtypeStruct((t * MOE_PARTS, d), F32),
        input_output_aliases=aliases,
        compiler_params=_cparams(("parallel",)),
        name="final",
    )(*args)


def _layer(x2, c, positions, seq, ada_w, ada_b, norm_mix_pre, norm_mix_post, norm_ffn_pre, norm_ffn_post,
           w_in, w_branch_a, w_branch_b, w_out, w_router, b_router, w_up, b_up, w_down, b_down):
    t, d = x2.shape
    batch = t // seq
    mod3 = _adaln(c, ada_w, ada_b).reshape(batch, 6, d)

    qa, ka, va, qb, kb, vb, ga, gb = _in_proj(x2, mod3, norm_mix_pre, positions, w_in.astype(BF16), seq)
    dil = [_dilated_group(qa, ka, va, batch, seq, g) for g in range(len(DIL_GROUPS))]
    sb = _stickbreak(qb, kb, vb, batch, seq)
    wa, wb, wo = w_branch_a.astype(BF16), w_branch_b.astype(BF16), w_out.astype(BF16)
    f = w_up.shape[2] // 2
    b_glu, b_lin = b_up[:, 0::2].reshape(N_EXPERTS, 1, f), b_up[:, 1::2].reshape(N_EXPERTS, 1, f)
    b_down3 = b_down.reshape(N_EXPERTS, 1, d)
    nblk = (t // MOE_PARTS * TOP_K) // MOE_ROWS + N_EXPERTS

    routed = []
    for part in range(MOE_PARTS):
        x1, h2, idx, gates, rank, counts = _post_mix(
            dil, sb, ga, gb, x2, mod3, norm_mix_post, norm_ffn_pre, wa, wb, wo, w_router, b_router, seq, part)
        counts = counts.reshape(N_EXPERTS)
        padded = jnp.maximum((counts + MOE_ROWS - 1) // MOE_ROWS, 1) * MOE_ROWS
        pad_end = jnp.cumsum(padded)
        pad_start = (pad_end - padded).astype(I32)
        block_first_row = jnp.arange(nblk, dtype=I32)[:, None] * MOE_ROWS
        block_expert = jnp.minimum(jnp.sum(pad_end[None, :] <= block_first_row, axis=1), N_EXPERTS - 1).astype(I32)
        n_used = (pad_end[-1:] // MOE_ROWS).astype(I32)
        slot = _slots(pad_start, idx, rank)
        xs = _sc_dispatch(h2, slot, nblk * MOE_ROWS)
        routed.append((x1, gates, slot, xs, block_expert, n_used))

    gathered = []
    prepared = None
    for x1, gates, slot, xs, block_expert, n_used in routed:
        if prepared is None:
            ys, prepared = _experts(xs, block_expert, n_used, b_glu, b_lin, b_down3, raw=(w_up, w_down))
        else:
            ys = _experts(xs, block_expert, n_used, b_glu, b_lin, b_down3, prepared=prepared)
        gathered.append(_sc_combine(ys, slot))

    out = None
    for part, ((x1, gates, *_), g) in enumerate(zip(routed, gathered)):
        out = _final(g, gates, x1, mod3, norm_ffn_post, seq, part, out)
    return out


def kernel(x, c, positions, ada_w, ada_b, norm_mix_pre, norm_mix_post, norm_ffn_pre, norm_ffn_post,
           w_in, w_branch_a, w_branch_b, w_out, w_router, b_router, w_up, b_up, w_down, b_down):
    batch, seq, d = x.shape
    tokens = batch * seq
    assert w_in.shape[-1] == 3 * DIL_W + 3 * SB_W + 2 * d and d % (2 * LANES) == 0
    assert w_router.shape[-1] == N_EXPERTS and w_up.shape[-1] == 2 * w_down.shape[-2]
    assert seq % ROW_BLOCK == 0 and seq % (SB_TQ * SB_SUB) == 0
    assert all(seq % max(Q_BLOCK * r, DIL_TOKENS) == 0 for _, r in DIL_GROUPS)
    assert tokens % (MOE_PARTS * ROW_BLOCK) == 0
    assert tokens % (MOE_PARTS * SC_WORKERS * max(SC_DISPATCH_CHUNK, SC_COMBINE_CHUNK)) == 0
    x2 = x.reshape(tokens, d)
    for layer in range(ada_w.shape[0]):
        x2 = _layer(x2, c, positions, seq, ada_w[layer], ada_b[layer], norm_mix_pre[layer], norm_mix_post[layer],
                    norm_ffn_pre[layer], norm_ffn_post[layer], w_in[layer], w_branch_a[layer], w_branch_b[layer],
                    w_out[layer], w_router[layer], b_router[layer], w_up[layer], b_up[layer], w_down[layer],
                    b_down[layer])
    return x2.reshape(batch, seq, d)
```

```python
import functools

import numpy as np
import jax
import jax.numpy as jnp
from jax import lax
from jax.experimental import pallas as pl
from jax.experimental.pallas import tpu as pltpu
from jax.experimental.pallas import tpu_sc as plsc

F32 = jnp.float32
BF16 = jnp.bfloat16
I32 = jnp.int32
U32 = jnp.uint32

HEAD_DIM = 64
DIL_GROUPS = ((128, 1), (512, 4), (2048, 16))
DIL_HEADS_PER_GROUP = 4
DIL_HEADS = DIL_HEADS_PER_GROUP * len(DIL_GROUPS)
DIL_W = DIL_HEADS * HEAD_DIM
DIL_OUT = DIL_HEADS_PER_GROUP * HEAD_DIM
SB_HEADS = 8
SB_W = SB_HEADS * HEAD_DIM
ROPE_THETA = 500000.0
ROPE_DIMS = HEAD_DIM // 4
Q_BLOCK = 128
N_EXPERTS = 32
TOP_K = 4
SWIGLU_ALPHA = 1.702
SWIGLU_LIMIT = 7.0
NORM_EPS = 1e-6
NEG_INF = -1e30

LANES = 128
ROW_BLOCK = 512
MOE_ROWS = 512
MOE_PARTS = 2
DIL_TOKENS = 1024
SB_TQ = 256
SB_SUB = 8
SB_LOG_FLOOR = -105.0
V7X_VMEM_BYTES = 64 * 1024 * 1024
VMEM_LIMIT = V7X_VMEM_BYTES * 7 // 8

SC_CORES = 2
SC_SUBCORES = 16
SC_WORKERS = SC_CORES * SC_SUBCORES
SC_DISPATCH_CHUNK = 64
SC_COMBINE_CHUNK = 32


def _cparams(sem):
    return pltpu.CompilerParams(dimension_semantics=sem, vmem_limit_bytes=VMEM_LIMIT)


def _rms(x, gain):
    ms = jnp.mean(x * x, axis=-1, keepdims=True)
    return x * lax.rsqrt(ms + NORM_EPS) * gain


def _pack_bf16_pairs(x):
    n = x.shape[1] // 2
    u = lax.bitcast_convert_type(x, U32)
    r = (u + jnp.uint32(0x7FFF) + ((u >> 16) & jnp.uint32(1))) >> 16
    return r[:, :n] | (r[:, n:] << 16)


def _unpack_bf16_pairs(w):
    lo = lax.bitcast_convert_type(w << 16, F32)
    hi = lax.bitcast_convert_type(w & jnp.uint32(0xFFFF0000), F32)
    return lo, hi


def _adaln_kernel(c_ref, w_ref, b_ref, o_ref):
    c = c_ref[...]
    s = c * jax.nn.sigmoid(c)
    o_ref[...] = jnp.dot(s, w_ref[...], preferred_element_type=F32,
                         precision=lax.Precision.HIGHEST) + b_ref[...]


def _adaln(c, ada_w, ada_b):
    b, d = c.shape
    n = ada_w.shape[1]
    return pl.pallas_call(
        _adaln_kernel,
        grid=(n // d,),
        in_specs=[pl.BlockSpec((b, d), lambda j: (0, 0)),
                  pl.BlockSpec((d, d), lambda j: (0, j)),
                  pl.BlockSpec((1, d), lambda j: (0, j))],
        out_specs=pl.BlockSpec((b, d), lambda j: (0, j)),
        out_shape=jax.ShapeDtypeStruct((b, n), F32),
        compiler_params=_cparams(("arbitrary",)),
        name="adaln",
    )(c, ada_w, ada_b.reshape(1, n))


def _rope_table():
    half = ROPE_DIMS // 2
    inv_freq = ROPE_THETA ** (-(np.arange(half, dtype=np.float32) * 2.0 / ROPE_DIMS))
    return jnp.asarray(np.broadcast_to(inv_freq[:, None], (half, LANES)).astype(np.float32))


def _inproj_kernel(x_ref, mod_ref, g_ref, pos_ref, rope_ref, w_ref,
                   qa_ref, ka_ref, va_ref, qb_ref, kb_ref, vb_ref, ga_ref, gb_ref):
    x = x_ref[...]
    tm = x.shape[0]
    shift = mod_ref[0, 0:1, :]
    scale = mod_ref[0, 1:2, :]
    hb = (_rms(x, g_ref[...]) * (1.0 + scale) + shift).astype(BF16)
    qk_scale = HEAD_DIM ** -0.5
    half = ROPE_DIMS // 2

    ang = rope_ref[:, 0:1] * pos_ref[0].astype(F32)
    cos_c, sin_c = jnp.cos(ang), jnp.sin(ang)
    rest = HEAD_DIM - ROPE_DIMS
    ones, zeros = jnp.ones((rest, tm), F32), jnp.zeros((rest, tm), F32)
    cos = jnp.concatenate([cos_c, cos_c, ones] * (LANES // HEAD_DIM), axis=0).T
    sin = jnp.concatenate([-sin_c, sin_c, zeros] * (LANES // HEAD_DIM), axis=0).T
    first = (lax.broadcasted_iota(I32, (1, LANES), 1) & (HEAD_DIM - 1)) < half

    def rope(t):
        def one(tile):
            partner = jnp.where(first, pltpu.roll(tile, LANES - half, 1), pltpu.roll(tile, half, 1))
            return tile * cos + partner * sin
        return jnp.concatenate([one(t[:, :LANES]), one(t[:, LANES:])], axis=1)

    def project(ref, col, width, finish):
        for c0 in range(0, width, 2 * LANES):
            t = jnp.dot(hb, w_ref[:, col + c0:col + c0 + 2 * LANES], preferred_element_type=F32)
            ref[:, c0:c0 + 2 * LANES] = finish(t).astype(ref.dtype)

    d = x.shape[1]
    col = 0
    for ref, width, finish in ((qa_ref, DIL_W, lambda t: rope(t) * qk_scale), (ka_ref, DIL_W, rope),
                               (va_ref, DIL_W, lambda t: t), (qb_ref, SB_W, lambda t: t * qk_scale),
                               (kb_ref, SB_W, lambda t: t), (vb_ref, SB_W, lambda t: t),
                               (ga_ref, d, jax.nn.sigmoid), (gb_ref, d, jax.nn.sigmoid)):
        project(ref, col, width, finish)
        col += width


def _in_proj(x2, mod3, gain, positions, w_in_bf16, seq):
    t, d = x2.shape
    tm = ROW_BLOCK
    nb = t // tm
    per_seq = seq // tm
    pos3 = positions.reshape(nb, 1, tm)
    widths = (DIL_W, DIL_W, DIL_W, SB_W, SB_W, SB_W, d, d)
    row = lambda i: (i, 0)
    return pl.pallas_call(
        _inproj_kernel,
        grid=(nb,),
        in_specs=[pl.BlockSpec((tm, d), row),
                  pl.BlockSpec((1, 6, d), lambda i: (i // per_seq, 0, 0)),
                  pl.BlockSpec((1, d), lambda i: (0, 0)),
                  pl.BlockSpec((1, 1, tm), lambda i: (i, 0, 0)),
                  pl.BlockSpec((8, LANES), lambda i: (0, 0)),
                  pl.BlockSpec(w_in_bf16.shape, lambda i: (0, 0))],
        out_specs=[pl.BlockSpec((tm, w), row) for w in widths],
        out_shape=[jax.ShapeDtypeStruct((t, w), BF16) for w in widths],
        compiler_params=_cparams(("parallel",)),
        name="in_proj",
    )(x2, mod3, gain.reshape(1, d), pos3, _rope_table(), w_in_bf16)


def _dilated_kernel(q_ref, kp_ref, kc_ref, vp_ref, vc_ref, o_ref, l_ref,
                    qf_ref, kf_ref, vf_ref, of_ref, lf_ref, *, r, nq):
    n = pl.program_id(1)
    prev_rows = Q_BLOCK * r
    halves = DIL_OUT // LANES
    for c in range(halves):
        cols = slice(c * LANES, (c + 1) * LANES)
        qf_ref[c] = q_ref[:, cols].astype(F32)
        kf_ref[c, 0:prev_rows, :] = kp_ref[:, cols].astype(F32)
        kf_ref[c, prev_rows:, :] = kc_ref[:, cols].astype(F32)
        vf_ref[c, 0:prev_rows, :] = vp_ref[:, cols].astype(F32)
        vf_ref[c, prev_rows:, :] = vc_ref[:, cols].astype(F32)

    def gather_rows(ref, start, size):
        idx = pl.ds(start, size, stride=r) if r > 1 else pl.ds(start, size)
        return jnp.concatenate([ref[c, idx, :] for c in range(halves)], axis=1).astype(BF16)

    def scatter_rows(ref, start, val):
        idx = pl.ds(start, Q_BLOCK, stride=r) if r > 1 else pl.ds(start, Q_BLOCK)
        for c in range(halves):
            ref[c, idx, :] = val[:, c * LANES:(c + 1) * LANES]

    qi = lax.broadcasted_iota(I32, (Q_BLOCK, 2 * Q_BLOCK), 0) + Q_BLOCK
    ki = lax.broadcasted_iota(I32, (Q_BLOCK, 2 * Q_BLOCK), 1)
    dist = qi - ki
    band = (dist >= 0) & (dist <= Q_BLOCK)
    lane = lax.broadcasted_iota(I32, (1, DIL_OUT), 1)

    heads = [(lane >= h * HEAD_DIM) & (lane < (h + 1) * HEAD_DIM) for h in range(DIL_HEADS_PER_GROUP)]

    def sub_block(idx, carry):
        rho = idx % r
        qb = idx // r
        base = r * Q_BLOCK * qb + rho
        q = gather_rows(qf_ref, base, Q_BLOCK)
        k = gather_rows(kf_ref, base, 2 * Q_BLOCK)
        v = gather_rows(vf_ref, base, 2 * Q_BLOCK)
        valid = band & ((n * nq + qb - 1) * Q_BLOCK + ki >= 0)
        qs = jnp.concatenate([jnp.where(heads[h], q, jnp.zeros_like(q)) for h in range(DIL_HEADS_PER_GROUP)], axis=0)
        s = lax.dot_general(qs, k, (((1,), (1,)), ((), ())), preferred_element_type=F32)
        s = jnp.where(jnp.concatenate([valid] * DIL_HEADS_PER_GROUP, axis=0), s, NEG_INF)
        m = jnp.max(s, axis=1, keepdims=True)
        p = jnp.exp(s - m).astype(BF16)
        pv = jnp.dot(p, v, preferred_element_type=F32)
        den = jnp.sum(p.astype(F32), axis=1, keepdims=True)
        inv = 1.0 / den
        lse = m + jnp.log(den)
        o = jnp.zeros((Q_BLOCK, DIL_OUT), F32)
        l = jnp.zeros((Q_BLOCK, DIL_OUT), F32)
        for h in range(DIL_HEADS_PER_GROUP):
            blk = slice(h * Q_BLOCK, (h + 1) * Q_BLOCK)
            o = jnp.where(heads[h], pv[blk] * inv[blk], o)
            l = jnp.where(heads[h], lse[blk], l)
        scatter_rows(of_ref, base, o)
        scatter_rows(lf_ref, base, l)
        return carry

    lax.fori_loop(0, r * nq, sub_block, 0, unroll=8)
    for c in range(halves):
        o_ref[:, c * LANES:(c + 1) * LANES] = of_ref[c].astype(o_ref.dtype)
        l_ref[:, c * LANES:(c + 1) * LANES] = lf_ref[c]


def _dilated_group(qa, ka, va, batch, seq, group):
    _, r = DIL_GROUPS[group]
    nq = max(1, DIL_TOKENS // (Q_BLOCK * r))
    prev_rows = Q_BLOCK * r
    cur_rows = prev_rows * nq
    halves = DIL_OUT // LANES
    view = lambda a: a.reshape(batch, seq, DIL_W)
    cur = lambda b, n: (b, n, group)
    prev = lambda b, n: (b, jnp.maximum(n * nq - 1, 0), group)
    cur_spec = pl.BlockSpec((None, cur_rows, DIL_OUT), cur)
    prev_spec = pl.BlockSpec((None, prev_rows, DIL_OUT), prev)
    out_spec = pl.BlockSpec((None, cur_rows, DIL_OUT), lambda b, n: (b, n, 0))
    o, l = pl.pallas_call(
        functools.partial(_dilated_kernel, r=r, nq=nq),
        grid=(batch, seq // cur_rows),
        in_specs=[cur_spec, prev_spec, cur_spec, prev_spec, cur_spec],
        out_specs=[out_spec, out_spec],
        out_shape=[jax.ShapeDtypeStruct((batch, seq, DIL_OUT), BF16), jax.ShapeDtypeStruct((batch, seq, DIL_OUT), F32)],
        scratch_shapes=[pltpu.VMEM((halves, cur_rows, LANES), F32),
                        pltpu.VMEM((halves, prev_rows + cur_rows, LANES), F32),
                        pltpu.VMEM((halves, prev_rows + cur_rows, LANES), F32),
                        pltpu.VMEM((halves, cur_rows, LANES), F32),
                        pltpu.VMEM((halves, cur_rows, LANES), F32)],
        compiler_params=_cparams(("parallel", "arbitrary")),
        name=f"dilated_g{group}",
    )(view(qa), view(ka), view(ka), view(va), view(va))
    return o.reshape(batch * seq, DIL_OUT), l.reshape(batch * seq, DIL_OUT)


def _stickbreak_kernel(q_ref, k_ref, v_ref, o_ref, acc_ref, csum_ref, terms_ref, lb_ref):
    step = pl.program_id(2)
    tq = SB_TQ
    lane = lax.broadcasted_iota(I32, (1, LANES), 1)
    row = lax.broadcasted_iota(I32, (2 * tq, tq), 0)
    causal = lax.broadcasted_iota(I32, (2 * tq, tq), 1) < jnp.where(row >= tq, row - tq, row)
    later = (lax.broadcasted_iota(I32, (tq, tq), 0) > lax.broadcasted_iota(I32, (tq, tq), 1))
    later = jnp.where(later, 1.0, 0.0).astype(BF16)
    later2 = jnp.concatenate([later, later], axis=0)

    def stacked_queries(u):
        q = q_ref[u * tq:(u + 1) * tq, :]
        zero = jnp.zeros_like(q)
        return jnp.concatenate([jnp.where(lane < HEAD_DIM, q, zero), jnp.where(lane >= HEAD_DIM, q, zero)], axis=0)

    def emit(u, acc):
        o_ref[u * tq:(u + 1) * tq, :] = jnp.where(lane < HEAD_DIM, acc[:tq], acc[tq:]).astype(o_ref.dtype)


    def raw_scores(qs, j):
        start = pl.multiple_of(j * tq, tq)
        k = k_ref[pl.ds(start, tq), :]
        return lax.dot_general(qs, k, (((1,), (1,)), ((), ())), preferred_element_type=F32)

    def log_terms(z, diagonal):
        sp = jnp.maximum(z, 0.0) + jnp.log(1.0 + jnp.exp(-jnp.abs(z)))
        log_not = jnp.where(causal, -sp, 0.0) if diagonal else -sp
        log_beta = jnp.where(causal, z - sp, NEG_INF) if diagonal else z - sp
        hi = log_not.astype(BF16)
        terms_ref[:, :tq] = hi
        terms_ref[:, tq:] = (log_not - hi.astype(F32)).astype(BF16)
        csum = csum_ref[...]
        lb_ref[...] = log_beta + csum
        csum = csum + jnp.sum(log_not, axis=1, keepdims=True)
        csum_ref[...] = csum
        return jnp.max(csum)

    def later_sums():
        return jnp.dot(terms_ref[...], later2, preferred_element_type=F32)

    def weighted_values(j, inner):
        start = pl.multiple_of(j * tq, tq)
        v = v_ref[pl.ds(start, tq), :]
        w = jnp.exp(lb_ref[...] + inner)
        return jnp.dot(w.astype(BF16), v, preferred_element_type=F32)

    def cond(state):
        j, top = state
        return (j >= 0) & (top > SB_LOG_FLOOR)

    pending = None
    for u in range(SB_SUB):
        i = step * SB_SUB + u
        qs = stacked_queries(u)
        if pending is None:
            z = raw_scores(qs, i)
            pv = None
        else:
            inner = later_sums()
            z = raw_scores(qs, i)
            pv = weighted_values(pending[1] + 1, inner)
        csum_ref[...] = jnp.zeros_like(csum_ref)
        acc_ref[u] = jnp.zeros(acc_ref.shape[1:], F32)
        top = log_terms(z, True)
        if pending is not None:
            emit(pending[0], acc_ref[pending[0]] + pv)

        def body(state, qs=qs, u=u):
            j, _ = state
            inner = later_sums()
            z = raw_scores(qs, j)
            pv = weighted_values(j + 1, inner)
            top = log_terms(z, False)
            acc_ref[u] += pv
            return j - 1, top

        last, _ = lax.while_loop(cond, body, (i - 1, top))
        pending = (u, last)
    emit(pending[0], acc_ref[pending[0]] + weighted_values(pending[1] + 1, later_sums()))


def _stickbreak(qb, kb, vb, batch, seq):
    view = lambda a: a.reshape(batch, seq, SB_W)
    rows = SB_TQ * SB_SUB
    pairs = SB_W // LANES
    kv_spec = pl.BlockSpec((None, seq, LANES), lambda b, hp, i: (b, 0, hp))
    q_spec = pl.BlockSpec((None, rows, LANES), lambda b, hp, i: (b, i, hp))
    out = pl.pallas_call(
        _stickbreak_kernel,
        grid=(batch, pairs, seq // rows),
        in_specs=[q_spec, kv_spec, kv_spec],
        out_specs=q_spec,
        out_shape=jax.ShapeDtypeStruct((batch, seq, SB_W), BF16),
        scratch_shapes=[pltpu.VMEM((SB_SUB, 2 * SB_TQ, LANES), F32), pltpu.VMEM((2 * SB_TQ, 1), F32),
                        pltpu.VMEM((2 * SB_TQ, 2 * SB_TQ), BF16), pltpu.VMEM((2 * SB_TQ, SB_TQ), F32)],
        compiler_params=_cparams(("parallel", "parallel", "arbitrary")),
        name="stickbreak",
    )(view(qb), view(kb), view(vb))
    return out.reshape(batch * seq, SB_W)


def _postmix_kernel(o0_ref, o1_ref, o2_ref, l0_ref, l1_ref, l2_ref, sb_ref, ga_ref, gb_ref, x_ref,
                    mod_ref, gpost_ref, gffn_ref, wa_ref, wb_ref, wo_ref, wr_ref, br_ref,
                    x1_ref, h2_ref, idx_ref, gate_ref, rank_ref, cnt_ref, carry_ref, before_ref):
    step = pl.program_id(0)
    tm = x_ref.shape[0]

    @pl.when(step == 0)
    def _():
        carry_ref[...] = jnp.zeros_like(carry_ref)
        earlier = lax.broadcasted_iota(I32, (tm, tm), 0) < lax.broadcasted_iota(I32, (tm, tm), 1)
        before_ref[...] = jnp.where(earlier, 1.0, 0.0).astype(BF16)

    l0, l1, l2 = l0_ref[...], l1_ref[...], l2_ref[...]
    m = jnp.maximum(jnp.maximum(l0, l1), l2)
    e0, e1, e2 = jnp.exp(l0 - m), jnp.exp(l1 - m), jnp.exp(l2 - m)
    o0, o1, o2 = (ref[...].astype(F32) for ref in (o0_ref, o1_ref, o2_ref))
    merged = (e0 * o0 + e1 * o1 + e2 * o2) / (e0 + e1 + e2)
    ya = jnp.dot(merged.astype(BF16), wa_ref[...], preferred_element_type=F32)
    yb = jnp.dot(sb_ref[...], wb_ref[...], preferred_element_type=F32)
    mix = ga_ref[...] * ya.astype(BF16) + gb_ref[...] * yb.astype(BF16)
    y = jnp.dot(mix, wo_ref[...], preferred_element_type=F32)
    x1 = x_ref[...] + _rms(y, gpost_ref[...] * mod_ref[0, 2:3, :])
    x1_ref[...] = x1
    h2 = _rms(x1, gffn_ref[...] * (1.0 + mod_ref[0, 4:5, :])) + mod_ref[0, 3:4, :]
    h2_ref[...] = _pack_bf16_pairs(h2)

    def split(a):
        hi = a.astype(BF16)
        return hi, (a - hi.astype(F32)).astype(BF16)

    w_hi, w_lo = split(wr_ref[...])
    h_hi, h_lo = split(h2)
    logits = lax.dot_general(jnp.concatenate([w_hi, w_hi, w_lo], axis=1), jnp.concatenate([h_hi, h_lo, h_hi], axis=1),
                             (((1,), (1,)), ((), ())), preferred_element_type=F32) + br_ref[...]
    e_iota = lax.broadcasted_iota(I32, (N_EXPERTS, tm), 0)

    picks, vals = [], []
    work = logits
    chosen = jnp.zeros((N_EXPERTS, tm), F32)
    for _ in range(TOP_K):
        top = jnp.max(work, axis=0, keepdims=True)
        idx = jnp.min(jnp.where(work == top, e_iota, N_EXPERTS), axis=0, keepdims=True)
        one = e_iota == idx
        picks.append((idx, one))
        vals.append(top)
        chosen = jnp.where(one, 1.0, chosen)
        work = jnp.where(one, -jnp.inf, work)
    exps = [jnp.exp(v - vals[0]) for v in vals]
    den = exps[0] + exps[1] + exps[2] + exps[3]

    rank = jnp.dot(chosen.astype(BF16), before_ref[...], preferred_element_type=F32) + carry_ref[...]
    carry_ref[...] += jnp.sum(chosen, axis=1, keepdims=True)
    cnt_ref[...] = carry_ref[...].astype(I32)
    for kk, (idx, one) in enumerate(picks):
        idx_ref[kk:kk + 1, :] = idx
        gate_ref[kk:kk + 1, :] = exps[kk] / den
        rank_ref[kk:kk + 1, :] = jnp.sum(jnp.where(one, rank, 0.0), axis=0, keepdims=True).astype(I32)


def _post_mix(dil, sb, ga, gb, x2, mod3, gpost, gffn, wa, wb, wo, w_router, b_router, seq, part):
    t_all, d = x2.shape
    t = t_all // MOE_PARTS
    tm = ROW_BLOCK
    nb = t // tm
    off = part * nb
    per_seq = seq // tm
    row_in = lambda i: (i + off, 0)
    row = lambda i: (i, 0)
    const = lambda i: (0, 0)
    colblk = lambda i: (0, i)
    (o0, l0), (o1, l1), (o2, l2) = dil
    in_specs = ([pl.BlockSpec((tm, DIL_OUT), row_in)] * 6
                + [pl.BlockSpec((tm, SB_W), row_in), pl.BlockSpec((tm, d), row_in), pl.BlockSpec((tm, d), row_in),
                   pl.BlockSpec((tm, d), row_in),
                   pl.BlockSpec((1, 6, d), lambda i: ((i + off) // per_seq, 0, 0)),
                   pl.BlockSpec((1, d), const), pl.BlockSpec((1, d), const),
                   pl.BlockSpec(wa.shape, const), pl.BlockSpec(wb.shape, const), pl.BlockSpec(wo.shape, const),
                   pl.BlockSpec((N_EXPERTS, d), const), pl.BlockSpec((N_EXPERTS, 1), const)])
    out_specs = [pl.BlockSpec((tm, d), row), pl.BlockSpec((tm, d // 2), row),
                 pl.BlockSpec((TOP_K, tm), colblk), pl.BlockSpec((TOP_K, tm), colblk),
                 pl.BlockSpec((TOP_K, tm), colblk), pl.BlockSpec((N_EXPERTS, 1), const)]
    out_shape = [jax.ShapeDtypeStruct((t, d), F32), jax.ShapeDtypeStruct((t, d // 2), U32),
                 jax.ShapeDtypeStruct((TOP_K, t), I32), jax.ShapeDtypeStruct((TOP_K, t), F32),
                 jax.ShapeDtypeStruct((TOP_K, t), I32), jax.ShapeDtypeStruct((N_EXPERTS, 1), I32)]
    return pl.pallas_call(
        _postmix_kernel,
        grid=(nb,),
        in_specs=in_specs,
        out_specs=out_specs,
        out_shape=out_shape,
        scratch_shapes=[pltpu.VMEM((N_EXPERTS, 1), F32), pltpu.VMEM((tm, tm), BF16)],
        compiler_params=_cparams(("arbitrary",)),
        name="post_mix",
    )(o0, o1, o2, l0, l1, l2, sb, ga, gb, x2, mod3, gpost.reshape(1, d), gffn.reshape(1, d),
      wa, wb, wo, w_router.T, b_router.reshape(N_EXPERTS, 1))


def _slots_kernel(start_ref, idx_ref, rank_ref, slot_ref):
    idx = idx_ref[...]
    base = jnp.zeros_like(idx)
    for e in range(N_EXPERTS):
        base = jnp.where(idx == e, start_ref[e], base)
    slot_ref[...] = base + rank_ref[...]


def _slots(pad_start, idx, rank):
    k, t = idx.shape
    tb = min(t, 8192)
    blk = pl.BlockSpec((k, tb), lambda i, s: (0, i))
    return pl.pallas_call(
        _slots_kernel,
        grid_spec=pltpu.PrefetchScalarGridSpec(
            num_scalar_prefetch=1, grid=(t // tb,), in_specs=[blk, blk], out_specs=blk),
        out_shape=jax.ShapeDtypeStruct((k, t), I32),
        compiler_params=_cparams(("parallel",)),
        name="slots",
    )(pad_start, idx, rank)


def _sc_mesh():
    return plsc.VectorSubcoreMesh(core_axis_name="c", subcore_axis_name="s")


def _sc_dispatch(rows, slot, n_slots):
    t, d = rows.shape
    chunk = SC_DISPATCH_CHUNK
    per_worker = t // SC_WORKERS
    n_chunks = per_worker // chunk

    @functools.partial(
        pl.kernel, mesh=_sc_mesh(),
        out_type=jax.ShapeDtypeStruct((n_slots, d), rows.dtype),
        scratch_types=[pltpu.VMEM((TOP_K, chunk), I32), pltpu.VMEM((chunk, d), rows.dtype),
                       pltpu.SemaphoreType.DMA],
        name="dispatch",
    )
    def run(rows_hbm, slot_hbm, out_hbm, idx_v, rows_v, sem):
        wid = lax.axis_index("s") * SC_CORES + lax.axis_index("c")

        @pl.loop(0, n_chunks)
        def _(ci):
            base = pl.multiple_of(wid * per_worker + ci * chunk, chunk)
            loads = [pltpu.make_async_copy(rows_hbm.at[pl.ds(base, chunk)], rows_v, sem)]
            loads += [pltpu.make_async_copy(slot_hbm.at[kk, pl.ds(base, chunk)], idx_v.at[kk], sem)
                      for kk in range(TOP_K)]
            for cp in loads:
                cp.start()
            for cp in loads:
                cp.wait()
            copies = [pltpu.make_async_copy(rows_v, out_hbm.at[idx_v.at[kk]], sem) for kk in range(TOP_K)]
            for cp in copies:
                cp.start()
            for cp in copies:
                cp.wait()

    return run(rows, slot)


def _sc_combine(ys, slot):
    _, d = ys.shape
    k, t = slot.shape
    chunk = SC_COMBINE_CHUNK
    per_worker = t // SC_WORKERS
    n_chunks = per_worker // chunk

    @functools.partial(
        pl.kernel, mesh=_sc_mesh(),
        out_type=jax.ShapeDtypeStruct((k, t, d), ys.dtype),
        scratch_types=[pltpu.VMEM((TOP_K, chunk), I32), pltpu.VMEM((TOP_K, chunk, d), ys.dtype),
                       pltpu.SemaphoreType.DMA],
        name="combine",
    )
    def run(ys_hbm, slot_hbm, out_hbm, idx_v, rows_v, sem):
        wid = lax.axis_index("s") * SC_CORES + lax.axis_index("c")

        @pl.loop(0, n_chunks)
        def _(ci):
            base = pl.multiple_of(wid * per_worker + ci * chunk, chunk)
            loads = [pltpu.make_async_copy(slot_hbm.at[kk, pl.ds(base, chunk)], idx_v.at[kk], sem)
                     for kk in range(TOP_K)]
            for cp in loads:
                cp.start()
            for cp in loads:
                cp.wait()
            gathers = [pltpu.make_async_copy(ys_hbm.at[idx_v.at[kk]], rows_v.at[kk], sem) for kk in range(TOP_K)]
            for cp in gathers:
                cp.start()
            for cp in gathers:
                cp.wait()
            stores = [pltpu.make_async_copy(rows_v.at[kk], out_hbm.at[kk, pl.ds(base, chunk)], sem)
                      for kk in range(TOP_K)]
            for cp in stores:
                cp.start()
            for cp in stores:
                cp.wait()

    return run(ys, slot)


def _experts_kernel(be_ref, nused_ref, x_ref, *refs, prepared):
    if prepared:
        glu_hbm, lin_hbm, bg_ref, bl_ref, down_hbm, bd_ref, y_ref, glu_buf, lin_buf, down_buf, sem = refs
        streams = ((glu_hbm, glu_buf), (lin_hbm, lin_buf), (down_hbm, down_buf))
    else:
        wu_hbm, bg_ref, bl_ref, wd_hbm, bd_ref, y_ref, glu_w_ref, lin_w_ref, down_w_ref, wu_buf, wd_buf, sem = refs
        streams = ((wu_hbm, wu_buf), (wd_hbm, wd_buf))
    i = pl.program_id(0)
    live = i < nused_ref[0]
    e = be_ref[i]
    slot = e % 2
    new_expert = (i == 0) | (e != be_ref[jnp.maximum(i - 1, 0)])

    def copies(expert, into):
        return [pltpu.make_async_copy(hbm.at[expert], buf.at[into], sem.at[into, n])
                for n, (hbm, buf) in enumerate(streams)]

    @pl.when(i == 0)
    def _():
        for cp in copies(e, slot):
            cp.start()

    @pl.when(live & new_expert)
    def _():
        for cp in copies(e, slot):
            cp.wait()

        @pl.when(e + 1 < N_EXPERTS)
        def _():
            for cp in copies(e + 1, 1 - slot):
                cp.start()

        if not prepared:
            tile = 2 * LANES
            src = lax.broadcasted_iota(I32, (tile, tile), 0)
            dst = lax.broadcasted_iota(I32, (tile, tile), 1)
            wanted = jnp.where(dst < LANES, 2 * dst, 2 * (dst - LANES) + 1)
            pick = jnp.where(src == wanted, 1.0, 0.0).astype(BF16)
            for c in range(wu_buf.shape[2] // tile):
                w = wu_buf[slot, :, c * tile:(c + 1) * tile].astype(BF16)
                both = jnp.dot(w, pick, preferred_element_type=F32)
                glu_w_ref[0, :, c * LANES:(c + 1) * LANES] = both[:, :LANES].astype(BF16)
                lin_w_ref[0, :, c * LANES:(c + 1) * LANES] = both[:, LANES:].astype(BF16)
            down_w_ref[0] = wd_buf[slot].astype(BF16)

    @pl.when(live)
    def _():
        if prepared:
            glu_w, lin_w, down_w = glu_buf[slot], lin_buf[slot], down_buf[slot]
        else:
            glu_w, lin_w, down_w = glu_w_ref[0], lin_w_ref[0], down_w_ref[0]
        lo, hi = _unpack_bf16_pairs(x_ref[...])
        x = jnp.concatenate([lo, hi], axis=1).astype(BF16)
        glu = jnp.dot(x, glu_w, preferred_element_type=F32) + bg_ref[0]
        lin = jnp.dot(x, lin_w, preferred_element_type=F32) + bl_ref[0]
        glu = jnp.minimum(glu, SWIGLU_LIMIT)
        lin = jnp.clip(lin, -SWIGLU_LIMIT, SWIGLU_LIMIT)
        act = glu * jax.nn.sigmoid(SWIGLU_ALPHA * glu) * (lin + 1.0)
        y = jnp.dot(act.astype(BF16), down_w, preferred_element_type=F32) + bd_ref[0]
        y_ref[...] = _pack_bf16_pairs(y)


def _experts(xs, block_expert, n_used, b_glu, b_lin, b_down, raw=None, prepared=None):
    n_slots, half_d = xs.shape
    e, _, f = b_glu.shape
    d = b_down.shape[2]
    nblk = n_slots // MOE_ROWS
    rows = lambda i, be, nu: (i, 0)
    by_e = lambda i, be, nu: (be[i], 0, 0)
    x_spec = pl.BlockSpec((MOE_ROWS, half_d), rows)
    bias_f, bias_d = pl.BlockSpec((1, 1, f), by_e), pl.BlockSpec((1, 1, d), by_e)
    up_bf16, down_bf16 = pl.BlockSpec((1, d, f), by_e), pl.BlockSpec((1, f, d), by_e)
    ys_shape = jax.ShapeDtypeStruct((n_slots, half_d), U32)
    in_hbm = pl.BlockSpec(memory_space=pl.ANY)
    if prepared is not None:
        in_specs = [x_spec, in_hbm, in_hbm, bias_f, bias_f, in_hbm, bias_d]
        args = (xs, prepared[0], prepared[1], b_glu, b_lin, prepared[2], b_down)
        out_specs, out_shape = x_spec, ys_shape
        scratch = [pltpu.VMEM((2, d, f), BF16), pltpu.VMEM((2, d, f), BF16), pltpu.VMEM((2, f, d), BF16),
                   pltpu.SemaphoreType.DMA((2, 3))]
    else:
        w_up, w_down = raw
        in_specs = [x_spec, in_hbm, bias_f, bias_f, in_hbm, bias_d]
        args = (xs, w_up, b_glu, b_lin, w_down, b_down)
        out_specs = [x_spec, up_bf16, up_bf16, down_bf16]
        out_shape = [ys_shape, jax.ShapeDtypeStruct((e, d, f), BF16), jax.ShapeDtypeStruct((e, d, f), BF16),
                     jax.ShapeDtypeStruct((e, f, d), BF16)]
        scratch = [pltpu.VMEM((2, d, 2 * f), F32), pltpu.VMEM((2, f, d), F32), pltpu.SemaphoreType.DMA((2, 2))]
    out = pl.pallas_call(
        functools.partial(_experts_kernel, prepared=prepared is not None),
        grid_spec=pltpu.PrefetchScalarGridSpec(
            num_scalar_prefetch=2, grid=(nblk,), in_specs=in_specs, out_specs=out_specs, scratch_shapes=scratch),
        out_shape=out_shape,
        compiler_params=_cparams(("arbitrary",)),
        name="experts",
    )(block_expert, n_used, *args)
    return out if prepared is not None else (out[0], tuple(out[1:]))


def _final_kernel(g_ref, gate_ref, x1_ref, mod_ref, gain_ref, *rest):
    o_ref = rest[-1]
    tm = x1_ref.shape[0]
    gates = gate_ref[...]
    padded = jnp.concatenate([gates, jnp.zeros((LANES - TOP_K, tm), F32)], axis=0)
    gate_cols = padded.T
    y_lo = jnp.zeros(g_ref.shape[1:], F32)
    y_hi = jnp.zeros(g_ref.shape[1:], F32)
    for kk in range(TOP_K):
        lo, hi = _unpack_bf16_pairs(g_ref[kk])
        y_lo = y_lo + lo * gate_cols[:, kk:kk + 1]
        y_hi = y_hi + hi * gate_cols[:, kk:kk + 1]
    y = jnp.concatenate([y_lo, y_hi], axis=1)
    o_ref[...] = x1_ref[...] + mod_ref[0, 5:6, :] * _rms(y, gain_ref[...])


def _final(g, gates, x1, mod3, gain, seq, part, out_so_far):
    t, d = x1.shape
    tm = ROW_BLOCK
    nb = t // tm
    off = part * nb
    per_seq = seq // tm
    row = lambda i: (i, 0)
    in_specs = [pl.BlockSpec((TOP_K, tm, d // 2), lambda i: (0, i, 0)),
                pl.BlockSpec((TOP_K, tm), lambda i: (0, i)),
                pl.BlockSpec((tm, d), row),
                pl.BlockSpec((1, 6, d), lambda i: ((i + off) // per_seq, 0, 0)),
                pl.BlockSpec((1, d), lambda i: (0, 0))]
    args = [g, gates, x1, mod3, gain.reshape(1, d)]
    aliases = {}
    if out_so_far is not None:
        in_specs.append(pl.BlockSpec(memory_space=pl.ANY))
        args.append(out_so_far)
        aliases = {len(args) - 1: 0}
    return pl.pallas_call(
        _final_kernel,
        grid=(nb,),
        in_specs=in_specs,
        out_specs=pl.BlockSpec((tm, d), lambda i: (i + off, 0)),
        out_shape=jax.ShapeDtypeStruct((t * MOE_PARTS, d), F32),
        input_output_aliases=aliases,
        compiler_params=_cparams(("parallel",)),
        name="final",
    )(*args)


def _layer(x2, c, positions, seq, ada_w, ada_b, norm_mix_pre, norm_mix_post, norm_ffn_pre, norm_ffn_post,
           w_in, w_branch_a, w_branch_b, w_out, w_router, b_router, w_up, b_up, w_down, b_down):
    t, d = x2.shape
    batch = t // seq
    mod3 = _adaln(c, ada_w, ada_b).reshape(batch, 6, d)

    qa, ka, va, qb, kb, vb, ga, gb = _in_proj(x2, mod3, norm_mix_pre, positions, w_in.astype(BF16), seq)
    dil = [_dilated_group(qa, ka, va, batch, seq, g) for g in range(len(DIL_GROUPS))]
    sb = _stickbreak(qb, kb, vb, batch, seq)
    wa, wb, wo = w_branch_a.astype(BF16), w_branch_b.astype(BF16), w_out.astype(BF16)
    f = w_up.shape[2] // 2
    b_glu, b_lin = b_up[:, 0::2].reshape(N_EXPERTS, 1, f), b_up[:, 1::2].reshape(N_EXPERTS, 1, f)
    b_down3 = b_down.reshape(N_EXPERTS, 1, d)
    nblk = (t // MOE_PARTS * TOP_K) // MOE_ROWS + N_EXPERTS

    routed = []
    for part in range(MOE_PARTS):
        x1, h2, idx, gates, rank, counts = _post_mix(
            dil, sb, ga, gb, x2, mod3, norm_mix_post, norm_ffn_pre, wa, wb, wo, w_router, b_router, seq, part)
        counts = counts.reshape(N_EXPERTS)
        padded = jnp.maximum((counts + MOE_ROWS - 1) // MOE_ROWS, 1) * MOE_ROWS
        pad_end = jnp.cumsum(padded)
        pad_start = (pad_end - padded).astype(I32)
        block_first_row = jnp.arange(nblk, dtype=I32)[:, None] * MOE_ROWS
        block_expert = jnp.minimum(jnp.sum(pad_end[None, :] <= block_first_row, axis=1), N_EXPERTS - 1).astype(I32)
        n_used = (pad_end[-1:] // MOE_ROWS).astype(I32)
        slot = _slots(pad_start, idx, rank)
        xs = _sc_dispatch(h2, slot, nblk * MOE_ROWS)
        routed.append((x1, gates, slot, xs, block_expert, n_used))

    gathered = []
    prepared = None
    for x1, gates, slot, xs, block_expert, n_used in routed:
        if prepared is None:
            ys, prepared = _experts(xs, block_expert, n_used, b_glu, b_lin, b_down3, raw=(w_up, w_down))
        else:
            ys = _experts(xs, block_expert, n_used, b_glu, b_lin, b_down3, prepared=prepared)
        gathered.append(_sc_combine(ys, slot))

    out = None
    for part, ((x1, gates, *_), g) in enumerate(zip(routed, gathered)):
        out = _final(g, gates, x1, mod3, norm_ffn_post, seq, part, out)
    return out


def kernel(x, c, positions, ada_w, ada_b, norm_mix_pre, norm_mix_post, norm_ffn_pre, norm_ffn_post,
           w_in, w_branch_a, w_branch_b, w_out, w_router, b_router, w_up, b_up, w_down, b_down):
    batch, seq, d = x.shape
    tokens = batch * seq
    assert w_in.shape[-1] == 3 * DIL_W + 3 * SB_W + 2 * d and d % (2 * LANES) == 0
    assert w_router.shape[-1] == N_EXPERTS and w_up.shape[-1] == 2 * w_down.shape[-2]
    assert seq % ROW_BLOCK == 0 and seq % (SB_TQ * SB_SUB) == 0
    assert all(seq % max(Q_BLOCK * r, DIL_TOKENS) == 0 for _, r in DIL_GROUPS)
    assert tokens % (MOE_PARTS * ROW_BLOCK) == 0
    assert tokens % (MOE_PARTS * SC_WORKERS * max(SC_DISPATCH_CHUNK, SC_COMBINE_CHUNK)) == 0
    x2 = x.reshape(tokens, d)
    for layer in range(ada_w.shape[0]):
        x2 = _layer(x2, c, positions, seq, ada_w[layer], ada_b[layer], norm_mix_pre[layer], norm_mix_post[layer],
                    norm_ffn_pre[layer], norm_ffn_post[layer], w_in[layer], w_branch_a[layer], w_branch_b[layer],
                    w_out[layer], w_router[layer], b_router[layer], w_up[layer], b_up[layer], w_down[layer],
                    b_down[layer])
    return x2.reshape(batch, seq, d)
```

```python
import functools

import numpy as np
import jax
import jax.numpy as jnp
from jax import lax
from jax.experimental import pallas as pl
from jax.experimental.pallas import tpu as pltpu
from jax.experimental.pallas import tpu_sc as plsc

F32 = jnp.float32
BF16 = jnp.bfloat16
I32 = jnp.int32
U32 = jnp.uint32

HEAD_DIM = 64
DIL_GROUPS = ((128, 1), (512, 4), (2048, 16))
DIL_HEADS_PER_GROUP = 4
DIL_HEADS = DIL_HEADS_PER_GROUP * len(DIL_GROUPS)
DIL_W = DIL_HEADS * HEAD_DIM
DIL_OUT = DIL_HEADS_PER_GROUP * HEAD_DIM
SB_HEADS = 8
SB_W = SB_HEADS * HEAD_DIM
ROPE_THETA = 500000.0
ROPE_DIMS = HEAD_DIM // 4
Q_BLOCK = 128
N_EXPERTS = 32
TOP_K = 4
SWIGLU_ALPHA = 1.702
SWIGLU_LIMIT = 7.0
NORM_EPS = 1e-6
NEG_INF = -1e30

LANES = 128
ROW_BLOCK = 512
MOE_ROWS = 512
MOE_PARTS = 2
DIL_TOKENS = 1024
SB_TQ = 256
SB_SUB = 8
SB_LOG_FLOOR = -105.0
V7X_VMEM_BYTES = 64 * 1024 * 1024
VMEM_LIMIT = V7X_VMEM_BYTES * 7 // 8

SC_CORES = 2
SC_SUBCORES = 16
SC_WORKERS = SC_CORES * SC_SUBCORES
SC_DISPATCH_CHUNK = 64
SC_COMBINE_CHUNK = 32


def _cparams(sem):
    return pltpu.CompilerParams(dimension_semantics=sem, vmem_limit_bytes=VMEM_LIMIT)


def _rms(x, gain):
    ms = jnp.mean(x * x, axis=-1, keepdims=True)
    return x * lax.rsqrt(ms + NORM_EPS) * gain


def _pack_bf16_pairs(x):
    n = x.shape[1] // 2
    u = lax.bitcast_convert_type(x, U32)
    r = (u + jnp.uint32(0x7FFF) + ((u >> 16) & jnp.uint32(1))) >> 16
    return r[:, :n] | (r[:, n:] << 16)


def _unpack_bf16_pairs(w):
    lo = lax.bitcast_convert_type(w << 16, F32)
    hi = lax.bitcast_convert_type(w & jnp.uint32(0xFFFF0000), F32)
    return lo, hi


def _adaln_kernel(c_ref, w_ref, b_ref, o_ref):
    c = c_ref[...]
    s = c * jax.nn.sigmoid(c)
    o_ref[...] = jnp.dot(s, w_ref[...], preferred_element_type=F32,
                         precision=lax.Precision.HIGHEST) + b_ref[...]


def _adaln(c, ada_w, ada_b):
    b, d = c.shape
    n = ada_w.shape[1]
    return pl.pallas_call(
        _adaln_kernel,
        grid=(n // d,),
        in_specs=[pl.BlockSpec((b, d), lambda j: (0, 0)),
                  pl.BlockSpec((d, d), lambda j: (0, j)),
                  pl.BlockSpec((1, d), lambda j: (0, j))],
        out_specs=pl.BlockSpec((b, d), lambda j: (0, j)),
        out_shape=jax.ShapeDtypeStruct((b, n), F32),
        compiler_params=_cparams(("arbitrary",)),
        name="adaln",
    )(c, ada_w, ada_b.reshape(1, n))


def _rope_table():
    half = ROPE_DIMS // 2
    inv_freq = ROPE_THETA ** (-(np.arange(half, dtype=np.float32) * 2.0 / ROPE_DIMS))
    return jnp.asarray(np.broadcast_to(inv_freq[:, None], (half, LANES)).astype(np.float32))


def _inproj_kernel(x_ref, mod_ref, g_ref, pos_ref, rope_ref, w_ref,
                   qa_ref, ka_ref, va_ref, qb_ref, kb_ref, vb_ref, ga_ref, gb_ref):
    x = x_ref[...]
    tm = x.shape[0]
    shift = mod_ref[0, 0:1, :]
    scale = mod_ref[0, 1:2, :]
    hb = (_rms(x, g_ref[...]) * (1.0 + scale) + shift).astype(BF16)
    qk_scale = HEAD_DIM ** -0.5
    half = ROPE_DIMS // 2

    ang = rope_ref[:, 0:1] * pos_ref[0].astype(F32)
    cos_c, sin_c = jnp.cos(ang), jnp.sin(ang)
    rest = HEAD_DIM - ROPE_DIMS
    ones, zeros = jnp.ones((rest, tm), F32), jnp.zeros((rest, tm), F32)
    cos = jnp.concatenate([cos_c, cos_c, ones] * (LANES // HEAD_DIM), axis=0).T
    sin = jnp.concatenate([-sin_c, sin_c, zeros] * (LANES // HEAD_DIM), axis=0).T
    first = (lax.broadcasted_iota(I32, (1, LANES), 1) & (HEAD_DIM - 1)) < half

    def rope(t):
        def one(tile):
            partner = jnp.where(first, pltpu.roll(tile, LANES - half, 1), pltpu.roll(tile, half, 1))
            return tile * cos + partner * sin
        return jnp.concatenate([one(t[:, :LANES]), one(t[:, LANES:])], axis=1)

    def project(ref, col, width, finish):
        for c0 in range(0, width, 2 * LANES):
            t = jnp.dot(hb, w_ref[:, col + c0:col + c0 + 2 * LANES], preferred_element_type=F32)
            ref[:, c0:c0 + 2 * LANES] = finish(t).astype(ref.dtype)

    d = x.shape[1]
    col = 0
    for ref, width, finish in ((qa_ref, DIL_W, lambda t: rope(t) * qk_scale), (ka_ref, DIL_W, rope),
                               (va_ref, DIL_W, lambda t: t), (qb_ref, SB_W, lambda t: t * qk_scale),
                               (kb_ref, SB_W, lambda t: t), (vb_ref, SB_W, lambda t: t),
                               (ga_ref, d, jax.nn.sigmoid), (gb_ref, d, jax.nn.sigmoid)):
        project(ref, col, width, finish)
        col += width


def _in_proj(x2, mod3, gain, positions, w_in_bf16, seq):
    t, d = x2.shape
    tm = ROW_BLOCK
    nb = t // tm
    per_seq = seq // tm
    pos3 = positions.reshape(nb, 1, tm)
    widths = (DIL_W, DIL_W, DIL_W, SB_W, SB_W, SB_W, d, d)
    row = lambda i: (i, 0)
    return pl.pallas_call(
        _inproj_kernel,
        grid=(nb,),
        in_specs=[pl.BlockSpec((tm, d), row),
                  pl.BlockSpec((1, 6, d), lambda i: (i // per_seq, 0, 0)),
                  pl.BlockSpec((1, d), lambda i: (0, 0)),
                  pl.BlockSpec((1, 1, tm), lambda i: (i, 0, 0)),
                  pl.BlockSpec((8, LANES), lambda i: (0, 0)),
                  pl.BlockSpec(w_in_bf16.shape, lambda i: (0, 0))],
        out_specs=[pl.BlockSpec((tm, w), row) for w in widths],
        out_shape=[jax.ShapeDtypeStruct((t, w), BF16) for w in widths],
        compiler_params=_cparams(("parallel",)),
        name="in_proj",
    )(x2, mod3, gain.reshape(1, d), pos3, _rope_table(), w_in_bf16)


def _dilated_kernel(q_ref, kp_ref, kc_ref, vp_ref, vc_ref, o_ref, l_ref,
                    qf_ref, kf_ref, vf_ref, of_ref, lf_ref, *, r, nq):
    n = pl.program_id(1)
    prev_rows = Q_BLOCK * r
    halves = DIL_OUT // LANES
    for c in range(halves):
        cols = slice(c * LANES, (c + 1) * LANES)
        qf_ref[c] = q_ref[:, cols].astype(F32)
        kf_ref[c, 0:prev_rows, :] = kp_ref[:, cols].astype(F32)
        kf_ref[c, prev_rows:, :] = kc_ref[:, cols].astype(F32)
        vf_ref[c, 0:prev_rows, :] = vp_ref[:, cols].astype(F32)
        vf_ref[c, prev_rows:, :] = vc_ref[:, cols].astype(F32)

    def gather_rows(ref, start, size):
        idx = pl.ds(start, size, stride=r) if r > 1 else pl.ds(start, size)
        return jnp.concatenate([ref[c, idx, :] for c in range(halves)], axis=1).astype(BF16)

    def scatter_rows(ref, start, val):
        idx = pl.ds(start, Q_BLOCK, stride=r) if r > 1 else pl.ds(start, Q_BLOCK)
        for c in range(halves):
            ref[c, idx, :] = val[:, c * LANES:(c + 1) * LANES]

    qi = lax.broadcasted_iota(I32, (Q_BLOCK, 2 * Q_BLOCK), 0) + Q_BLOCK
    ki = lax.broadcasted_iota(I32, (Q_BLOCK, 2 * Q_BLOCK), 1)
    dist = qi - ki
    band = (dist >= 0) & (dist <= Q_BLOCK)
    lane = lax.broadcasted_iota(I32, (1, DIL_OUT), 1)

    heads = [(lane >= h * HEAD_DIM) & (lane < (h + 1) * HEAD_DIM) for h in range(DIL_HEADS_PER_GROUP)]

    def sub_block(idx, carry):
        rho = idx % r
        qb = idx // r
        base = r * Q_BLOCK * qb + rho
        q = gather_rows(qf_ref, base, Q_BLOCK)
        k = gather_rows(kf_ref, base, 2 * Q_BLOCK)
        v = gather_rows(vf_ref, base, 2 * Q_BLOCK)
        valid = band & ((n * nq + qb - 1) * Q_BLOCK + ki >= 0)
        qs = jnp.concatenate([jnp.where(heads[h], q, jnp.zeros_like(q)) for h in range(DIL_HEADS_PER_GROUP)], axis=0)
        s = lax.dot_general(qs, k, (((1,), (1,)), ((), ())), preferred_element_type=F32)
        s = jnp.where(jnp.concatenate([valid] * DIL_HEADS_PER_GROUP, axis=0), s, NEG_INF)
        m = jnp.max(s, axis=1, keepdims=True)
        p = jnp.exp(s - m).astype(BF16)
        pv = jnp.dot(p, v, preferred_element_type=F32)
        den = jnp.sum(p.astype(F32), axis=1, keepdims=True)
        inv = 1.0 / den
        lse = m + jnp.log(den)
        o = jnp.zeros((Q_BLOCK, DIL_OUT), F32)
        l = jnp.zeros((Q_BLOCK, DIL_OUT), F32)
        for h in range(DIL_HEADS_PER_GROUP):
            blk = slice(h * Q_BLOCK, (h + 1) * Q_BLOCK)
            o = jnp.where(heads[h], pv[blk] * inv[blk], o)
            l = jnp.where(heads[h], lse[blk], l)
        scatter_rows(of_ref, base, o)
        scatter_rows(lf_ref, base, l)
        return carry

    lax.fori_loop(0, r * nq, sub_block, 0, unroll=8)
    for c in range(halves):
        o_ref[:, c * LANES:(c + 1) * LANES] = of_ref[c].astype(o_ref.dtype)
        l_ref[:, c * LANES:(c + 1) * LANES] = lf_ref[c]


def _dilated_group(qa, ka, va, batch, seq, group):
    _, r = DIL_GROUPS[group]
    nq = max(1, DIL_TOKENS // (Q_BLOCK * r))
    prev_rows = Q_BLOCK * r
    cur_rows = prev_rows * nq
    halves = DIL_OUT // LANES
    view = lambda a: a.reshape(batch, seq, DIL_W)
    cur = lambda b, n: (b, n, group)
    prev = lambda b, n: (b, jnp.maximum(n * nq - 1, 0), group)
    cur_spec = pl.BlockSpec((None, cur_rows, DIL_OUT), cur)
    prev_spec = pl.BlockSpec((None, prev_rows, DIL_OUT), prev)
    out_spec = pl.BlockSpec((None, cur_rows, DIL_OUT), lambda b, n: (b, n, 0))
    o, l = pl.pallas_call(
        functools.partial(_dilated_kernel, r=r, nq=nq),
        grid=(batch, seq // cur_rows),
        in_specs=[cur_spec, prev_spec, cur_spec, prev_spec, cur_spec],
        out_specs=[out_spec, out_spec],
        out_shape=[jax.ShapeDtypeStruct((batch, seq, DIL_OUT), BF16), jax.ShapeDtypeStruct((batch, seq, DIL_OUT), F32)],
        scratch_shapes=[pltpu.VMEM((halves, cur_rows, LANES), F32),
                        pltpu.VMEM((halves, prev_rows + cur_rows, LANES), F32),
                        pltpu.VMEM((halves, prev_rows + cur_rows, LANES), F32),
                        pltpu.VMEM((halves, cur_rows, LANES), F32),
                        pltpu.VMEM((halves, cur_rows, LANES), F32)],
        compiler_params=_cparams(("parallel", "arbitrary")),
        name=f"dilated_g{group}",
    )(view(qa), view(ka), view(ka), view(va), view(va))
    return o.reshape(batch * seq, DIL_OUT), l.reshape(batch * seq, DIL_OUT)


def _stickbreak_kernel(q_ref, k_ref, v_ref, o_ref, acc_ref, csum_ref, terms_ref, lb_ref):
    step = pl.program_id(2)
    tq = SB_TQ
    lane = lax.broadcasted_iota(I32, (1, LANES), 1)
    row = lax.broadcasted_iota(I32, (2 * tq, tq), 0)
    causal = lax.broadcasted_iota(I32, (2 * tq, tq), 1) < jnp.where(row >= tq, row - tq, row)
    later = (lax.broadcasted_iota(I32, (tq, tq), 0) > lax.broadcasted_iota(I32, (tq, tq), 1))
    later = jnp.where(later, 1.0, 0.0).astype(BF16)
    later2 = jnp.concatenate([later, later], axis=0)

    def stacked_queries(u):
        q = q_ref[u * tq:(u + 1) * tq, :]
        zero = jnp.zeros_like(q)
        return jnp.concatenate([jnp.where(lane < HEAD_DIM, q, zero), jnp.where(lane >= HEAD_DIM, q, zero)], axis=0)

    def emit(u, acc):
        o_ref[u * tq:(u + 1) * tq, :] = jnp.where(lane < HEAD_DIM, acc[:tq], acc[tq:]).astype(o_ref.dtype)


    def raw_scores(qs, j):
        start = pl.multiple_of(j * tq, tq)
        k = k_ref[pl.ds(start, tq), :]
        return lax.dot_general(qs, k, (((1,), (1,)), ((), ())), preferred_element_type=F32)

    def log_terms(z, diagonal):
        sp = jnp.maximum(z, 0.0) + jnp.log(1.0 + jnp.exp(-jnp.abs(z)))
        log_not = jnp.where(causal, -sp, 0.0) if diagonal else -sp
        log_beta = jnp.where(causal, z - sp, NEG_INF) if diagonal else z - sp
        hi = log_not.astype(BF16)
        terms_ref[:, :tq] = hi
        terms_ref[:, tq:] = (log_not - hi.astype(F32)).astype(BF16)
        csum = csum_ref[...]
        lb_ref[...] = log_beta + csum
        csum = csum + jnp.sum(log_not, axis=1, keepdims=True)
        csum_ref[...] = csum
        return jnp.max(csum)

    def later_sums():
        return jnp.dot(terms_ref[...], later2, preferred_element_type=F32)

    def weighted_values(j, inner):
        start = pl.multiple_of(j * tq, tq)
        v = v_ref[pl.ds(start, tq), :]
        w = jnp.exp(lb_ref[...] + inner)
        return jnp.dot(w.astype(BF16), v, preferred_element_type=F32)

    def cond(state):
        j, top = state
        return (j >= 0) & (top > SB_LOG_FLOOR)

    pending = None
    for u in range(SB_SUB):
        i = step * SB_SUB + u
        qs = stacked_queries(u)
        if pending is None:
            z = raw_scores(qs, i)
            pv = None
        else:
            inner = later_sums()
            z = raw_scores(qs, i)
            pv = weighted_values(pending[1] + 1, inner)
        csum_ref[...] = jnp.zeros_like(csum_ref)
        acc_ref[u] = jnp.zeros(acc_ref.shape[1:], F32)
        top = log_terms(z, True)
        if pending is not None:
            emit(pending[0], acc_ref[pending[0]] + pv)

        def body(state, qs=qs, u=u):
            j, _ = state
            inner = later_sums()
            z = raw_scores(qs, j)
            pv = weighted_values(j + 1, inner)
            top = log_terms(z, False)
            acc_ref[u] += pv
            return j - 1, top

        last, _ = lax.while_loop(cond, body, (i - 1, top))
        pending = (u, last)
    emit(pending[0], acc_ref[pending[0]] + weighted_values(pending[1] + 1, later_sums()))


def _stickbreak(qb, kb, vb, batch, seq):
    view = lambda a: a.reshape(batch, seq, SB_W)
    rows = SB_TQ * SB_SUB
    pairs = SB_W // LANES
    kv_spec = pl.BlockSpec((None, seq, LANES), lambda b, hp, i: (b, 0, hp))
    q_spec = pl.BlockSpec((None, rows, LANES), lambda b, hp, i: (b, i, hp))
    out = pl.pallas_call(
        _stickbreak_kernel,
        grid=(batch, pairs, seq // rows),
        in_specs=[q_spec, kv_spec, kv_spec],
        out_specs=q_spec,
        out_shape=jax.ShapeDtypeStruct((batch, seq, SB_W), BF16),
        scratch_shapes=[pltpu.VMEM((SB_SUB, 2 * SB_TQ, LANES), F32), pltpu.VMEM((2 * SB_TQ, 1), F32),
                        pltpu.VMEM((2 * SB_TQ, 2 * SB_TQ), BF16), pltpu.VMEM((2 * SB_TQ, SB_TQ), F32)],
        compiler_params=_cparams(("parallel", "parallel", "arbitrary")),
        name="stickbreak",
    )(view(qb), view(kb), view(vb))
    return out.reshape(batch * seq, SB_W)


def _postmix_kernel(o0_ref, o1_ref, o2_ref, l0_ref, l1_ref, l2_ref, sb_ref, ga_ref, gb_ref, x_ref,
                    mod_ref, gpost_ref, gffn_ref, wa_ref, wb_ref, wo_ref, wr_ref, br_ref,
                    x1_ref, h2_ref, idx_ref, gate_ref, rank_ref, cnt_ref, carry_ref, before_ref):
    step = pl.program_id(0)
    tm = x_ref.shape[0]

    @pl.when(step == 0)
    def _():
        carry_ref[...] = jnp.zeros_like(carry_ref)
        earlier = lax.broadcasted_iota(I32, (tm, tm), 0) < lax.broadcasted_iota(I32, (tm, tm), 1)
        before_ref[...] = jnp.where(earlier, 1.0, 0.0).astype(BF16)

    l0, l1, l2 = l0_ref[...], l1_ref[...], l2_ref[...]
    m = jnp.maximum(jnp.maximum(l0, l1), l2)
    e0, e1, e2 = jnp.exp(l0 - m), jnp.exp(l1 - m), jnp.exp(l2 - m)
    o0, o1, o2 = (ref[...].astype(F32) for ref in (o0_ref, o1_ref, o2_ref))
    merged = (e0 * o0 + e1 * o1 + e2 * o2) / (e0 + e1 + e2)
    ya = jnp.dot(merged.astype(BF16), wa_ref[...], preferred_element_type=F32)
    yb = jnp.dot(sb_ref[...], wb_ref[...], preferred_element_type=F32)
    mix = ga_ref[...] * ya.astype(BF16) + gb_ref[...] * yb.astype(BF16)
    y = jnp.dot(mix, wo_ref[...], preferred_element_type=F32)
    x1 = x_ref[...] + _rms(y, gpost_ref[...] * mod_ref[0, 2:3, :])
    x1_ref[...] = x1
    h2 = _rms(x1, gffn_ref[...] * (1.0 + mod_ref[0, 4:5, :])) + mod_ref[0, 3:4, :]
    h2_ref[...] = _pack_bf16_pairs(h2)

    def split(a):
        hi = a.astype(BF16)
        return hi, (a - hi.astype(F32)).astype(BF16)

    w_hi, w_lo = split(wr_ref[...])
    h_hi, h_lo = split(h2)
    logits = lax.dot_general(jnp.concatenate([w_hi, w_hi, w_lo], axis=1), jnp.concatenate([h_hi, h_lo, h_hi], axis=1),
                             (((1,), (1,)), ((), ())), preferred_element_type=F32) + br_ref[...]
    e_iota = lax.broadcasted_iota(I32, (N_EXPERTS, tm), 0)

    picks, vals = [], []
    work = logits
    chosen = jnp.zeros((N_EXPERTS, tm), F32)
    for _ in range(TOP_K):
        top = jnp.max(work, axis=0, keepdims=True)
        idx = jnp.min(jnp.where(work == top, e_iota, N_EXPERTS), axis=0, keepdims=True)
        one = e_iota == idx
        picks.append((idx, one))
        vals.append(top)
        chosen = jnp.where(one, 1.0, chosen)
        work = jnp.where(one, -jnp.inf, work)
    exps = [jnp.exp(v - vals[0]) for v in vals]
    den = exps[0] + exps[1] + exps[2] + exps[3]

    rank = jnp.dot(chosen.astype(BF16), before_ref[...], preferred_element_type=F32) + carry_ref[...]
    carry_ref[...] += jnp.sum(chosen, axis=1, keepdims=True)
    cnt_ref[...] = carry_ref[...].astype(I32)
    for kk, (idx, one) in enumerate(picks):
        idx_ref[kk:kk + 1, :] = idx
        gate_ref[kk:kk + 1, :] = exps[kk] / den
        rank_ref[kk:kk + 1, :] = jnp.sum(jnp.where(one, rank, 0.0), axis=0, keepdims=True).astype(I32)


def _post_mix(dil, sb, ga, gb, x2, mod3, gpost, gffn, wa, wb, wo, w_router, b_router, seq, part):
    t_all, d = x2.shape
    t = t_all // MOE_PARTS
    tm = ROW_BLOCK
    nb = t // tm
    off = part * nb
    per_seq = seq // tm
    row_in = lambda i: (i + off, 0)
    row = lambda i: (i, 0)
    const = lambda i: (0, 0)
    colblk = lambda i: (0, i)
    (o0, l0), (o1, l1), (o2, l2) = dil
    in_specs = ([pl.BlockSpec((tm, DIL_OUT), row_in)] * 6
                + [pl.BlockSpec((tm, SB_W), row_in), pl.BlockSpec((tm, d), row_in), pl.BlockSpec((tm, d), row_in),
                   pl.BlockSpec((tm, d), row_in),
                   pl.BlockSpec((1, 6, d), lambda i: ((i + off) // per_seq, 0, 0)),
                   pl.BlockSpec((1, d), const), pl.BlockSpec((1, d), const),
                   pl.BlockSpec(wa.shape, const), pl.BlockSpec(wb.shape, const), pl.BlockSpec(wo.shape, const),
                   pl.BlockSpec((N_EXPERTS, d), const), pl.BlockSpec((N_EXPERTS, 1), const)])
    out_specs = [pl.BlockSpec((tm, d), row), pl.BlockSpec((tm, d // 2), row),
                 pl.BlockSpec((TOP_K, tm), colblk), pl.BlockSpec((TOP_K, tm), colblk),
                 pl.BlockSpec((TOP_K, tm), colblk), pl.BlockSpec((N_EXPERTS, 1), const)]
    out_shape = [jax.ShapeDtypeStruct((t, d), F32), jax.ShapeDtypeStruct((t, d // 2), U32),
                 jax.ShapeDtypeStruct((TOP_K, t), I32), jax.ShapeDtypeStruct((TOP_K, t), F32),
                 jax.ShapeDtypeStruct((TOP_K, t), I32), jax.ShapeDtypeStruct((N_EXPERTS, 1), I32)]
    return pl.pallas_call(
        _postmix_kernel,
        grid=(nb,),
        in_specs=in_specs,
        out_specs=out_specs,
        out_shape=out_shape,
        scratch_shapes=[pltpu.VMEM((N_EXPERTS, 1), F32), pltpu.VMEM((tm, tm), BF16)],
        compiler_params=_cparams(("arbitrary",)),
        name="post_mix",
    )(o0, o1, o2, l0, l1, l2, sb, ga, gb, x2, mod3, gpost.reshape(1, d), gffn.reshape(1, d),
      wa, wb, wo, w_router.T, b_router.reshape(N_EXPERTS, 1))


def _slots_kernel(start_ref, idx_ref, rank_ref, slot_ref):
    idx = idx_ref[...]
    base = jnp.zeros_like(idx)
    for e in range(N_EXPERTS):
        base = jnp.where(idx == e, start_ref[e], base)
    slot_ref[...] = base + rank_ref[...]


def _slots(pad_start, idx, rank):
    k, t = idx.shape
    tb = min(t, 8192)
    blk = pl.BlockSpec((k, tb), lambda i, s: (0, i))
    return pl.pallas_call(
        _slots_kernel,
        grid_spec=pltpu.PrefetchScalarGridSpec(
            num_scalar_prefetch=1, grid=(t // tb,), in_specs=[blk, blk], out_specs=blk),
        out_shape=jax.ShapeDtypeStruct((k, t), I32),
        compiler_params=_cparams(("parallel",)),
        name="slots",
    )(pad_start, idx, rank)


def _sc_mesh():
    return plsc.VectorSubcoreMesh(core_axis_name="c", subcore_axis_name="s")


def _sc_dispatch(rows, slot, n_slots):
    t, d = rows.shape
    chunk = SC_DISPATCH_CHUNK
    per_worker = t // SC_WORKERS
    n_chunks = per_worker // chunk

    @functools.partial(
        pl.kernel, mesh=_sc_mesh(),
        out_type=jax.ShapeDtypeStruct((n_slots, d), rows.dtype),
        scratch_types=[pltpu.VMEM((TOP_K, chunk), I32), pltpu.VMEM((chunk, d), rows.dtype),
                       pltpu.SemaphoreType.DMA],
        name="dispatch",
    )
    def run(rows_hbm, slot_hbm, out_hbm, idx_v, rows_v, sem):
        wid = lax.axis_index("s") * SC_CORES + lax.axis_index("c")

        @pl.loop(0, n_chunks)
        def _(ci):
            base = pl.multiple_of(wid * per_worker + ci * chunk, chunk)
            loads = [pltpu.make_async_copy(rows_hbm.at[pl.ds(base, chunk)], rows_v, sem)]
            loads += [pltpu.make_async_copy(slot_hbm.at[kk, pl.ds(base, chunk)], idx_v.at[kk], sem)
                      for kk in range(TOP_K)]
            for cp in loads:
                cp.start()
            for cp in loads:
                cp.wait()
            copies = [pltpu.make_async_copy(rows_v, out_hbm.at[idx_v.at[kk]], sem) for kk in range(TOP_K)]
            for cp in copies:
                cp.start()
            for cp in copies:
                cp.wait()

    return run(rows, slot)


def _sc_combine(ys, slot):
    _, d = ys.shape
    k, t = slot.shape
    chunk = SC_COMBINE_CHUNK
    per_worker = t // SC_WORKERS
    n_chunks = per_worker // chunk

    @functools.partial(
        pl.kernel, mesh=_sc_mesh(),
        out_type=jax.ShapeDtypeStruct((k, t, d), ys.dtype),
        scratch_types=[pltpu.VMEM((TOP_K, chunk), I32), pltpu.VMEM((TOP_K, chunk, d), ys.dtype),
                       pltpu.SemaphoreType.DMA],
        name="combine",
    )
    def run(ys_hbm, slot_hbm, out_hbm, idx_v, rows_v, sem):
        wid = lax.axis_index("s") * SC_CORES + lax.axis_index("c")

        @pl.loop(0, n_chunks)
        def _(ci):
            base = pl.multiple_of(wid * per_worker + ci * chunk, chunk)
            loads = [pltpu.make_async_copy(slot_hbm.at[kk, pl.ds(base, chunk)], idx_v.at[kk], sem)
                     for kk in range(TOP_K)]
            for cp in loads:
                cp.start()
            for cp in loads:
                cp.wait()
            gathers = [pltpu.make_async_copy(ys_hbm.at[idx_v.at[kk]], rows_v.at[kk], sem) for kk in range(TOP_K)]
            for cp in gathers:
                cp.start()
            for cp in gathers:
                cp.wait()
            stores = [pltpu.make_async_copy(rows_v.at[kk], out_hbm.at[kk, pl.ds(base, chunk)], sem)
                      for kk in range(TOP_K)]
            for cp in stores:
                cp.start()
            for cp in stores:
                cp.wait()

    return run(ys, slot)


def _experts_kernel(be_ref, nused_ref, x_ref, *refs, prepared):
    if prepared:
        glu_hbm, lin_hbm, bg_ref, bl_ref, down_hbm, bd_ref, y_ref, glu_buf, lin_buf, down_buf, sem = refs
        streams = ((glu_hbm, glu_buf), (lin_hbm, lin_buf), (down_hbm, down_buf))
    else:
        wu_hbm, bg_ref, bl_ref, wd_hbm, bd_ref, y_ref, glu_w_ref, lin_w_ref, down_w_ref, wu_buf, wd_buf, sem = refs
        streams = ((wu_hbm, wu_buf), (wd_hbm, wd_buf))
    i = pl.program_id(0)
    live = i < nused_ref[0]
    e = be_ref[i]
    slot = e % 2
    new_expert = (i == 0) | (e != be_ref[jnp.maximum(i - 1, 0)])

    def copies(expert, into):
        return [pltpu.make_async_copy(hbm.at[expert], buf.at[into], sem.at[into, n])
                for n, (hbm, buf) in enumerate(streams)]

    @pl.when(i == 0)
    def _():
        for cp in copies(e, slot):
            cp.start()

    @pl.when(live & new_expert)
    def _():
        for cp in copies(e, slot):
            cp.wait()

        @pl.when(e + 1 < N_EXPERTS)
        def _():
            for cp in copies(e + 1, 1 - slot):
                cp.start(priority=1)

        if not prepared:
            tile = 2 * LANES
            src = lax.broadcasted_iota(I32, (tile, tile), 0)
            dst = lax.broadcasted_iota(I32, (tile, tile), 1)
            wanted = jnp.where(dst < LANES, 2 * dst, 2 * (dst - LANES) + 1)
            pick = jnp.where(src == wanted, 1.0, 0.0).astype(BF16)
            for c in range(wu_buf.shape[2] // tile):
                w = wu_buf[slot, :, c * tile:(c + 1) * tile].astype(BF16)
                both = jnp.dot(w, pick, preferred_element_type=F32)
                glu_w_ref[0, :, c * LANES:(c + 1) * LANES] = both[:, :LANES].astype(BF16)
                lin_w_ref[0, :, c * LANES:(c + 1) * LANES] = both[:, LANES:].astype(BF16)
            down_w_ref[0] = wd_buf[slot].astype(BF16)

    @pl.when(live)
    def _():
        if prepared:
            glu_w, lin_w, down_w = glu_buf[slot], lin_buf[slot], down_buf[slot]
        else:
            glu_w, lin_w, down_w = glu_w_ref[0], lin_w_ref[0], down_w_ref[0]
        lo, hi = _unpack_bf16_pairs(x_ref[...])
        x = jnp.concatenate([lo, hi], axis=1).astype(BF16)
        glu = jnp.dot(x, glu_w, preferred_element_type=F32) + bg_ref[0]
        lin = jnp.dot(x, lin_w, preferred_element_type=F32) + bl_ref[0]
        glu = jnp.minimum(glu, SWIGLU_LIMIT)
        lin = jnp.clip(lin, -SWIGLU_LIMIT, SWIGLU_LIMIT)
        act = glu * jax.nn.sigmoid(SWIGLU_ALPHA * glu) * (lin + 1.0)
        y = jnp.dot(act.astype(BF16), down_w, preferred_element_type=F32) + bd_ref[0]
        y_ref[...] = _pack_bf16_pairs(y)


def _experts(xs, block_expert, n_used, b_glu, b_lin, b_down, raw=None, prepared=None):
    n_slots, half_d = xs.shape
    e, _, f = b_glu.shape
    d = b_down.shape[2]
    nblk = n_slots // MOE_ROWS
    rows = lambda i, be, nu: (i, 0)
    by_e = lambda i, be, nu: (be[i], 0, 0)
    x_spec = pl.BlockSpec((MOE_ROWS, half_d), rows)
    bias_f, bias_d = pl.BlockSpec((1, 1, f), by_e), pl.BlockSpec((1, 1, d), by_e)
    up_bf16, down_bf16 = pl.BlockSpec((1, d, f), by_e), pl.BlockSpec((1, f, d), by_e)
    ys_shape = jax.ShapeDtypeStruct((n_slots, half_d), U32)
    in_hbm = pl.BlockSpec(memory_space=pl.ANY)
    if prepared is not None:
        in_specs = [x_spec, in_hbm, in_hbm, bias_f, bias_f, in_hbm, bias_d]
        args = (xs, prepared[0], prepared[1], b_glu, b_lin, prepared[2], b_down)
        out_specs, out_shape = x_spec, ys_shape
        scratch = [pltpu.VMEM((2, d, f), BF16), pltpu.VMEM((2, d, f), BF16), pltpu.VMEM((2, f, d), BF16),
                   pltpu.SemaphoreType.DMA((2, 3))]
    else:
        w_up, w_down = raw
        in_specs = [x_spec, in_hbm, bias_f, bias_f, in_hbm, bias_d]
        args = (xs, w_up, b_glu, b_lin, w_down, b_down)
        out_specs = [x_spec, up_bf16, up_bf16, down_bf16]
        out_shape = [ys_shape, jax.ShapeDtypeStruct((e, d, f), BF16), jax.ShapeDtypeStruct((e, d, f), BF16),
                     jax.ShapeDtypeStruct((e, f, d), BF16)]
        scratch = [pltpu.VMEM((2, d, 2 * f), F32), pltpu.VMEM((2, f, d), F32), pltpu.SemaphoreType.DMA((2, 2))]
    out = pl.pallas_call(
        functools.partial(_experts_kernel, prepared=prepared is not None),
        grid_spec=pltpu.PrefetchScalarGridSpec(
            num_scalar_prefetch=2, grid=(nblk,), in_specs=in_specs, out_specs=out_specs, scratch_shapes=scratch),
        out_shape=out_shape,
        compiler_params=_cparams(("arbitrary",)),
        name="experts",
    )(block_expert, n_used, *args)
    return out if prepared is not None else (out[0], tuple(out[1:]))


def _final_kernel(g_ref, gate_ref, x1_ref, mod_ref, gain_ref, *rest):
    o_ref = rest[-1]
    tm = x1_ref.shape[0]
    gates = gate_ref[...]
    padded = jnp.concatenate([gates, jnp.zeros((LANES - TOP_K, tm), F32)], axis=0)
    gate_cols = padded.T
    y_lo = jnp.zeros(g_ref.shape[1:], F32)
    y_hi = jnp.zeros(g_ref.shape[1:], F32)
    for kk in range(TOP_K):
        lo, hi = _unpack_bf16_pairs(g_ref[kk])
        y_lo = y_lo + lo * gate_cols[:, kk:kk + 1]
        y_hi = y_hi + hi * gate_cols[:, kk:kk + 1]
    y = jnp.concatenate([y_lo, y_hi], axis=1)
    o_ref[...] = x1_ref[...] + mod_ref[0, 5:6, :] * _rms(y, gain_ref[...])


def _final(g, gates, x1, mod3, gain, seq, part, out_so_far):
    t, d = x1.shape
    tm = ROW_BLOCK
    nb = t // tm
    off = part * nb
    per_seq = seq // tm
    row = lambda i: (i, 0)
    in_specs = [pl.BlockSpec((TOP_K, tm, d // 2), lambda i: (0, i, 0)),
                pl.BlockSpec((TOP_K, tm), lambda i: (0, i)),
                pl.BlockSpec((tm, d), row),
                pl.BlockSpec((1, 6, d), lambda i: ((i + off) // per_seq, 0, 0)),
                pl.BlockSpec((1, d), lambda i: (0, 0))]
    args = [g, gates, x1, mod3, gain.reshape(1, d)]
    aliases = {}
    if out_so_far is not None:
        in_specs.append(pl.BlockSpec(memory_space=pl.ANY))
        args.append(out_so_far)
        aliases = {len(args) - 1: 0}
    return pl.pallas_call(
        _final_kernel,
        grid=(nb,),
        in_specs=in_specs,
        out_specs=pl.BlockSpec((tm, d), lambda i: (i + off, 0)),
        out_shape=jax.ShapeDtypeStruct((t * MOE_PARTS, d), F32),
        input_output_aliases=aliases,
        compiler_params=_cparams(("parallel",)),
        name="final",
    )(*args)


def _layer(x2, c, positions, seq, ada_w, ada_b, norm_mix_pre, norm_mix_post, norm_ffn_pre, norm_ffn_post,
           w_in, w_branch_a, w_branch_b, w_out, w_router, b_router, w_up, b_up, w_down, b_down):
    t, d = x2.shape
    batch = t // seq
    mod3 = _adaln(c, ada_w, ada_b).reshape(batch, 6, d)

    qa, ka, va, qb, kb, vb, ga, gb = _in_proj(x2, mod3, norm_mix_pre, positions, w_in.astype(BF16), seq)
    dil = [_dilated_group(qa, ka, va, batch, seq, g) for g in range(len(DIL_GROUPS))]
    sb = _stickbreak(qb, kb, vb, batch, seq)
    wa, wb, wo = w_branch_a.astype(BF16), w_branch_b.astype(BF16), w_out.astype(BF16)
    f = w_up.shape[2] // 2
    b_glu, b_lin = b_up[:, 0::2].reshape(N_EXPERTS, 1, f), b_up[:, 1::2].reshape(N_EXPERTS, 1, f)
    b_down3 = b_down.reshape(N_EXPERTS, 1, d)
    nblk = (t // MOE_PARTS * TOP_K) // MOE_ROWS + N_EXPERTS

    routed = []
    for part in range(MOE_PARTS):
        x1, h2, idx, gates, rank, counts = _post_mix(
            dil, sb, ga, gb, x2, mod3, norm_mix_post, norm_ffn_pre, wa, wb, wo, w_router, b_router, seq, part)
        counts = counts.reshape(N_EXPERTS)
        padded = jnp.maximum((counts + MOE_ROWS - 1) // MOE_ROWS, 1) * MOE_ROWS
        pad_end = jnp.cumsum(padded)
        pad_start = (pad_end - padded).astype(I32)
        block_first_row = jnp.arange(nblk, dtype=I32)[:, None] * MOE_ROWS
        block_expert = jnp.minimum(jnp.sum(pad_end[None, :] <= block_first_row, axis=1), N_EXPERTS - 1).astype(I32)
        n_used = (pad_end[-1:] // MOE_ROWS).astype(I32)
        slot = _slots(pad_start, idx, rank)
        xs = _sc_dispatch(h2, slot, nblk * MOE_ROWS)
        routed.append((x1, gates, slot, xs, block_expert, n_used))

    gathered = []
    prepared = None
    for x1, gates, slot, xs, block_expert, n_used in routed:
        if prepared is None:
            ys, prepared = _experts(xs, block_expert, n_used, b_glu, b_lin, b_down3, raw=(w_up, w_down))
        else:
            ys = _experts(xs, block_expert, n_used, b_glu, b_lin, b_down3, prepared=prepared)
        gathered.append(_sc_combine(ys, slot))

    out = None
    for part, ((x1, gates, *_), g) in enumerate(zip(routed, gathered)):
        out = _final(g, gates, x1, mod3, norm_ffn_post, seq, part, out)
    return out


def kernel(x, c, positions, ada_w, ada_b, norm_mix_pre, norm_mix_post, norm_ffn_pre, norm_ffn_post,
           w_in, w_branch_a, w_branch_b, w_out, w_router, b_router, w_up, b_up, w_down, b_down):
    batch, seq, d = x.shape
    tokens = batch * seq
    assert w_in.shape[-1] == 3 * DIL_W + 3 * SB_W + 2 * d and d % (2 * LANES) == 0
    assert w_router.shape[-1] == N_EXPERTS and w_up.shape[-1] == 2 * w_down.shape[-2]
    assert seq % ROW_BLOCK == 0 and seq % (SB_TQ * SB_SUB) == 0
    assert all(seq % max(Q_BLOCK * r, DIL_TOKENS) == 0 for _, r in DIL_GROUPS)
    assert tokens % (MOE_PARTS * ROW_BLOCK) == 0
    assert tokens % (MOE_PARTS * SC_WORKERS * max(SC_DISPATCH_CHUNK, SC_COMBINE_CHUNK)) == 0
    x2 = x.reshape(tokens, d)
    for layer in range(ada_w.shape[0]):
        x2 = _layer(x2, c, positions, seq, ada_w[layer], ada_b[layer], norm_mix_pre[layer], norm_mix_post[layer],
                    norm_ffn_pre[layer], norm_ffn_post[layer], w_in[layer], w_branch_a[layer], w_branch_b[layer],
                    w_out[layer], w_router[layer], b_router[layer], w_up[layer], b_up[layer], w_down[layer],
                    b_down[layer])
    return x2.reshape(batch, seq, d)
```
